```python
import math
import jax, jax.numpy as jnp
from jax import lax
import numpy as np

D_MODEL = 2048
BATCH = 32
SEQ = 256
DEPTH = 2
DEC_BATCH = 8
DEC_SEQ = 2048
PAST_LEN = 512

GRID_W = 64
NORM_EPS = 1e-6
ATTN_HEADS = 8
ATTN_QK_DIM = D_MODEL // (2 * ATTN_HEADS)
ATTN_V_DIM = 2 * ATTN_QK_DIM
ROPE_THETA = 10000.0
Q_BLOCK = 128
SSD_D_INNER = 2 * D_MODEL
SSD_HEAD_DIM = 64
SSD_HEADS = SSD_D_INNER // SSD_HEAD_DIM
SSD_GROUPS = 8
SSD_STATE = 128
SSD_CONV = 5
SSD_CHUNK = 128
SSD_BC_DIM = SSD_GROUPS * SSD_STATE
SSD_CONV_CH = SSD_D_INNER + 2 * SSD_BC_DIM
SSD_IN_DIM = SSD_D_INNER + SSD_CONV_CH + 2 * SSD_HEADS
N_EXPERTS = 32
TOP_K = 4
D_FF = D_MODEL
SWIGLU_LIMIT = 7.0
SWIGLU_ALPHA = 1.702
MOE_BLOCK = 256
N_ATTN_LAYERS = (DEPTH + 1) // 2
N_SSD_LAYERS = DEPTH // 2

kernel_name = "hybrid_diffattn_ssd_moe_prefix_denoise_step"


def rms_norm(x, w):
    xf = x.astype(jnp.float32)
    y = xf * lax.rsqrt(jnp.mean(xf * xf, axis=-1, keepdims=True) + NORM_EPS)
    return (y * w.astype(jnp.float32)).astype(x.dtype)


def ada_mod(cond, w, b):
    m = jax.nn.silu(cond) @ w + b
    return jnp.split(m[:, None, :], 6, axis=-1)


def axial_rope_tables(seq_len):
    rows = seq_len // GRID_W
    row = jnp.repeat(jnp.arange(rows, dtype=jnp.float32), GRID_W)
    col = jnp.tile(jnp.arange(GRID_W, dtype=jnp.float32), rows)
    axis_dim = ATTN_QK_DIM // 2
    inv = ROPE_THETA ** (-jnp.arange(0, axis_dim, 2, dtype=jnp.float32) / axis_dim)
    ang = jnp.stack([row[:, None] * inv, col[:, None] * inv], axis=1)
    return jnp.cos(ang), jnp.sin(ang)


def apply_axial_rope(x, cos, sin):
    half = ATTN_QK_DIM // 4
    xr = x.reshape(x.shape[:-1] + (2, 2, half)).astype(jnp.float32)
    x1, x2 = xr[..., 0, :], xr[..., 1, :]
    c = cos[None, :, None, None]
    s = sin[None, :, None, None]
    out = jnp.stack([x1 * c - x2 * s, x2 * c + x1 * s], axis=-2)
    return out.reshape(x.shape).astype(x.dtype)


def diff_qkv(h, w_qkv, q_norm_w, k_norm_w):
    b, l, _ = h.shape
    q, k, v = jnp.split(h @ w_qkv, 3, axis=-1)
    q = rms_norm(q.reshape(b, l, ATTN_HEADS, 2, ATTN_QK_DIM), q_norm_w)
    k = rms_norm(k.reshape(b, l, ATTN_HEADS, 2, ATTN_QK_DIM), k_norm_w)
    return q, k, v.reshape(b, l, ATTN_HEADS, ATTN_V_DIM)


def diff_attention(q, k, v, lam):
    b, lq = q.shape[:2]
    nblk = lq // Q_BLOCK
    qb = jnp.moveaxis(q.reshape((b, nblk, Q_BLOCK) + q.shape[2:]), 1, 0)
    scale = ATTN_QK_DIM ** -0.5

    def block(qblk):
        s = jnp.einsum('bqhmd,bkhmd->bhmqk', qblk, k).astype(jnp.float32) * scale
        p = jax.nn.softmax(s, axis=-1)
        w = p[:, :, 0] - lam * p[:, :, 1]
        return jnp.einsum('bhqk,bkhd->bqhd', w.astype(v.dtype), v)

    o = lax.map(block, qb)
    return jnp.moveaxis(o, 0, 1).reshape(b, lq, ATTN_HEADS, ATTN_V_DIM)


def diff_out(o, subln_w, w_o, lam_init):
    o = rms_norm(o, subln_w) * (1.0 - lam_init)
    return o.reshape(o.shape[:2] + (-1,)) @ w_o


def ssd_chunked(x, a, bm, cm, h0):
    f32 = jnp.float32
    b, L = x.shape[:2]
    T = L // SSD_CHUNK
    R = SSD_HEADS // SSD_GROUPS
    x = x.astype(f32).reshape(b, T, SSD_CHUNK, SSD_GROUPS, R, SSD_HEAD_DIM)
    a = a.astype(f32).reshape(b, T, SSD_CHUNK, SSD_GROUPS, R)
    bm = bm.astype(f32).reshape(b, T, SSD_CHUNK, SSD_GROUPS, SSD_STATE)
    cm = cm.astype(f32).reshape(b, T, SSD_CHUNK, SSD_GROUPS, SSD_STATE)
    a_cs = jnp.cumsum(a, axis=2)
    lower = jnp.tril(jnp.ones((SSD_CHUNK, SSD_CHUNK), dtype=bool))[None, None, :, :, None, None]
    seg = a_cs[:, :, :, None] - a_cs[:, :, None, :]
    decay = jnp.exp(jnp.where(lower, seg, -jnp.inf))
    cb = jnp.einsum('btlgn,btsgn->btlsg', cm, bm)
    y_diag = jnp.einsum('btlsgr,btsgrp->btlgrp', cb[..., None] * decay, x)
    decay_end = jnp.exp(a_cs[:, :, -1:] - a_cs)
    chunk_states = jnp.einsum('btsgn,btsgrp->btgrpn', bm, decay_end[..., None] * x)
    chunk_decay = jnp.exp(a_cs[:, :, -1])

    def step(h, inp):
        dec, st = inp
        return dec[..., None, None] * h + st, h

    h0g = h0.astype(f32).reshape(b, SSD_GROUPS, R, SSD_HEAD_DIM, SSD_STATE)
    final, h_in = lax.scan(step, h0g, (jnp.moveaxis(chunk_decay, 1, 0), jnp.moveaxis(chunk_states, 1, 0)))
    h_in = jnp.moveaxis(h_in, 0, 1)
    y_off = jnp.einsum('btlgn,btgrpn->btlgrp', cm, h_in) * jnp.exp(a_cs)[..., None]
    y = (y_diag + y_off).reshape(b, L, SSD_HEADS, SSD_HEAD_DIM)
    return y, final.reshape(b, SSD_HEADS, SSD_HEAD_DIM, SSD_STATE)


def ssd_mixer(h, h0, w_in, conv_w, conv_b, dt_bias, a_log, d_skip, norm_w, w_out):
    f32 = jnp.float32
    b, L, _ = h.shape
    zxbcdt = h @ w_in
    z = zxbcdt[..., :SSD_D_INNER]
    xbc = zxbcdt[..., SSD_D_INNER:SSD_D_INNER + SSD_CONV_CH]
    dt_raw = zxbcdt[..., SSD_D_INNER + SSD_CONV_CH:]
    xbc = lax.conv_general_dilated(xbc, conv_w[:, None, :], window_strides=(1,),
                                   padding=[(SSD_CONV // 2, SSD_CONV // 2)],
                                   dimension_numbers=('NWC', 'WIO', 'NWC'),
                                   feature_group_count=SSD_CONV_CH)
    xbc = jax.nn.silu(xbc + conv_b)
    xs = xbc[..., :SSD_D_INNER].reshape(b, L, SSD_HEADS, SSD_HEAD_DIM).astype(f32)
    bm = xbc[..., SSD_D_INNER:SSD_D_INNER + SSD_BC_DIM].reshape(b, L, SSD_GROUPS, SSD_STATE)
    cm = xbc[..., SSD_D_INNER + SSD_BC_DIM:].reshape(b, L, SSD_GROUPS, SSD_STATE)
    dt = jax.nn.softplus(dt_raw.astype(f32).reshape(b, L, 2, SSD_HEADS) + dt_bias.astype(f32))
    a_neg = -jnp.exp(a_log.astype(f32))
    y_f, s_f = ssd_chunked(xs * dt[:, :, 0, :, None], dt[:, :, 0] * a_neg[0], bm, cm, h0[:, 0])
    flip = lambda t: jnp.flip(t, axis=1)
    y_b, s_b = ssd_chunked(flip(xs * dt[:, :, 1, :, None]), flip(dt[:, :, 1] * a_neg[1]),
                           flip(bm), flip(cm), h0[:, 1])
    d_tot = (d_skip[0].astype(f32) + d_skip[1].astype(f32))[:, None]
    y = y_f + flip(y_b) + d_tot * xs
    y = y * jax.nn.silu(z.astype(f32)).reshape(b, L, SSD_HEADS, SSD_HEAD_DIM)
    y = rms_norm(y.reshape(b, L, SSD_GROUPS, SSD_D_INNER // SSD_GROUPS),
                 norm_w.reshape(SSD_GROUPS, SSD_D_INNER // SSD_GROUPS))
    y = y.reshape(b, L, SSD_D_INNER).astype(h.dtype)
    return y @ w_out, jnp.stack([s_f, s_b], axis=1).astype(h.dtype)


def moe_ffn(h, w_router, b_router, w_gate_up, b_gate_up, w_down, b_down):
    n, d = h.shape
    logits = (h @ w_router).astype(jnp.float32) + b_router.astype(jnp.float32)
    top_val, top_idx = lax.top_k(logits, TOP_K)
    gates = jax.nn.softmax(top_val, axis=-1).astype(h.dtype)
    n_assign = n * TOP_K
    flat_e = top_idx.reshape(-1)
    order = jnp.argsort(flat_e)
    sorted_e = flat_e[order]
    counts = jnp.bincount(flat_e, length=N_EXPERTS)
    padded = (counts + MOE_BLOCK - 1) // MOE_BLOCK * MOE_BLOCK
    start = jnp.cumsum(counts) - counts
    pad_end = jnp.cumsum(padded)
    pad_start = pad_end - padded
    dest = pad_start[sorted_e] + jnp.arange(n_assign) - start[sorted_e]
    n_blocks = -(-n_assign // MOE_BLOCK) + N_EXPERTS
    slots = n_blocks * MOE_BLOCK
    slot_tok = jnp.zeros((slots,), jnp.int32).at[dest].set((order // TOP_K).astype(jnp.int32))
    slot_gate = jnp.zeros((slots,), h.dtype).at[dest].set(gates.reshape(-1)[order])
    block_e = jnp.minimum(jnp.searchsorted(pad_end, jnp.arange(n_blocks) * MOE_BLOCK, side='right'),
                          N_EXPERTS - 1)

    def expert_block(args):
        xb, gb, e = args
        gu = xb @ w_gate_up[e] + b_gate_up[e]
        g = jnp.minimum(gu[:, :D_FF], SWIGLU_LIMIT)
        u = jnp.clip(gu[:, D_FF:], -SWIGLU_LIMIT, SWIGLU_LIMIT)
        act = (u + 1.0) * (g * jax.nn.sigmoid(SWIGLU_ALPHA * g))
        return (act @ w_down[e] + b_down[e]) * gb[:, None]

    yb = lax.map(expert_block, (h[slot_tok].reshape(n_blocks, MOE_BLOCK, d),
                                slot_gate.reshape(n_blocks, MOE_BLOCK), block_e))
    return jax.ops.segment_sum(yb.reshape(slots, d), slot_tok, num_segments=n)


def setup_inputs(seed: int = 0) -> dict:
    key = jax.random.key(seed)
    ks = iter(jax.random.split(key, 48))
    f32 = jnp.float32

    def nrm(shape, scale):
        return jax.random.normal(next(ks), shape, f32) * scale

    D = D_MODEL
    NA, NS = N_ATTN_LAYERS, N_SSD_LAYERS
    H, P, N = SSD_HEADS, SSD_HEAD_DIM, SSD_STATE
    dt0 = jnp.exp(jax.random.uniform(next(ks), (NS, 2, H), f32, math.log(1e-3), math.log(1e-1)))
    dt_bias = dt0 + jnp.log(-jnp.expm1(-dt0))
    a_log = jnp.log(jax.random.uniform(next(ks), (NS, 2, H), f32, 1.0, 16.0))
    return {
        'x_prompt': nrm((BATCH, SEQ, D), 1.0),
        'x_sample': nrm((DEC_BATCH, DEC_SEQ, D), 1.0),
        'c': nrm((DEC_BATCH, D), 1.0),
        'c_ctx': nrm((D,), 1.0),
        'cache_k': nrm((DEC_BATCH, NA, PAST_LEN, ATTN_HEADS, 2, ATTN_QK_DIM), 1.0),
        'cache_v': nrm((DEC_BATCH, NA, PAST_LEN, ATTN_HEADS, ATTN_V_DIM), 1.0),
        'state_ssm': nrm((DEC_BATCH, NS, 2, H, P, N), 0.1),
        'norm1_w': 1.0 + nrm((DEPTH, D), 0.02),
        'norm2_w': 1.0 + nrm((DEPTH, D), 0.02),
        'w_ada': nrm((DEPTH, D, 6 * D), 0.5 * D ** -0.5),
        'b_ada': nrm((DEPTH, 6 * D), 0.01),
        'w_qkv': nrm((NA, D, 3 * D), D ** -0.5),
        'q_norm_w': 1.0 + nrm((NA, ATTN_QK_DIM), 0.02),
        'k_norm_w': 1.0 + nrm((NA, ATTN_QK_DIM), 0.02),
        'lambda_q1': nrm((NA, ATTN_QK_DIM), 0.1),
        'lambda_k1': nrm((NA, ATTN_QK_DIM), 0.1),
        'lambda_q2': nrm((NA, ATTN_QK_DIM), 0.1),
        'lambda_k2': nrm((NA, ATTN_QK_DIM), 0.1),
        'subln_w': 1.0 + nrm((NA, ATTN_V_DIM), 0.02),
        'w_o': nrm((NA, D, D), D ** -0.5),
        'w_in_ssd': nrm((NS, D, SSD_IN_DIM), D ** -0.5),
        'conv_w': nrm((NS, SSD_CONV, SSD_CONV_CH), SSD_CONV ** -0.5),
        'conv_b': nrm((NS, SSD_CONV_CH), 0.01),
        'dt_bias': dt_bias,
        'a_log': a_log,
        'd_skip': 1.0 + nrm((NS, 2, H), 0.02),
        'ssd_norm_w': 1.0 + nrm((NS, SSD_D_INNER), 0.02),
        'w_out_ssd': nrm((NS, SSD_D_INNER, D), SSD_D_INNER ** -0.5),
        'w_router': nrm((DEPTH, D, N_EXPERTS), D ** -0.5),
        'b_router': nrm((DEPTH, N_EXPERTS), 0.01),
        'w_gate_up': nrm((DEPTH, N_EXPERTS, D, 2 * D_FF), D ** -0.5),
        'b_gate_up': nrm((DEPTH, N_EXPERTS, 2 * D_FF), 0.01),
        'w_down': nrm((DEPTH, N_EXPERTS, D_FF, D), D_FF ** -0.5),
        'b_down': nrm((DEPTH, N_EXPERTS, D), 0.01),
    }


def reference(x_prompt, x_sample, c, c_ctx, cache_k, cache_v, state_ssm,
              norm1_w, norm2_w, w_ada, b_ada,
              w_qkv, q_norm_w, k_norm_w, lambda_q1, lambda_k1, lambda_q2, lambda_k2, subln_w, w_o,
              w_in_ssd, conv_w, conv_b, dt_bias, a_log, d_skip, ssd_norm_w, w_out_ssd,
              w_router, b_router, w_gate_up, b_gate_up, w_down, b_down):
    f32 = jnp.float32
    xp, xs = x_prompt, x_sample
    cos, sin = axial_rope_tables(xs.shape[1])
    new_k, new_v, new_s = [], [], []
    for i in range(DEPTH):
        sh1p, sc1p, g1p, sh2p, sc2p, g2p = ada_mod(c_ctx[None], w_ada[i], b_ada[i])
        sh1s, sc1s, g1s, sh2s, sc2s, g2s = ada_mod(c, w_ada[i], b_ada[i])
        hp = rms_norm(xp, norm1_w[i]) * (1.0 + sc1p) + sh1p
        hs = rms_norm(xs, norm1_w[i]) * (1.0 + sc1s) + sh1s
        j = i // 2
        if i % 2 == 0:
            lam_init = 0.8 - 0.6 * math.exp(-0.3 * i)
            lam = (jnp.exp(jnp.sum(lambda_q1[j].astype(f32) * lambda_k1[j].astype(f32)))
                   - jnp.exp(jnp.sum(lambda_q2[j].astype(f32) * lambda_k2[j].astype(f32))) + lam_init)
            qp, kp, vp = diff_qkv(hp, w_qkv[j], q_norm_w[j], k_norm_w[j])
            new_k.append(kp)
            new_v.append(vp)
            mp = diff_out(diff_attention(qp, kp, vp, lam), subln_w[j], w_o[j], lam_init)
            qs, k_lat, v_lat = diff_qkv(hs, w_qkv[j], q_norm_w[j], k_norm_w[j])
            qs = apply_axial_rope(qs, cos, sin)
            k_lat = apply_axial_rope(k_lat, cos, sin)
            k_all = jnp.concatenate([cache_k[:, j].astype(k_lat.dtype), k_lat], axis=1)
            v_all = jnp.concatenate([cache_v[:, j].astype(v_lat.dtype), v_lat], axis=1)
            ms = diff_out(diff_attention(qs, k_all, v_all, lam), subln_w[j], w_o[j], lam_init)
        else:
            h0 = jnp.zeros((xp.shape[0], 2, SSD_HEADS, SSD_HEAD_DIM, SSD_STATE), xp.dtype)
            mp, sp = ssd_mixer(hp, h0, w_in_ssd[j], conv_w[j], conv_b[j], dt_bias[j], a_log[j],
                               d_skip[j], ssd_norm_w[j], w_out_ssd[j])
            new_s.append(sp)
            ms, _ = ssd_mixer(hs, state_ssm[:, j].astype(xs.dtype), w_in_ssd[j], conv_w[j], conv_b[j],
                              dt_bias[j], a_log[j], d_skip[j], ssd_norm_w[j], w_out_ssd[j])
        xp = xp + g1p * mp
        xs = xs + g1s * ms
        hp2 = rms_norm(xp, norm2_w[i]) * (1.0 + sc2p) + sh2p
        hs2 = rms_norm(xs, norm2_w[i]) * (1.0 + sc2s) + sh2s
        n_p = xp.shape[0] * xp.shape[1]
        tokens = jnp.concatenate([hp2.reshape(-1, D_MODEL), hs2.reshape(-1, D_MODEL)], axis=0)
        f = moe_ffn(tokens, w_router[i], b_router[i], w_gate_up[i], b_gate_up[i], w_down[i], b_down[i])
        xp = xp + g2p * f[:n_p].reshape(xp.shape)
        xs = xs + g2s * f[n_p:].reshape(xs.shape)
    return (xp, xs, jnp.stack(new_k, axis=1), jnp.stack(new_v, axis=1), jnp.stack(new_s, axis=1))
```

```python
import functools
import math

import jax
import jax.numpy as jnp
from jax import lax
from jax.experimental import pallas as pl
from jax.experimental.pallas import tpu as pltpu

F32 = jnp.float32
BF16 = jnp.bfloat16

D_MODEL = 2048
NORM_EPS = 1e-6
GRID_W = 64
ROPE_THETA = 10000.0
ATTN_HEADS = 8
QK_DIM = 128
V_DIM = 256
SSD_D_INNER = 4096
SSD_HEAD_DIM = 64
SSD_HEADS = 64
SSD_GROUPS = 8
SSD_STATE = 128
SSD_CONV = 5
SSD_CHUNK = 128
SSD_BC_DIM = SSD_GROUPS * SSD_STATE
SSD_CONV_CH = SSD_D_INNER + 2 * SSD_BC_DIM
N_EXPERTS = 32
TOP_K = 4
D_FF = 2048
SWIGLU_LIMIT = 7.0
SWIGLU_ALPHA = 1.702
MOE_BLOCK = 256
LANES = 128
GROUP_W = SSD_D_INNER // SSD_GROUPS
VMEM_LIMIT = 56 * 1024 * 1024


def _params(*sem):
    return pltpu.CompilerParams(dimension_semantics=sem, vmem_limit_bytes=VMEM_LIMIT)


def _group_of_tile(i, tm, n_p, ls):
    return jnp.where(i * tm < n_p, 0, 1 + (i * tm - n_p) // ls)


def _mod_spec(which, tm, tn, n_p, ls):
    return pl.BlockSpec((1, 1, tn), lambda i, j: (_group_of_tile(i, tm, n_p, ls) * 6 + which, 0, j))


def _mm_kernel(a_ref, w_ref, *rest, epilogue):
    acc = jnp.dot(a_ref[...].astype(BF16), w_ref[...].astype(BF16), preferred_element_type=F32)
    if epilogue == "bias":
        b_ref, o_ref = rest
        o_ref[...] = (acc + b_ref[...]).astype(o_ref.dtype)
    elif epilogue == "resid":
        x_ref, g_ref, o_ref = rest
        o_ref[...] = x_ref[...] + g_ref[0] * acc
    else:
        (o_ref,) = rest
        o_ref[...] = acc.astype(o_ref.dtype)


def _matmul(a, w, *, n_out, col_off=0, tm, tn, out_dtype, name, bias=None, resid=None):
    m, k = a.shape
    joff = col_off // tn
    in_specs = [pl.BlockSpec((tm, k), lambda i, j: (i, 0)),
                pl.BlockSpec((k, tn), lambda i, j: (0, j + joff))]
    args = [a, w]
    if bias is not None:
        epilogue = "bias"
        in_specs.append(pl.BlockSpec((1, tn), lambda i, j: (0, j)))
        args.append(bias)
    elif resid is not None:
        epilogue = "resid"
        x, mod, which, n_p, ls = resid
        in_specs += [pl.BlockSpec((tm, tn), lambda i, j: (i, j)), _mod_spec(which, tm, tn, n_p, ls)]
        args += [x, mod]
    else:
        epilogue = "plain"
    return pl.pallas_call(
        functools.partial(_mm_kernel, epilogue=epilogue),
        grid=(m // tm, n_out // tn),
        in_specs=in_specs,
        out_specs=pl.BlockSpec((tm, tn), lambda i, j: (i, j)),
        out_shape=jax.ShapeDtypeStruct((m, n_out), out_dtype),
        compiler_params=_params("parallel", "arbitrary"),
        name=name,
    )(*args)


def _modnorm(x, w, shift, scale):
    ms = jnp.mean(x * x, axis=-1, keepdims=True)
    return (x * lax.rsqrt(ms + NORM_EPS) * w) * (1.0 + scale) + shift


def _modnorm_kernel(x_ref, w_ref, sh_ref, sc_ref, o_ref):
    o_ref[...] = _modnorm(x_ref[...], w_ref[...], sh_ref[0], sc_ref[0]).astype(o_ref.dtype)


def _modnorm_call(x, w, mod, which_shift, n_p, ls, *, tm, name):
    m, d = x.shape
    return pl.pallas_call(
        _modnorm_kernel,
        grid=(m // tm, 1),
        in_specs=[pl.BlockSpec((tm, d), lambda i, j: (i, 0)),
                  pl.BlockSpec((1, d), lambda i, j: (0, 0)),
                  _mod_spec(which_shift, tm, d, n_p, ls),
                  _mod_spec(which_shift + 1, tm, d, n_p, ls)],
        out_specs=pl.BlockSpec((tm, d), lambda i, j: (i, 0)),
        out_shape=jax.ShapeDtypeStruct((m, d), BF16),
        compiler_params=_params("parallel", "arbitrary"),
        name=name,
    )(x, w, mod, mod)


def _split2(v):
    hi = v.astype(BF16)
    lo = (v - hi.astype(F32)).astype(BF16)
    return hi, lo


def _router_kernel(x_ref, w_ref, sh_ref, sc_ref, wr_ref, br_ref, h_ref, idx_ref, gate_ref):
    h = _modnorm(x_ref[...], w_ref[...], sh_ref[0], sc_ref[0])
    h_ref[...] = h.astype(BF16)
    h_hi, h_lo = _split2(h)
    w_hi, w_lo = _split2(wr_ref[...])
    logits = (jnp.dot(h_hi, w_hi, preferred_element_type=F32)
              + jnp.dot(h_hi, w_lo, preferred_element_type=F32)
              + jnp.dot(h_lo, w_hi, preferred_element_type=F32)) + br_ref[...]
    lane = lax.broadcasted_iota(jnp.int32, logits.shape, 1).astype(F32)
    cur = jnp.where(lane < N_EXPERTS, logits, -jnp.inf)
    vals, idxs = [], []
    for _ in range(TOP_K):
        m = jnp.max(cur, axis=-1, keepdims=True)
        am = jnp.min(jnp.where(cur == m, lane, float(LANES)), axis=-1, keepdims=True)
        vals.append(m)
        idxs.append(am)
        cur = jnp.where(lane == am, -jnp.inf, cur)
    exps = [jnp.exp(v - vals[0]) for v in vals]
    denom = exps[0] + exps[1] + exps[2] + exps[3]
    idx_out = jnp.zeros(logits.shape, F32)
    gate_out = jnp.zeros(logits.shape, F32)
    for k in range(TOP_K):
        idx_out = jnp.where(lane == k, idxs[k], idx_out)
        gate_out = jnp.where(lane == k, exps[k] / denom, gate_out)
    idx_ref[...] = idx_out.astype(jnp.int32)
    gate_ref[...] = gate_out


def _router_call(x, w, mod, n_p, ls, w_router, b_router, *, tm, name):
    m, d = x.shape
    wr = jnp.pad(w_router, ((0, 0), (0, LANES - N_EXPERTS)))
    br = jnp.pad(b_router, (0, LANES - N_EXPERTS)).reshape(1, LANES)
    return pl.pallas_call(
        _router_kernel,
        grid=(m // tm, 1),
        in_specs=[pl.BlockSpec((tm, d), lambda i, j: (i, 0)),
                  pl.BlockSpec((1, d), lambda i, j: (0, 0)),
                  _mod_spec(3, tm, d, n_p, ls),
                  _mod_spec(4, tm, d, n_p, ls),
                  pl.BlockSpec((d, LANES), lambda i, j: (0, 0)),
                  pl.BlockSpec((1, LANES), lambda i, j: (0, 0))],
        out_specs=[pl.BlockSpec((tm, d), lambda i, j: (i, 0)),
                   pl.BlockSpec((tm, LANES), lambda i, j: (i, 0)),
                   pl.BlockSpec((tm, LANES), lambda i, j: (i, 0))],
        out_shape=[jax.ShapeDtypeStruct((m, d), BF16),
                   jax.ShapeDtypeStruct((m, LANES), jnp.int32),
                   jax.ShapeDtypeStruct((m, LANES), F32)],
        compiler_params=_params("parallel", "arbitrary"),
        name=name,
    )(x, w, mod, mod, wr, br)


def _qkv_kernel(a_ref, w_ref, nw_ref, *rest, rope, tn):
    if rope:
        cos_ref, sin_ref, q_ref, k_ref, v_ref = rest
    else:
        q_ref, k_ref, v_ref = rest
    j = pl.program_id(1)
    nq = D_MODEL // tn
    acc = jnp.dot(a_ref[...], w_ref[...], preferred_element_type=F32)

    def normed(widx, out_scale):
        w = nw_ref[widx:widx + 1, :]
        outs = []
        for c in range(tn // QK_DIM):
            xc = acc[:, c * QK_DIM:(c + 1) * QK_DIM]
            ms = jnp.mean(xc * xc, axis=-1, keepdims=True)
            y = xc * lax.rsqrt(ms + NORM_EPS) * w
            if rope:
                lane = lax.broadcasted_iota(jnp.int32, y.shape, 1)
                partner = jnp.where(lane % 64 < 32, pltpu.roll(y, QK_DIM - 32, 1), pltpu.roll(y, 32, 1))
                y = y * cos_ref[...] + partner * sin_ref[...]
            outs.append(y * out_scale if out_scale != 1.0 else y)
        return jnp.concatenate(outs, axis=1)

    @pl.when(j < nq)
    def _():
        q_ref[...] = normed(0, QK_DIM ** -0.5).astype(q_ref.dtype)

    @pl.when((j >= nq) & (j < 2 * nq))
    def _():
        k_ref[...] = normed(1, 1.0).astype(k_ref.dtype)

    @pl.when(j >= 2 * nq)
    def _():
        v_ref[...] = acc.astype(v_ref.dtype)


def _qkv_call(h, w_qkv, qk_norm_w, rope_tabs, *, kv_dtype, tm, tn, ls, name):
    m, d = h.shape
    nq = d // tn
    rope = rope_tabs is not None
    in_specs = [pl.BlockSpec((tm, d), lambda i, j: (i, 0)),
                pl.BlockSpec((d, tn), lambda i, j: (0, j)),
                pl.BlockSpec((2, QK_DIM), lambda i, j: (0, 0))]
    args = [h, w_qkv, qk_norm_w]
    if rope:
        nt = ls // tm
        in_specs += [pl.BlockSpec((tm, QK_DIM), lambda i, j: (i % nt, 0))] * 2
        args += list(rope_tabs)
    return pl.pallas_call(
        functools.partial(_qkv_kernel, rope=rope, tn=tn),
        grid=(m // tm, 3 * nq),
        in_specs=in_specs,
        out_specs=[pl.BlockSpec((tm, tn), lambda i, j: (i, jnp.minimum(j, nq - 1))),
                   pl.BlockSpec((tm, tn), lambda i, j: (i, jnp.clip(j - nq, 0, nq - 1))),
                   pl.BlockSpec((tm, tn), lambda i, j: (i, jnp.clip(j - 2 * nq, 0, nq - 1)))],
        out_shape=[jax.ShapeDtypeStruct((m, d), BF16),
                   jax.ShapeDtypeStruct((m, d), kv_dtype),
                   jax.ShapeDtypeStruct((m, d), kv_dtype)],
        compiler_params=_params("parallel", "arbitrary"),
        name=name,
    )(*args)


def _rope_tables(ls):
    rows = ls // GRID_W
    row = jnp.repeat(jnp.arange(rows, dtype=F32), GRID_W)
    col = jnp.tile(jnp.arange(GRID_W, dtype=F32), rows)
    axis_dim = QK_DIM // 2
    inv = ROPE_THETA ** (-jnp.arange(0, axis_dim, 2, dtype=F32) / axis_dim)
    ar, ac = row[:, None] * inv, col[:, None] * inv
    cos = jnp.concatenate([jnp.cos(ar), jnp.cos(ar), jnp.cos(ac), jnp.cos(ac)], axis=1)
    sin = jnp.concatenate([-jnp.sin(ar), jnp.sin(ar), -jnp.sin(ac), jnp.sin(ac)], axis=1)
    return cos, sin


def _attn_kernel(lam_ref, subw_ref, q_ref, *rest, nseg, lam_init):
    k_refs, v_refs, o_ref = rest[:nseg], rest[nseg:2 * nseg], rest[2 * nseg]
    lp = lam_ref[...]
    lam = (jnp.exp(jnp.sum(lp[0:1] * lp[1:2], axis=-1, keepdims=True))
           - jnp.exp(jnp.sum(lp[2:3] * lp[3:4], axis=-1, keepdims=True)) + lam_init)
    q = q_ref[...]
    ks = [r[...].astype(BF16) for r in k_refs]
    vs = [r[...].astype(BF16) for r in v_refs]
    probs, invs = [], []
    for mi in range(2):
        qm = q[:, mi * QK_DIM:(mi + 1) * QK_DIM]
        ss = [lax.dot_general(qm, k[:, mi * QK_DIM:(mi + 1) * QK_DIM], (((1,), (1,)), ((), ())),
                              preferred_element_type=F32) for k in ks]
        mx = functools.reduce(jnp.maximum, [jnp.max(s, axis=-1, keepdims=True) for s in ss])
        ps = [jnp.exp(s - mx) for s in ss]
        den = functools.reduce(jnp.add, [jnp.sum(p, axis=-1, keepdims=True) for p in ps])
        probs.append(ps)
        invs.append(1.0 / den)
    c1 = lam * invs[1]
    acc = None
    for si in range(nseg):
        wgt = (probs[0][si] * invs[0] - probs[1][si] * c1).astype(BF16)
        part = jnp.dot(wgt, vs[si], preferred_element_type=F32)
        acc = part if acc is None else acc + part
    ms = jnp.mean(acc * acc, axis=-1, keepdims=True)
    o = acc * lax.rsqrt(ms + NORM_EPS) * subw_ref[...] * (1.0 - lam_init)
    o_ref[...] = o.astype(o_ref.dtype)


def _attn_call(q, kv_segs, lam_params, subln_w, *, nb, lq, tq, lam_init, name):
    nqb = lq // tq
    nseg = len(kv_segs)
    in_specs = [pl.BlockSpec((4, QK_DIM), lambda b, h, i: (0, 0)),
                pl.BlockSpec((1, V_DIM), lambda b, h, i: (0, 0)),
                pl.BlockSpec((tq, V_DIM), lambda b, h, i: (b * nqb + i, h))]
    in_specs += [pl.BlockSpec((lk, V_DIM), lambda b, h, i: (b, h)) for (_, _, lk) in kv_segs] * 2
    args = [lam_params, subln_w.reshape(1, V_DIM), q]
    args += [k for (k, _, _) in kv_segs] + [v for (_, v, _) in kv_segs]
    return pl.pallas_call(
        functools.partial(_attn_kernel, nseg=nseg, lam_init=lam_init),
        grid=(nb, ATTN_HEADS, nqb),
        in_specs=in_specs,
        out_specs=pl.BlockSpec((tq, V_DIM), lambda b, h, i: (b * nqb + i, h)),
        out_shape=jax.ShapeDtypeStruct(q.shape, BF16),
        compiler_params=_params("parallel", "parallel", "arbitrary"),
        name=name,
    )(*args)


def _conv_kernel(xp_ref, x_ref, xn_ref, w_ref, b_ref, o_ref, ext_ref, *, tc, n_p, lp, ls, halo):
    i = pl.program_id(0)
    row0 = i * tc
    in_prompt = row0 < n_p
    first = jnp.where(in_prompt, row0 % lp == 0, (row0 - n_p) % ls == 0)
    last = jnp.where(in_prompt, (row0 + tc) % lp == 0, (row0 + tc - n_p) % ls == 0)
    prev = xp_ref[...].astype(F32)[halo - 8:halo, :]
    nxt = xn_ref[...].astype(F32)[0:8, :]
    ext_ref[0:8, :] = jnp.where(first, 0.0, prev)
    ext_ref[8:8 + tc, :] = x_ref[...].astype(F32)
    ext_ref[8 + tc:16 + tc, :] = jnp.where(last, 0.0, nxt)
    acc = b_ref[...] + w_ref[0:1, :] * ext_ref[6:6 + tc, :]
    for k in range(1, SSD_CONV):
        acc = acc + w_ref[k:k + 1, :] * ext_ref[6 + k:6 + k + tc, :]
    o_ref[...] = (acc / (1.0 + jnp.exp(-acc))).astype(o_ref.dtype)


def _conv_call(xbc, conv_w, conv_b, *, n_p, lp, ls, tc, tcn, name):
    m, ch = xbc.shape
    halo = 16
    hb = tc // halo
    nhb = m // halo
    return pl.pallas_call(
        functools.partial(_conv_kernel, tc=tc, n_p=n_p, lp=lp, ls=ls, halo=halo),
        grid=(m // tc, ch // tcn),
        in_specs=[pl.BlockSpec((halo, tcn), lambda i, j: (jnp.maximum(i * hb - 1, 0), j)),
                  pl.BlockSpec((tc, tcn), lambda i, j: (i, j)),
                  pl.BlockSpec((halo, tcn), lambda i, j: (jnp.minimum((i + 1) * hb, nhb - 1), j)),
                  pl.BlockSpec((SSD_CONV, tcn), lambda i, j: (0, j)),
                  pl.BlockSpec((1, tcn), lambda i, j: (0, j))],
        out_specs=pl.BlockSpec((tc, tcn), lambda i, j: (i, j)),
        out_shape=jax.ShapeDtypeStruct((m, ch), BF16),
        scratch_shapes=[pltpu.VMEM((tc + 16, tcn), F32)],
        compiler_params=_params("parallel", "arbitrary"),
        name=name,
    )(xbc, xbc, xbc, conv_w, conv_b.reshape(1, ch))


def _ssd_kernel(xbc_ref, dtraw_ref, dtb_ref, alog_ref, tri_ref, e_ref, *rest, direction, zero_init, nt):
    if zero_init:
        y_ref, fin_ref, st_ref = rest
    else:
        h0_ref, y_ref, fin_ref, st_ref = rest
    t = pl.program_id(1)
    q = SSD_CHUNK
    hpg = SSD_HEADS // SSD_GROUPS

    @pl.when(t == 0)
    def _():
        for g in range(SSD_GROUPS):
            if zero_init:
                st_ref[g] = jnp.zeros((SSD_STATE, GROUP_W), F32)
            else:
                hg = h0_ref[0, 0, g * hpg:(g + 1) * hpg].reshape(GROUP_W, SSD_STATE)
                st_ref[g] = hg.T

    x = dtraw_ref[...] + dtb_ref[...]
    dt = jnp.maximum(x, 0.0) + jnp.log1p(jnp.exp(-jnp.abs(x)))
    a = dt * (-jnp.exp(alog_ref[...]))
    tri = tri_ref[...]
    a1 = a.astype(BF16)
    r1 = a - a1.astype(F32)
    a2 = r1.astype(BF16)
    a3 = (r1 - a2.astype(F32)).astype(BF16)
    cs = (jnp.dot(tri, a1, preferred_element_type=F32) + jnp.dot(tri, a2, preferred_element_type=F32)
          + jnp.dot(tri, a3, preferred_element_type=F32))
    cs_t = cs.T
    tot = cs[q - 1:q, :] if direction == 0 else cs[0:1, :]
    dec_end = jnp.exp(tot - cs)
    ecs = jnp.exp(cs)
    cdec = jnp.broadcast_to(jnp.exp(tot), (8, LANES))
    li = lax.broadcasted_iota(jnp.int32, (q, q), 0)
    si = lax.broadcasted_iota(jnp.int32, (q, q), 1)
    mask = (li >= si) if direction == 0 else (li <= si)
    lane = lax.broadcasted_iota(jnp.int32, (q, LANES), 1)

    for g in range(SSD_GROUPS):
        eg = e_ref[g]

        def expand(v, eg=eg):
            hi, lo = _split2(v)
            return jnp.dot(hi, eg, preferred_element_type=F32) + jnp.dot(lo, eg, preferred_element_type=F32)

        xg = xbc_ref[:, g * GROUP_W:(g + 1) * GROUP_W].astype(F32)
        xd = xg * expand(dt)
        xdb = xd.astype(BF16)
        bg = xbc_ref[:, SSD_D_INNER + g * SSD_STATE:SSD_D_INNER + (g + 1) * SSD_STATE]
        cg = xbc_ref[:, SSD_D_INNER + SSD_BC_DIM + g * SSD_STATE:SSD_D_INNER + SSD_BC_DIM + (g + 1) * SSD_STATE]
        cb = lax.dot_general(cg, bg, (((1,), (1,)), ((), ())), preferred_element_type=F32)
        ys = []
        for p in range(hpg // 2):
            ms = []
            for c in (direction * SSD_HEADS + g * hpg + 2 * p, direction * SSD_HEADS + g * hpg + 2 * p + 1):
                seg = cs[:, c:c + 1] - cs_t[c:c + 1, :]
                ms.append((cb * jnp.exp(jnp.where(mask, seg, -jnp.inf))).astype(BF16))
            lhs = jnp.concatenate(ms, axis=1)
            xp = xdb[:, p * LANES:(p + 1) * LANES]
            zero = jnp.zeros_like(xp)
            rhs = jnp.concatenate([jnp.where(lane < SSD_HEAD_DIM, xp, zero),
                                   jnp.where(lane >= SSD_HEAD_DIM, xp, zero)], axis=0)
            ys.append(jnp.dot(lhs, rhs, preferred_element_type=F32))
        y_diag = jnp.concatenate(ys, axis=1)
        s_in = st_ref[g]
        y_off = jnp.dot(cg, s_in.astype(BF16), preferred_element_type=F32) * expand(ecs)
        y_ref[:, g * GROUP_W:(g + 1) * GROUP_W] = (y_diag + y_off).astype(y_ref.dtype)
        xe = (xd * expand(dec_end)).astype(BF16)
        bg_t = bg.astype(F32).T.astype(BF16)
        new = jnp.dot(bg_t, xe, preferred_element_type=F32)
        st_ref[g] = s_in * expand(cdec)[0:1, :] + new

    @pl.when(t == nt - 1)
    def _():
        for g in range(SSD_GROUPS):
            fin_ref[0, g * hpg:(g + 1) * hpg] = st_ref[g].T.reshape(hpg, SSD_HEAD_DIM, SSD_STATE)


def _ssd_scan_call(xbc_act, dt_raw, dt_bias, a_log, h0, *, nb, seq, row_off, direction, name):
    q = SSD_CHUNK
    nt = seq // q
    boff = row_off // q
    li = jnp.arange(q)[:, None]
    si = jnp.arange(q)[None, :]
    tri = ((li >= si) if direction == 0 else (li <= si)).astype(BF16)
    hpg = SSD_HEADS // SSD_GROUPS
    rows = jnp.arange(LANES)[None, :, None]
    cols = jnp.arange(GROUP_W)[None, None, :]
    gs = jnp.arange(SSD_GROUPS)[:, None, None]
    expand_mat = (rows == direction * SSD_HEADS + gs * hpg + cols // SSD_HEAD_DIM).astype(BF16)

    def chunk(b, t):
        return boff + b * nt + (t if direction == 0 else nt - 1 - t)

    zero_init = h0 is None
    in_specs = [pl.BlockSpec((q, SSD_CONV_CH), lambda b, t: (chunk(b, t), 0)),
                pl.BlockSpec((q, LANES), lambda b, t: (chunk(b, t), 0)),
                pl.BlockSpec((1, LANES), lambda b, t: (0, 0)),
                pl.BlockSpec((1, LANES), lambda b, t: (0, 0)),
                pl.BlockSpec((q, q), lambda b, t: (0, 0)),
                pl.BlockSpec((SSD_GROUPS, LANES, GROUP_W), lambda b, t: (0, 0, 0))]
    args = [xbc_act, dt_raw, dt_bias.reshape(1, LANES), a_log.reshape(1, LANES), tri, expand_mat]
    if not zero_init:
        in_specs.append(pl.BlockSpec((1, 1, SSD_HEADS, SSD_HEAD_DIM, SSD_STATE),
                                     lambda b, t: (b, direction, 0, 0, 0)))
        args.append(h0)
    return pl.pallas_call(
        functools.partial(_ssd_kernel, direction=direction, zero_init=zero_init, nt=nt),
        grid=(nb, nt),
        in_specs=in_specs,
        out_specs=[pl.BlockSpec((q, SSD_D_INNER), lambda b, t: (b * nt + (t if direction == 0 else nt - 1 - t), 0)),
                   pl.BlockSpec((1, SSD_HEADS, SSD_HEAD_DIM, SSD_STATE), lambda b, t: (b, 0, 0, 0))],
        out_shape=[jax.ShapeDtypeStruct((nb * seq, SSD_D_INNER), BF16),
                   jax.ShapeDtypeStruct((nb, SSD_HEADS, SSD_HEAD_DIM, SSD_STATE), F32)],
        scratch_shapes=[pltpu.VMEM((SSD_GROUPS, SSD_STATE, GROUP_W), F32)],
        compiler_params=_params("parallel", "arbitrary"),
        name=name,
    )(*args)


def _ssd_post_kernel(yf_ref, yb_ref, xs_ref, z_ref, d_ref, nw_ref, o_ref):
    z = z_ref[...].astype(F32)
    y = yf_ref[...].astype(F32) + yb_ref[...].astype(F32) + d_ref[...] * xs_ref[...].astype(F32)
    y = y * (z / (1.0 + jnp.exp(-z)))
    outs = []
    for g in range(SSD_GROUPS):
        yg = y[:, g * GROUP_W:(g + 1) * GROUP_W]
        ms = jnp.mean(yg * yg, axis=-1, keepdims=True)
        outs.append(yg * lax.rsqrt(ms + NORM_EPS) * nw_ref[:, g * GROUP_W:(g + 1) * GROUP_W])
    o_ref[...] = jnp.concatenate(outs, axis=1).astype(o_ref.dtype)


def _ssd_post_call(y_f, y_b, xbc_act, z, d_tot, norm_w, *, tm, name):
    m = y_f.shape[0]
    di = SSD_D_INNER
    row = pl.BlockSpec((tm, di), lambda i: (i, 0))
    vec = pl.BlockSpec((1, di), lambda i: (0, 0))
    return pl.pallas_call(
        _ssd_post_kernel,
        grid=(m // tm,),
        in_specs=[row, row, row, row, vec, vec],
        out_specs=row,
        out_shape=jax.ShapeDtypeStruct((m, di), BF16),
        compiler_params=_params("parallel"),
        name=name,
    )(y_f, y_b, xbc_act, z, d_tot, norm_w.reshape(1, di))


def _expert_changed(be_ref, i):
    return (i == 0) | (be_ref[i] != be_ref[jnp.maximum(i - 1, 0)])


def _moe_up_kernel(be_ref, nu_ref, x_ref, wg_ref, wu_ref, bg_ref, bu_ref, o_ref, wg_s, wu_s):
    i = pl.program_id(1)

    @pl.when(_expert_changed(be_ref, i))
    def _():
        wg_s[...] = wg_ref[0].astype(BF16)
        wu_s[...] = wu_ref[0].astype(BF16)

    @pl.when(i < nu_ref[0])
    def _():
        x = x_ref[...]
        g = jnp.dot(x, wg_s[...], preferred_element_type=F32) + bg_ref[0]
        u = jnp.dot(x, wu_s[...], preferred_element_type=F32) + bu_ref[0]
        g = jnp.minimum(g, SWIGLU_LIMIT)
        u = jnp.clip(u, -SWIGLU_LIMIT, SWIGLU_LIMIT)
        act = (u + 1.0) * (g / (1.0 + jnp.exp(-SWIGLU_ALPHA * g)))
        o_ref[...] = act.astype(o_ref.dtype)


def _moe_down_kernel(be_ref, nu_ref, a_ref, w_ref, b_ref, gate_ref, o_ref, w_s):
    i = pl.program_id(1)

    @pl.when(_expert_changed(be_ref, i))
    def _():
        w_s[...] = w_ref[0].astype(BF16)

    @pl.when(i < nu_ref[0])
    def _():
        y = jnp.dot(a_ref[...], w_s[...], preferred_element_type=F32) + b_ref[0]
        o_ref[...] = (y * gate_ref[...]).astype(o_ref.dtype)


def _moe_experts(x_sorted, gate_sorted, block_e, n_used, w_gate_up, b_gate_up, w_down, b_down, *, tn, name):
    slots, d = x_sorted.shape
    nblk = slots // MOE_BLOCK
    nj = D_FF // tn
    act = pl.pallas_call(
        _moe_up_kernel,
        grid_spec=pltpu.PrefetchScalarGridSpec(
            num_scalar_prefetch=2,
            grid=(nj, nblk),
            in_specs=[pl.BlockSpec((MOE_BLOCK, d), lambda j, i, be, nu: (i, 0)),
                      pl.BlockSpec((1, d, tn), lambda j, i, be, nu: (be[i], 0, j)),
                      pl.BlockSpec((1, d, tn), lambda j, i, be, nu: (be[i], 0, nj + j)),
                      pl.BlockSpec((1, 1, tn), lambda j, i, be, nu: (be[i], 0, j)),
                      pl.BlockSpec((1, 1, tn), lambda j, i, be, nu: (be[i], 0, nj + j))],
            out_specs=pl.BlockSpec((MOE_BLOCK, tn), lambda j, i, be, nu: (i, j)),
            scratch_shapes=[pltpu.VMEM((d, tn), BF16), pltpu.VMEM((d, tn), BF16)]),
        out_shape=jax.ShapeDtypeStruct((slots, D_FF), BF16),
        compiler_params=_params("arbitrary", "arbitrary"),
        name=name + "_up",
    )(block_e, n_used, x_sorted, w_gate_up, w_gate_up,
      b_gate_up.reshape(N_EXPERTS, 1, 2 * D_FF), b_gate_up.reshape(N_EXPERTS, 1, 2 * D_FF))
    nj2 = d // tn
    return pl.pallas_call(
        _moe_down_kernel,
        grid_spec=pltpu.PrefetchScalarGridSpec(
            num_scalar_prefetch=2,
            grid=(nj2, nblk),
            in_specs=[pl.BlockSpec((MOE_BLOCK, D_FF), lambda j, i, be, nu: (i, 0)),
                      pl.BlockSpec((1, D_FF, tn), lambda j, i, be, nu: (be[i], 0, j)),
                      pl.BlockSpec((1, 1, tn), lambda j, i, be, nu: (be[i], 0, j)),
                      pl.BlockSpec((MOE_BLOCK, 1), lambda j, i, be, nu: (i, 0))],
            out_specs=pl.BlockSpec((MOE_BLOCK, tn), lambda j, i, be, nu: (i, j)),
            scratch_shapes=[pltpu.VMEM((D_FF, tn), BF16)]),
        out_shape=jax.ShapeDtypeStruct((slots, d), BF16),
        compiler_params=_params("arbitrary", "arbitrary"),
        name=name + "_down",
    )(block_e, n_used, act, w_down, b_down.reshape(N_EXPERTS, 1, d), gate_sorted)


def _moe_layer(x, mod, n_p, ls, norm_w, w_router, b_router, w_gate_up, b_gate_up, w_down, b_down, *, name):
    n, d = x.shape
    h, idx_pad, gate_pad = _router_call(x, norm_w.reshape(1, d), mod, n_p, ls, w_router, b_router,
                                        tm=256, name=name + "_router")
    top_idx = idx_pad[:, :TOP_K]
    gates = gate_pad[:, :TOP_K]
    n_assign = n * TOP_K
    flat_e = top_idx.reshape(-1)
    onehot = (flat_e[:, None] == jnp.arange(N_EXPERTS, dtype=jnp.int32)[None, :]).astype(jnp.int32)
    csum = jnp.cumsum(onehot, axis=0)
    counts = csum[-1]
    rank = jnp.take_along_axis(csum, flat_e[:, None], axis=1)[:, 0] - 1
    padded = (counts + MOE_BLOCK - 1) // MOE_BLOCK * MOE_BLOCK
    pad_end = jnp.cumsum(padded)
    pad_start = pad_end - padded
    dest = pad_start[flat_e] + rank
    n_blocks = -(-n_assign // MOE_BLOCK) + N_EXPERTS
    slots = n_blocks * MOE_BLOCK
    slot_tok = jnp.zeros((slots,), jnp.int32).at[dest].set(jnp.arange(n_assign, dtype=jnp.int32) // TOP_K)
    slot_gate = jnp.zeros((slots,), F32).at[dest].set(gates.reshape(-1))
    block_e = jnp.minimum(jnp.searchsorted(pad_end, jnp.arange(n_blocks, dtype=jnp.int32) * MOE_BLOCK,
                                           side="right"), N_EXPERTS - 1).astype(jnp.int32)
    n_used = (pad_end[-1] // MOE_BLOCK).astype(jnp.int32).reshape(1)
    x_sorted = jnp.take(h, slot_tok, axis=0)
    y_sorted = _moe_experts(x_sorted, slot_gate.reshape(slots, 1), block_e, n_used,
                            w_gate_up, b_gate_up, w_down, b_down, tn=512, name=name)
    f = jnp.take(y_sorted, dest, axis=0).astype(F32).reshape(n, TOP_K, d).sum(axis=1)
    grp = jnp.where(jnp.arange(n) < n_p, 0, 1 + (jnp.arange(n) - n_p) // ls)
    g2 = mod.reshape(-1, 6, d)[:, 5][grp]
    return x + g2 * f


def _ada_call(cond, w, b, *, name):
    g = cond.shape[0]
    a = jnp.pad(jax.nn.silu(cond), ((0, 16 - g), (0, 0))).astype(BF16)
    m = _matmul(a, w, n_out=6 * D_MODEL, tm=16, tn=1024, out_dtype=F32, name=name,
                bias=b.reshape(1, 6 * D_MODEL))
    return m[:g].reshape(g * 6, 1, D_MODEL)


def kernel(x_prompt, x_sample, c, c_ctx, cache_k, cache_v, state_ssm, norm1_w, norm2_w, w_ada, b_ada, w_qkv, q_norm_w, k_norm_w, lambda_q1, lambda_k1, lambda_q2, lambda_k2, subln_w, w_o, w_in_ssd, conv_w, conv_b, dt_bias, a_log, d_skip, ssd_norm_w, w_out_ssd, w_router, b_router, w_gate_up, b_gate_up, w_down, b_down):
    bp, lp, d = x_prompt.shape
    bs, ls, _ = x_sample.shape
    past = cache_k.shape[2]
    n_p, n_s = bp * lp, bs * ls
    x = jnp.concatenate([x_prompt.reshape(n_p, d), x_sample.reshape(n_s, d)], axis=0)
    cond = jnp.concatenate([c_ctx[None], c], axis=0)
    resid_of = lambda xx, mod, which: (xx, mod, which, n_p, ls)

    mod = _ada_call(cond, w_ada[0], b_ada[0], name="ada0")
    h = _modnorm_call(x, norm1_w[0].reshape(1, d), mod, 0, n_p, ls, tm=256, name="norm1_0")
    wq = w_qkv[0].astype(BF16)
    qk_w = jnp.stack([q_norm_w[0], k_norm_w[0]], axis=0)
    lam_init = 0.8 - 0.6 * math.exp(-0.3 * 0)
    lam_params = jnp.stack([lambda_q1[0], lambda_k1[0], lambda_q2[0], lambda_k2[0]], axis=0)
    qp, kp, vp = _qkv_call(h[:n_p], wq, qk_w, None, kv_dtype=F32, tm=512, tn=512, ls=ls, name="qkv_prompt")
    qs, ks, vs = _qkv_call(h[n_p:], wq, qk_w, _rope_tables(ls), kv_dtype=BF16, tm=512, tn=512, ls=ls,
                           name="qkv_sample")
    op = _attn_call(qp, [(kp, vp, lp)], lam_params, subln_w[0], nb=bp, lq=lp, tq=lp, lam_init=lam_init,
                    name="attn_prompt")
    ck = cache_k[:, 0].reshape(bs * past, d)
    cv = cache_v[:, 0].reshape(bs * past, d)
    osamp = _attn_call(qs, [(ck, cv, past), (ks, vs, ls)], lam_params, subln_w[0], nb=bs, lq=ls, tq=256,
                       lam_init=lam_init, name="attn_sample")
    o = jnp.concatenate([op, osamp], axis=0)
    x = _matmul(o, w_o[0].astype(BF16), n_out=d, tm=512, tn=512, out_dtype=F32, name="attn_out",
                resid=resid_of(x, mod, 2))
    x = _moe_layer(x, mod, n_p, ls, norm2_w[0], w_router[0], b_router[0], w_gate_up[0], b_gate_up[0],
                   w_down[0], b_down[0], name="moe0")
    new_k = kp.reshape(bp, 1, lp, ATTN_HEADS, 2, QK_DIM)
    new_v = vp.reshape(bp, 1, lp, ATTN_HEADS, V_DIM)

    mod = _ada_call(cond, w_ada[1], b_ada[1], name="ada1")
    h = _modnorm_call(x, norm1_w[1].reshape(1, d), mod, 0, n_p, ls, tm=256, name="norm1_1")
    w_in = w_in_ssd[0].astype(BF16)
    z = _matmul(h, w_in, n_out=SSD_D_INNER, tm=512, tn=512, out_dtype=BF16, name="ssd_in_z")
    xbc = _matmul(h, w_in, n_out=SSD_CONV_CH, col_off=SSD_D_INNER, tm=512, tn=512, out_dtype=BF16,
                  name="ssd_in_xbc")
    dt_raw = _matmul(h, w_in, n_out=2 * SSD_HEADS, col_off=SSD_D_INNER + SSD_CONV_CH, tm=512, tn=LANES,
                     out_dtype=F32, name="ssd_in_dt")
    xbc_act = _conv_call(xbc, conv_w[0], conv_b[0], n_p=n_p, lp=lp, ls=ls, tc=256, tcn=2048, name="ssd_conv")
    ys, fins = [], []
    for direction in (0, 1):
        yp, fp = _ssd_scan_call(xbc_act, dt_raw, dt_bias[0], a_log[0], None, nb=bp, seq=lp, row_off=0,
                                direction=direction, name="ssd_scan_prompt%d" % direction)
        ysm, _ = _ssd_scan_call(xbc_act, dt_raw, dt_bias[0], a_log[0], state_ssm[:, 0], nb=bs, seq=ls,
                                row_off=n_p, direction=direction, name="ssd_scan_sample%d" % direction)
        ys.append(jnp.concatenate([yp, ysm], axis=0))
        fins.append(fp)
    d_tot = jnp.repeat(d_skip[0, 0] + d_skip[0, 1], SSD_HEAD_DIM).reshape(1, SSD_D_INNER)
    yn = _ssd_post_call(ys[0], ys[1], xbc_act, z, d_tot, ssd_norm_w[0], tm=256, name="ssd_post")
    x = _matmul(yn, w_out_ssd[0].astype(BF16), n_out=d, tm=512, tn=512, out_dtype=F32, name="ssd_out",
                resid=resid_of(x, mod, 2))
    x = _moe_layer(x, mod, n_p, ls, norm2_w[1], w_router[1], b_router[1], w_gate_up[1], b_gate_up[1],
                   w_down[1], b_down[1], name="moe1")
    new_s = jnp.stack(fins, axis=1)[:, None]
    return (x[:n_p].reshape(bp, lp, d), x[n_p:].reshape(bs, ls, d), new_k, new_v, new_s)
```

```python
import functools
import math

import jax
import jax.numpy as jnp
from jax import lax
from jax.experimental import pallas as pl
from jax.experimental.pallas import tpu as pltpu

F32 = jnp.float32
BF16 = jnp.bfloat16

D_MODEL = 2048
NORM_EPS = 1e-6
GRID_W = 64
ROPE_THETA = 10000.0
ATTN_HEADS = 8
QK_DIM = 128
V_DIM = 256
SSD_D_INNER = 4096
SSD_HEAD_DIM = 64
SSD_HEADS = 64
SSD_GROUPS = 8
SSD_STATE = 128
SSD_CONV = 5
SSD_CHUNK = 128
SSD_BC_DIM = SSD_GROUPS * SSD_STATE
SSD_CONV_CH = SSD_D_INNER + 2 * SSD_BC_DIM
N_EXPERTS = 32
TOP_K = 4
D_FF = 2048
SWIGLU_LIMIT = 7.0
SWIGLU_ALPHA = 1.702
MOE_BLOCK = 256
LANES = 128
GROUP_W = SSD_D_INNER // SSD_GROUPS
VMEM_LIMIT = 56 * 1024 * 1024


def _params(*sem):
    return pltpu.CompilerParams(dimension_semantics=sem, vmem_limit_bytes=VMEM_LIMIT)


def _group_of_tile(i, tm, n_p, ls):
    return jnp.where(i * tm < n_p, 0, 1 + (i * tm - n_p) // ls)


def _mod_spec(which, tm, tn, n_p, ls):
    return pl.BlockSpec((1, 1, tn), lambda i, j: (_group_of_tile(i, tm, n_p, ls) * 6 + which, 0, j))


def _mm_kernel(a_ref, w_ref, *rest, epilogue):
    acc = jnp.dot(a_ref[...].astype(BF16), w_ref[...].astype(BF16), preferred_element_type=F32)
    if epilogue == "bias":
        b_ref, o_ref = rest
        o_ref[...] = (acc + b_ref[...]).astype(o_ref.dtype)
    elif epilogue == "resid":
        x_ref, g_ref, o_ref = rest
        o_ref[...] = x_ref[...] + g_ref[0] * acc
    else:
        (o_ref,) = rest
        o_ref[...] = acc.astype(o_ref.dtype)


def _matmul(a, w, *, n_out, col_off=0, tm, tn, out_dtype, name, layer=None, bias=None, resid=None):
    m, k = a.shape
    joff = col_off // tn
    if layer is None:
        w_spec = pl.BlockSpec((k, tn), lambda i, j: (0, j + joff))
        b_spec = pl.BlockSpec((1, tn), lambda i, j: (0, j))
    else:
        w_spec = pl.BlockSpec((None, k, tn), lambda i, j: (layer, 0, j + joff))
        b_spec = pl.BlockSpec((None, 1, tn), lambda i, j: (layer, 0, j))
    in_specs = [pl.BlockSpec((tm, k), lambda i, j: (i, 0)), w_spec]
    args = [a, w]
    if bias is not None:
        epilogue = "bias"
        in_specs.append(b_spec)
        args.append(bias)
    elif resid is not None:
        epilogue = "resid"
        x, mod, which, n_p, ls = resid
        in_specs += [pl.BlockSpec((tm, tn), lambda i, j: (i, j)), _mod_spec(which, tm, tn, n_p, ls)]
        args += [x, mod]
    else:
        epilogue = "plain"
    return pl.pallas_call(
        functools.partial(_mm_kernel, epilogue=epilogue),
        grid=(m // tm, n_out // tn),
        in_specs=in_specs,
        out_specs=pl.BlockSpec((tm, tn), lambda i, j: (i, j)),
        out_shape=jax.ShapeDtypeStruct((m, n_out), out_dtype),
        compiler_params=_params("parallel", "arbitrary"),
        name=name,
    )(*args)


def _modnorm(x, w, shift, scale):
    ms = jnp.mean(x * x, axis=-1, keepdims=True)
    return (x * lax.rsqrt(ms + NORM_EPS) * w) * (1.0 + scale) + shift


def _modnorm_kernel(x_ref, w_ref, sh_ref, sc_ref, o_ref):
    o_ref[...] = _modnorm(x_ref[...], w_ref[...], sh_ref[0], sc_ref[0]).astype(o_ref.dtype)


def _modnorm_call(x, w, mod, which_shift, n_p, ls, *, tm, name):
    m, d = x.shape
    return pl.pallas_call(
        _modnorm_kernel,
        grid=(m // tm, 1),
        in_specs=[pl.BlockSpec((tm, d), lambda i, j: (i, 0)),
                  pl.BlockSpec((1, d), lambda i, j: (0, 0)),
                  _mod_spec(which_shift, tm, d, n_p, ls),
                  _mod_spec(which_shift + 1, tm, d, n_p, ls)],
        out_specs=pl.BlockSpec((tm, d), lambda i, j: (i, 0)),
        out_shape=jax.ShapeDtypeStruct((m, d), BF16),
        compiler_params=_params("parallel", "arbitrary"),
        name=name,
    )(x, w, mod, mod)


def _split2(v):
    hi = v.astype(BF16)
    lo = (v - hi.astype(F32)).astype(BF16)
    return hi, lo


def _router_kernel(x_ref, w_ref, sh_ref, sc_ref, wr_ref, br_ref, h_ref, idx_ref, gate_ref):
    h = _modnorm(x_ref[...], w_ref[...], sh_ref[0], sc_ref[0])
    h_ref[...] = h.astype(BF16)
    h_hi, h_lo = _split2(h)
    w_hi, w_lo = _split2(wr_ref[...])
    logits = (jnp.dot(h_hi, w_hi, preferred_element_type=F32)
              + jnp.dot(h_hi, w_lo, preferred_element_type=F32)
              + jnp.dot(h_lo, w_hi, preferred_element_type=F32)) + br_ref[...]
    lane = lax.broadcasted_iota(jnp.int32, logits.shape, 1).astype(F32)
    cur = jnp.where(lane < N_EXPERTS, logits, -jnp.inf)
    vals, idxs = [], []
    for _ in range(TOP_K):
        m = jnp.max(cur, axis=-1, keepdims=True)
        am = jnp.min(jnp.where(cur == m, lane, float(LANES)), axis=-1, keepdims=True)
        vals.append(m)
        idxs.append(am)
        cur = jnp.where(lane == am, -jnp.inf, cur)
    exps = [jnp.exp(v - vals[0]) for v in vals]
    denom = exps[0] + exps[1] + exps[2] + exps[3]
    idx_out = jnp.zeros(logits.shape, F32)
    gate_out = jnp.zeros(logits.shape, F32)
    for k in range(TOP_K):
        idx_out = jnp.where(lane == k, idxs[k], idx_out)
        gate_out = jnp.where(lane == k, exps[k] / denom, gate_out)
    idx_ref[...] = idx_out.astype(jnp.int32)
    gate_ref[...] = gate_out


def _router_call(x, w, mod, n_p, ls, w_router, b_router, *, tm, name):
    m, d = x.shape
    wr = jnp.pad(w_router, ((0, 0), (0, LANES - N_EXPERTS)))
    br = jnp.pad(b_router, (0, LANES - N_EXPERTS)).reshape(1, LANES)
    return pl.pallas_call(
        _router_kernel,
        grid=(m // tm, 1),
        in_specs=[pl.BlockSpec((tm, d), lambda i, j: (i, 0)),
                  pl.BlockSpec((1, d), lambda i, j: (0, 0)),
                  _mod_spec(3, tm, d, n_p, ls),
                  _mod_spec(4, tm, d, n_p, ls),
                  pl.BlockSpec((d, LANES), lambda i, j: (0, 0)),
                  pl.BlockSpec((1, LANES), lambda i, j: (0, 0))],
        out_specs=[pl.BlockSpec((tm, d), lambda i, j: (i, 0)),
                   pl.BlockSpec((tm, LANES), lambda i, j: (i, 0)),
                   pl.BlockSpec((tm, LANES), lambda i, j: (i, 0))],
        out_shape=[jax.ShapeDtypeStruct((m, d), BF16),
                   jax.ShapeDtypeStruct((m, LANES), jnp.int32),
                   jax.ShapeDtypeStruct((m, LANES), F32)],
        compiler_params=_params("parallel", "arbitrary"),
        name=name,
    )(x, w, mod, mod, wr, br)


def _qkv_kernel(a_ref, w_ref, nw_ref, *rest, rope, tn):
    if rope:
        cos_ref, sin_ref, q_ref, k_ref, v_ref = rest
    else:
        q_ref, k_ref, v_ref = rest
    j = pl.program_id(1)
    nq = D_MODEL // tn
    acc = jnp.dot(a_ref[...], w_ref[...], preferred_element_type=F32)

    def normed(widx, out_scale):
        w = nw_ref[widx:widx + 1, :]
        outs = []
        for c in range(tn // QK_DIM):
            xc = acc[:, c * QK_DIM:(c + 1) * QK_DIM]
            ms = jnp.mean(xc * xc, axis=-1, keepdims=True)
            y = xc * lax.rsqrt(ms + NORM_EPS) * w
            if rope:
                lane = lax.broadcasted_iota(jnp.int32, y.shape, 1)
                partner = jnp.where(lane % 64 < 32, pltpu.roll(y, QK_DIM - 32, 1), pltpu.roll(y, 32, 1))
                y = y * cos_ref[...] + partner * sin_ref[...]
            outs.append(y * out_scale if out_scale != 1.0 else y)
        return jnp.concatenate(outs, axis=1)

    @pl.when(j < nq)
    def _():
        q_ref[...] = normed(0, QK_DIM ** -0.5).astype(q_ref.dtype)

    @pl.when((j >= nq) & (j < 2 * nq))
    def _():
        k_ref[...] = normed(1, 1.0).astype(k_ref.dtype)

    @pl.when(j >= 2 * nq)
    def _():
        v_ref[...] = acc.astype(v_ref.dtype)


def _qkv_call(h, w_qkv, qk_norm_w, rope_tabs, *, row_off, m, kv_dtype, tm, tn, ls, name):
    d = h.shape[1]
    nq = d // tn
    ioff = row_off // tm
    rope = rope_tabs is not None
    in_specs = [pl.BlockSpec((tm, d), lambda i, j: (i + ioff, 0)),
                pl.BlockSpec((d, tn), lambda i, j: (0, j)),
                pl.BlockSpec((2, QK_DIM), lambda i, j: (0, 0))]
    args = [h, w_qkv, qk_norm_w]
    if rope:
        nt = ls // tm
        in_specs += [pl.BlockSpec((tm, QK_DIM), lambda i, j: (i % nt, 0))] * 2
        args += list(rope_tabs)
    return pl.pallas_call(
        functools.partial(_qkv_kernel, rope=rope, tn=tn),
        grid=(m // tm, 3 * nq),
        in_specs=in_specs,
        out_specs=[pl.BlockSpec((tm, tn), lambda i, j: (i, jnp.minimum(j, nq - 1))),
                   pl.BlockSpec((tm, tn), lambda i, j: (i, jnp.clip(j - nq, 0, nq - 1))),
                   pl.BlockSpec((tm, tn), lambda i, j: (i, jnp.clip(j - 2 * nq, 0, nq - 1)))],
        out_shape=[jax.ShapeDtypeStruct((m, d), BF16),
                   jax.ShapeDtypeStruct((m, d), kv_dtype),
                   jax.ShapeDtypeStruct((m, d), kv_dtype)],
        compiler_params=_params("parallel", "arbitrary"),
        name=name,
    )(*args)


def _rope_tables(ls):
    rows = ls // GRID_W
    row = jnp.repeat(jnp.arange(rows, dtype=F32), GRID_W)
    col = jnp.tile(jnp.arange(GRID_W, dtype=F32), rows)
    axis_dim = QK_DIM // 2
    inv = ROPE_THETA ** (-jnp.arange(0, axis_dim, 2, dtype=F32) / axis_dim)
    ar, ac = row[:, None] * inv, col[:, None] * inv
    cos = jnp.concatenate([jnp.cos(ar), jnp.cos(ar), jnp.cos(ac), jnp.cos(ac)], axis=1)
    sin = jnp.concatenate([-jnp.sin(ar), jnp.sin(ar), -jnp.sin(ac), jnp.sin(ac)], axis=1)
    return cos, sin


def _attn_kernel(lam_ref, subw_ref, q_ref, *rest, nseg, lam_init):
    k_refs, v_refs, o_ref = rest[:nseg], rest[nseg:2 * nseg], rest[-1]
    lp = lam_ref[...]
    lam = (jnp.exp(jnp.sum(lp[0:1] * lp[1:2], axis=-1, keepdims=True))
           - jnp.exp(jnp.sum(lp[2:3] * lp[3:4], axis=-1, keepdims=True)) + lam_init)
    q = q_ref[...]
    ks = [r[...].astype(BF16) for r in k_refs]
    vs = [r[...].astype(BF16) for r in v_refs]
    outs = []
    for mi in range(2):
        qm = q[:, mi * QK_DIM:(mi + 1) * QK_DIM]
        ss = [lax.dot_general(qm, k[:, mi * QK_DIM:(mi + 1) * QK_DIM], (((1,), (1,)), ((), ())),
                              preferred_element_type=F32) for k in ks]
        mx = functools.reduce(jnp.maximum, [jnp.max(s, axis=-1, keepdims=True) for s in ss])
        ps = [jnp.exp(s - mx) for s in ss]
        den = functools.reduce(jnp.add, [jnp.sum(p, axis=-1, keepdims=True) for p in ps])
        pv = functools.reduce(jnp.add, [jnp.dot(p.astype(BF16), v, preferred_element_type=F32)
                                        for p, v in zip(ps, vs)])
        outs.append(pv * (1.0 / den))
    acc = outs[0] - lam * outs[1]
    ms = jnp.mean(acc * acc, axis=-1, keepdims=True)
    o = acc * lax.rsqrt(ms + NORM_EPS) * subw_ref[...] * (1.0 - lam_init)
    o_ref[...] = o.astype(o_ref.dtype)


def _attn_call(q, kv_segs, lam_params, subln_w, o_into, *, n_rows, row_off, nb, lq, tq, lam_init, name):
    nqb = lq // tq
    nseg = len(kv_segs)
    ooff = row_off // tq
    in_specs = [pl.BlockSpec((4, QK_DIM), lambda b, h, i: (0, 0)),
                pl.BlockSpec((1, V_DIM), lambda b, h, i: (0, 0)),
                pl.BlockSpec((tq, V_DIM), lambda b, h, i: (b * nqb + i, h))]
    in_specs += [pl.BlockSpec((lk, V_DIM), lambda b, h, i: (b, h)) for (_, _, lk) in kv_segs] * 2
    args = [lam_params, subln_w.reshape(1, V_DIM), q]
    args += [k for (k, _, _) in kv_segs] + [v for (_, v, _) in kv_segs]
    aliases = {}
    if o_into is not None:
        in_specs.append(pl.BlockSpec(memory_space=pl.ANY))
        aliases = {len(args): 0}
        args.append(o_into)
    return pl.pallas_call(
        functools.partial(_attn_kernel, nseg=nseg, lam_init=lam_init),
        grid=(nb, ATTN_HEADS, nqb),
        in_specs=in_specs,
        out_specs=pl.BlockSpec((tq, V_DIM), lambda b, h, i: (ooff + b * nqb + i, h)),
        out_shape=jax.ShapeDtypeStruct((n_rows, q.shape[1]), BF16),
        input_output_aliases=aliases,
        compiler_params=_params("parallel", "parallel", "arbitrary"),
        name=name,
    )(*args)


def _conv_kernel(xp_ref, x_ref, xn_ref, w_ref, b_ref, o_ref, ext_ref, *, tc, n_p, lp, ls, halo):
    i = pl.program_id(0)
    row0 = i * tc
    in_prompt = row0 < n_p
    first = jnp.where(in_prompt, row0 % lp == 0, (row0 - n_p) % ls == 0)
    last = jnp.where(in_prompt, (row0 + tc) % lp == 0, (row0 + tc - n_p) % ls == 0)
    prev = xp_ref[...].astype(F32)[halo - 8:halo, :]
    nxt = xn_ref[...].astype(F32)[0:8, :]
    ext_ref[0:8, :] = jnp.where(first, 0.0, prev)
    ext_ref[8:8 + tc, :] = x_ref[...].astype(F32)
    ext_ref[8 + tc:16 + tc, :] = jnp.where(last, 0.0, nxt)
    acc = b_ref[...] + w_ref[0:1, :] * ext_ref[6:6 + tc, :]
    for k in range(1, SSD_CONV):
        acc = acc + w_ref[k:k + 1, :] * ext_ref[6 + k:6 + k + tc, :]
    o_ref[...] = (acc / (1.0 + jnp.exp(-acc))).astype(o_ref.dtype)


def _conv_call(xbc, conv_w, conv_b, *, n_p, lp, ls, tc, tcn, name):
    m, ch = xbc.shape
    halo = 16
    hb = tc // halo
    nhb = m // halo
    return pl.pallas_call(
        functools.partial(_conv_kernel, tc=tc, n_p=n_p, lp=lp, ls=ls, halo=halo),
        grid=(m // tc, ch // tcn),
        in_specs=[pl.BlockSpec((halo, tcn), lambda i, j: (jnp.maximum(i * hb - 1, 0), j)),
                  pl.BlockSpec((tc, tcn), lambda i, j: (i, j)),
                  pl.BlockSpec((halo, tcn), lambda i, j: (jnp.minimum((i + 1) * hb, nhb - 1), j)),
                  pl.BlockSpec((SSD_CONV, tcn), lambda i, j: (0, j)),
                  pl.BlockSpec((1, tcn), lambda i, j: (0, j))],
        out_specs=pl.BlockSpec((tc, tcn), lambda i, j: (i, j)),
        out_shape=jax.ShapeDtypeStruct((m, ch), BF16),
        scratch_shapes=[pltpu.VMEM((tc + 16, tcn), F32)],
        compiler_params=_params("parallel", "arbitrary"),
        name=name,
    )(xbc, xbc, xbc, conv_w, conv_b.reshape(1, ch))


def _ssd_kernel(xbc_ref, dtraw_ref, dtb_ref, alog_ref, tri_ref, e_ref, *rest, direction, zero_init, nt):
    y_ref, fin_ref, st_ref = rest[-3:]
    if not zero_init:
        h0_ref = rest[0]
    t = pl.program_id(1)
    q = SSD_CHUNK
    hpg = SSD_HEADS // SSD_GROUPS

    @pl.when(t == 0)
    def _():
        for g in range(SSD_GROUPS):
            if zero_init:
                st_ref[g] = jnp.zeros((SSD_STATE, GROUP_W), F32)
            else:
                hg = h0_ref[0, 0, g * hpg:(g + 1) * hpg].reshape(GROUP_W, SSD_STATE)
                st_ref[g] = hg.T

    x = dtraw_ref[...] + dtb_ref[...]
    dt = jnp.maximum(x, 0.0) + jnp.log1p(jnp.exp(-jnp.abs(x)))
    a = dt * (-jnp.exp(alog_ref[...]))
    tri = tri_ref[...]
    a1 = a.astype(BF16)
    r1 = a - a1.astype(F32)
    a2 = r1.astype(BF16)
    a3 = (r1 - a2.astype(F32)).astype(BF16)
    cs = (jnp.dot(tri, a1, preferred_element_type=F32) + jnp.dot(tri, a2, preferred_element_type=F32)
          + jnp.dot(tri, a3, preferred_element_type=F32))
    cs_t = cs.T
    tot = cs[q - 1:q, :] if direction == 0 else cs[0:1, :]
    dec_end = jnp.exp(tot - cs)
    ecs = jnp.exp(cs)
    cdec = jnp.broadcast_to(jnp.exp(tot), (8, LANES))
    li = lax.broadcasted_iota(jnp.int32, (q, q), 0)
    si = lax.broadcasted_iota(jnp.int32, (q, q), 1)
    mask = (li >= si) if direction == 0 else (li <= si)
    lane = lax.broadcasted_iota(jnp.int32, (q, LANES), 1)

    for g in range(SSD_GROUPS):
        eg = e_ref[g]

        def expand(v, eg=eg):
            hi, lo = _split2(v)
            return jnp.dot(hi, eg, preferred_element_type=F32) + jnp.dot(lo, eg, preferred_element_type=F32)

        xg = xbc_ref[:, g * GROUP_W:(g + 1) * GROUP_W].astype(F32)
        xd = xg * expand(dt)
        xdb = xd.astype(BF16)
        bg = xbc_ref[:, SSD_D_INNER + g * SSD_STATE:SSD_D_INNER + (g + 1) * SSD_STATE]
        cg = xbc_ref[:, SSD_D_INNER + SSD_BC_DIM + g * SSD_STATE:SSD_D_INNER + SSD_BC_DIM + (g + 1) * SSD_STATE]
        cb = lax.dot_general(cg, bg, (((1,), (1,)), ((), ())), preferred_element_type=F32)
        ys = []
        for p in range(hpg // 2):
            ms = []
            for c in (direction * SSD_HEADS + g * hpg + 2 * p, direction * SSD_HEADS + g * hpg + 2 * p + 1):
                seg = cs[:, c:c + 1] - cs_t[c:c + 1, :]
                ms.append((cb * jnp.exp(jnp.where(mask, seg, -jnp.inf))).astype(BF16))
            lhs = jnp.concatenate(ms, axis=1)
            xp = xdb[:, p * LANES:(p + 1) * LANES]
            zero = jnp.zeros_like(xp)
            rhs = jnp.concatenate([jnp.where(lane < SSD_HEAD_DIM, xp, zero),
                                   jnp.where(lane >= SSD_HEAD_DIM, xp, zero)], axis=0)
            ys.append(jnp.dot(lhs, rhs, preferred_element_type=F32))
        y_diag = jnp.concatenate(ys, axis=1)
        s_in = st_ref[g]
        y_off = jnp.dot(cg, s_in.astype(BF16), preferred_element_type=F32) * expand(ecs)
        y_ref[:, g * GROUP_W:(g + 1) * GROUP_W] = (y_diag + y_off).astype(y_ref.dtype)
        xe = (xd * expand(dec_end)).astype(BF16)
        bg_t = bg.astype(F32).T.astype(BF16)
        new = jnp.dot(bg_t, xe, preferred_element_type=F32)
        st_ref[g] = s_in * expand(cdec)[0:1, :] + new

    @pl.when(t == nt - 1)
    def _():
        for g in range(SSD_GROUPS):
            fin_ref[0, g * hpg:(g + 1) * hpg] = st_ref[g].T.reshape(hpg, SSD_HEAD_DIM, SSD_STATE)


def _ssd_scan_call(xbc_act, dt_raw, dt_bias, a_log, h0, y_into, *, nb, seq, row_off, direction, name):
    q = SSD_CHUNK
    nt = seq // q
    boff = row_off // q
    li = jnp.arange(q)[:, None]
    si = jnp.arange(q)[None, :]
    tri = ((li >= si) if direction == 0 else (li <= si)).astype(BF16)
    hpg = SSD_HEADS // SSD_GROUPS
    rows = jnp.arange(LANES)[None, :, None]
    cols = jnp.arange(GROUP_W)[None, None, :]
    gs = jnp.arange(SSD_GROUPS)[:, None, None]
    expand_mat = (rows == direction * SSD_HEADS + gs * hpg + cols // SSD_HEAD_DIM).astype(BF16)

    def chunk(b, t):
        return boff + b * nt + (t if direction == 0 else nt - 1 - t)

    zero_init = h0 is None
    in_specs = [pl.BlockSpec((q, SSD_CONV_CH), lambda b, t: (chunk(b, t), 0)),
                pl.BlockSpec((q, LANES), lambda b, t: (chunk(b, t), 0)),
                pl.BlockSpec((1, LANES), lambda b, t: (0, 0)),
                pl.BlockSpec((1, LANES), lambda b, t: (0, 0)),
                pl.BlockSpec((q, q), lambda b, t: (0, 0)),
                pl.BlockSpec((SSD_GROUPS, LANES, GROUP_W), lambda b, t: (0, 0, 0))]
    args = [xbc_act, dt_raw, dt_bias.reshape(1, LANES), a_log.reshape(1, LANES), tri, expand_mat]
    if not zero_init:
        in_specs.append(pl.BlockSpec((1, 1, SSD_HEADS, SSD_HEAD_DIM, SSD_STATE),
                                     lambda b, t: (b, direction, 0, 0, 0)))
        args.append(h0)
    aliases = {}
    if y_into is not None:
        in_specs.append(pl.BlockSpec(memory_space=pl.ANY))
        aliases = {len(args): 0}
        args.append(y_into)
    return pl.pallas_call(
        functools.partial(_ssd_kernel, direction=direction, zero_init=zero_init, nt=nt),
        grid=(nb, nt),
        in_specs=in_specs,
        out_specs=[pl.BlockSpec((q, SSD_D_INNER), lambda b, t: (chunk(b, t), 0)),
                   pl.BlockSpec((1, SSD_HEADS, SSD_HEAD_DIM, SSD_STATE), lambda b, t: (b, 0, 0, 0))],
        out_shape=[jax.ShapeDtypeStruct((xbc_act.shape[0], SSD_D_INNER), BF16),
                   jax.ShapeDtypeStruct((nb, SSD_HEADS, SSD_HEAD_DIM, SSD_STATE), F32)],
        scratch_shapes=[pltpu.VMEM((SSD_GROUPS, SSD_STATE, GROUP_W), F32)],
        input_output_aliases=aliases,
        compiler_params=_params("parallel", "arbitrary"),
        name=name,
    )(*args)


def _ssd_post_kernel(yf_ref, yb_ref, xs_ref, z_ref, d_ref, nw_ref, o_ref):
    z = z_ref[...].astype(F32)
    y = yf_ref[...].astype(F32) + yb_ref[...].astype(F32) + d_ref[...] * xs_ref[...].astype(F32)
    y = y * (z / (1.0 + jnp.exp(-z)))
    outs = []
    for g in range(SSD_GROUPS):
        yg = y[:, g * GROUP_W:(g + 1) * GROUP_W]
        ms = jnp.mean(yg * yg, axis=-1, keepdims=True)
        outs.append(yg * lax.rsqrt(ms + NORM_EPS) * nw_ref[:, g * GROUP_W:(g + 1) * GROUP_W])
    o_ref[...] = jnp.concatenate(outs, axis=1).astype(o_ref.dtype)


def _ssd_post_call(y_f, y_b, xbc_act, z, d_tot, norm_w, *, tm, name):
    m = y_f.shape[0]
    di = SSD_D_INNER
    row = pl.BlockSpec((tm, di), lambda i: (i, 0))
    vec = pl.BlockSpec((1, di), lambda i: (0, 0))
    return pl.pallas_call(
        _ssd_post_kernel,
        grid=(m // tm,),
        in_specs=[row, row, row, row, vec, vec],
        out_specs=row,
        out_shape=jax.ShapeDtypeStruct((m, di), BF16),
        compiler_params=_params("parallel"),
        name=name,
    )(y_f, y_b, xbc_act, z, d_tot, norm_w.reshape(1, di))


def _expert_changed(be_ref, i):
    return (i == 0) | (be_ref[i] != be_ref[jnp.maximum(i - 1, 0)])


def _moe_up_kernel(be_ref, nu_ref, x_ref, wg_ref, wu_ref, bg_ref, bu_ref, o_ref, wg_s, wu_s):
    i = pl.program_id(1)

    @pl.when(_expert_changed(be_ref, i))
    def _():
        wg_s[...] = wg_ref[...].astype(BF16)
        wu_s[...] = wu_ref[...].astype(BF16)

    @pl.when(i < nu_ref[0])
    def _():
        x = x_ref[...]
        g = jnp.dot(x, wg_s[...], preferred_element_type=F32) + bg_ref[...]
        u = jnp.dot(x, wu_s[...], preferred_element_type=F32) + bu_ref[...]
        g = jnp.minimum(g, SWIGLU_LIMIT)
        u = jnp.clip(u, -SWIGLU_LIMIT, SWIGLU_LIMIT)
        act = (u + 1.0) * (g / (1.0 + jnp.exp(-SWIGLU_ALPHA * g)))
        o_ref[...] = act.astype(o_ref.dtype)


def _moe_down_kernel(be_ref, nu_ref, a_ref, w_ref, b_ref, o_ref, w_s):
    i = pl.program_id(1)

    @pl.when(_expert_changed(be_ref, i))
    def _():
        w_s[...] = w_ref[...].astype(BF16)

    @pl.when(i < nu_ref[0])
    def _():
        y = jnp.dot(a_ref[...], w_s[...], preferred_element_type=F32) + b_ref[...]
        o_ref[...] = y.astype(o_ref.dtype)


def _moe_experts(x_sorted, block_e, n_used, layer, w_gate_up, b_gate_up, w_down, b_down, *, tn_up, tn_down,
                 name):
    slots, d = x_sorted.shape
    nblk = slots // MOE_BLOCK
    nj = D_FF // tn_up
    nl = w_gate_up.shape[0]
    b_gu = b_gate_up.reshape(nl, N_EXPERTS, 1, 2 * D_FF)
    act = pl.pallas_call(
        _moe_up_kernel,
        grid_spec=pltpu.PrefetchScalarGridSpec(
            num_scalar_prefetch=2,
            grid=(nj, nblk),
            in_specs=[pl.BlockSpec((MOE_BLOCK, d), lambda j, i, be, nu: (i, 0)),
                      pl.BlockSpec((None, None, d, tn_up), lambda j, i, be, nu: (layer, be[i], 0, j)),
                      pl.BlockSpec((None, None, d, tn_up), lambda j, i, be, nu: (layer, be[i], 0, nj + j)),
                      pl.BlockSpec((None, None, 1, tn_up), lambda j, i, be, nu: (layer, be[i], 0, j)),
                      pl.BlockSpec((None, None, 1, tn_up), lambda j, i, be, nu: (layer, be[i], 0, nj + j))],
            out_specs=pl.BlockSpec((MOE_BLOCK, tn_up), lambda j, i, be, nu: (i, j)),
            scratch_shapes=[pltpu.VMEM((d, tn_up), BF16), pltpu.VMEM((d, tn_up), BF16)]),
        out_shape=jax.ShapeDtypeStruct((slots, D_FF), BF16),
        compiler_params=_params("arbitrary", "arbitrary"),
        name=name + "_up",
    )(block_e, n_used, x_sorted, w_gate_up, w_gate_up, b_gu, b_gu)
    nj2 = d // tn_down
    return pl.pallas_call(
        _moe_down_kernel,
        grid_spec=pltpu.PrefetchScalarGridSpec(
            num_scalar_prefetch=2,
            grid=(nj2, nblk),
            in_specs=[pl.BlockSpec((MOE_BLOCK, D_FF), lambda j, i, be, nu: (i, 0)),
                      pl.BlockSpec((None, None, D_FF, tn_down), lambda j, i, be, nu: (layer, be[i], 0, j)),
                      pl.BlockSpec((None, None, 1, tn_down), lambda j, i, be, nu: (layer, be[i], 0, j))],
            out_specs=pl.BlockSpec((MOE_BLOCK, tn_down), lambda j, i, be, nu: (i, j)),
            scratch_shapes=[pltpu.VMEM((D_FF, tn_down), BF16)]),
        out_shape=jax.ShapeDtypeStruct((slots, d), BF16),
        compiler_params=_params("arbitrary", "arbitrary"),
        name=name + "_down",
    )(block_e, n_used, act, w_down, b_down.reshape(nl, N_EXPERTS, 1, d))


def _combine_kernel(y_ref, gate_ref, x_ref, g2_ref, *o_refs, split_tile):
    d = x_ref.shape[1]
    gates = gate_ref[...]
    f = None
    for k in range(TOP_K):
        term = gates[:, k:k + 1] * y_ref[:, k * d:(k + 1) * d].astype(F32)
        f = term if f is None else f + term
    out = x_ref[...] + g2_ref[0] * f
    if split_tile is None:
        o_refs[0][...] = out
    else:
        i = pl.program_id(0)

        @pl.when(i < split_tile)
        def _():
            o_refs[0][...] = out

        @pl.when(i >= split_tile)
        def _():
            o_refs[1][...] = out


def _combine_call(y4, gate_pad, x, mod, n_p, ls, *, tm, split, name):
    n, d = x.shape
    st = n_p // tm
    if split:
        out_specs = [pl.BlockSpec((tm, d), lambda i, j: (jnp.minimum(i, st - 1), 0)),
                     pl.BlockSpec((tm, d), lambda i, j: (jnp.maximum(i - st, 0), 0))]
        out_shape = [jax.ShapeDtypeStruct((n_p, d), F32), jax.ShapeDtypeStruct((n - n_p, d), F32)]
    else:
        out_specs = pl.BlockSpec((tm, d), lambda i, j: (i, 0))
        out_shape = jax.ShapeDtypeStruct((n, d), F32)
    return pl.pallas_call(
        functools.partial(_combine_kernel, split_tile=st if split else None),
        grid=(n // tm, 1),
        in_specs=[pl.BlockSpec((tm, TOP_K * d), lambda i, j: (i, 0)),
                  pl.BlockSpec((tm, LANES), lambda i, j: (i, 0)),
                  pl.BlockSpec((tm, d), lambda i, j: (i, 0)),
                  _mod_spec(5, tm, d, n_p, ls)],
        out_specs=out_specs,
        out_shape=out_shape,
        compiler_params=_params("arbitrary", "arbitrary"),
        name=name,
    )(y4, gate_pad, x, mod)


def _moe_layer(x, mod, n_p, ls, layer, norm_w, w_router, b_router, w_gate_up, b_gate_up, w_down, b_down, *,
               split, name):
    n, d = x.shape
    h, idx_pad, gate_pad = _router_call(x, norm_w.reshape(1, d), mod, n_p, ls, w_router, b_router,
                                        tm=256, name=name + "_router")
    top_idx = idx_pad[:, :TOP_K]
    n_assign = n * TOP_K
    flat_e = top_idx.reshape(-1)
    onehot = (flat_e[:, None] == jnp.arange(N_EXPERTS, dtype=jnp.int32)[None, :]).astype(jnp.int32)
    csum = jnp.cumsum(onehot, axis=0)
    counts = csum[-1]
    rank = jnp.take_along_axis(csum, flat_e[:, None], axis=1)[:, 0] - 1
    padded = (counts + MOE_BLOCK - 1) // MOE_BLOCK * MOE_BLOCK
    pad_end = jnp.cumsum(padded)
    pad_start = pad_end - padded
    dest = pad_start[flat_e] + rank
    n_blocks = -(-n_assign // MOE_BLOCK) + N_EXPERTS
    slots = n_blocks * MOE_BLOCK
    slot_tok = jnp.zeros((slots,), jnp.int32).at[dest].set(jnp.arange(n_assign, dtype=jnp.int32) // TOP_K,
                                                         unique_indices=True, mode="promise_in_bounds")
    block_e = jnp.minimum(jnp.searchsorted(pad_end, jnp.arange(n_blocks, dtype=jnp.int32) * MOE_BLOCK,
                                           side="right"), N_EXPERTS - 1).astype(jnp.int32)
    n_used = (pad_end[-1] // MOE_BLOCK).astype(jnp.int32).reshape(1)
    x_sorted = h.at[slot_tok].get(mode="promise_in_bounds")
    y_sorted = _moe_experts(x_sorted, block_e, n_used, layer, w_gate_up, b_gate_up, w_down, b_down,
                            tn_up=1024, tn_down=2048, name=name)
    y4 = y_sorted.at[dest].get(mode="promise_in_bounds").reshape(n, TOP_K * d)
    return _combine_call(y4, gate_pad, x, mod, n_p, ls, tm=256, split=split, name=name + "_combine")


def _ada_call(cond, w_ada, b_ada, layer, *, name):
    g = cond.shape[0]
    a = jnp.pad(jax.nn.silu(cond), ((0, 16 - g), (0, 0))).astype(BF16)
    m = _matmul(a, w_ada, n_out=6 * D_MODEL, tm=16, tn=1024, out_dtype=F32, name=name, layer=layer,
                bias=b_ada.reshape(b_ada.shape[0], 1, 6 * D_MODEL))
    return m[:g].reshape(g * 6, 1, D_MODEL)


def kernel(x_prompt, x_sample, c, c_ctx, cache_k, cache_v, state_ssm, norm1_w, norm2_w, w_ada, b_ada, w_qkv, q_norm_w, k_norm_w, lambda_q1, lambda_k1, lambda_q2, lambda_k2, subln_w, w_o, w_in_ssd, conv_w, conv_b, dt_bias, a_log, d_skip, ssd_norm_w, w_out_ssd, w_router, b_router, w_gate_up, b_gate_up, w_down, b_down):
    bp, lp, d = x_prompt.shape
    bs, ls, _ = x_sample.shape
    past = cache_k.shape[2]
    n_p, n_s = bp * lp, bs * ls
    x = jnp.concatenate([x_prompt.reshape(n_p, d), x_sample.reshape(n_s, d)], axis=0)
    cond = jnp.concatenate([c_ctx[None], c], axis=0)
    resid_of = lambda xx, mod, which: (xx, mod, which, n_p, ls)

    n = n_p + n_s
    mod = _ada_call(cond, w_ada, b_ada, 0, name="ada0")
    h = _modnorm_call(x, norm1_w[0].reshape(1, d), mod, 0, n_p, ls, tm=256, name="norm1_0")
    wq = w_qkv[0].astype(BF16)
    qk_w = jnp.stack([q_norm_w[0], k_norm_w[0]], axis=0)
    lam_init = 0.8 - 0.6 * math.exp(-0.3 * 0)
    lam_params = jnp.stack([lambda_q1[0], lambda_k1[0], lambda_q2[0], lambda_k2[0]], axis=0)
    qp, kp, vp = _qkv_call(h, wq, qk_w, None, row_off=0, m=n_p, kv_dtype=F32, tm=512, tn=512, ls=ls,
                           name="qkv_prompt")
    qs, ks, vs = _qkv_call(h, wq, qk_w, _rope_tables(ls), row_off=n_p, m=n_s, kv_dtype=BF16, tm=512, tn=512,
                           ls=ls, name="qkv_sample")
    o = _attn_call(qp, [(kp, vp, lp)], lam_params, subln_w[0], None, n_rows=n, row_off=0, nb=bp, lq=lp,
                   tq=lp, lam_init=lam_init, name="attn_prompt")
    ck = cache_k[:, 0].reshape(bs * past, d)
    cv = cache_v[:, 0].reshape(bs * past, d)
    o = _attn_call(qs, [(ck, cv, past), (ks, vs, ls)], lam_params, subln_w[0], o, n_rows=n, row_off=n_p,
                   nb=bs, lq=ls, tq=256, lam_init=lam_init, name="attn_sample")
    x = _matmul(o, w_o[0].astype(BF16), n_out=d, tm=1024, tn=512, out_dtype=F32, name="attn_out",
                resid=resid_of(x, mod, 2))
    x = _moe_layer(x, mod, n_p, ls, 0, norm2_w[0], w_router[0], b_router[0], w_gate_up, b_gate_up,
                   w_down, b_down, split=False, name="moe0")
    new_k = kp.reshape(bp, 1, lp, ATTN_HEADS, 2, QK_DIM)
    new_v = vp.reshape(bp, 1, lp, ATTN_HEADS, V_DIM)

    mod = _ada_call(cond, w_ada, b_ada, 1, name="ada1")
    h = _modnorm_call(x, norm1_w[1].reshape(1, d), mod, 0, n_p, ls, tm=256, name="norm1_1")
    w_in = w_in_ssd[0].astype(BF16)
    z = _matmul(h, w_in, n_out=SSD_D_INNER, tm=1024, tn=512, out_dtype=BF16, name="ssd_in_z")
    xbc = _matmul(h, w_in, n_out=SSD_CONV_CH, col_off=SSD_D_INNER, tm=1024, tn=512, out_dtype=BF16,
                  name="ssd_in_xbc")
    dt_raw = _matmul(h, w_in, n_out=2 * SSD_HEADS, col_off=SSD_D_INNER + SSD_CONV_CH, tm=1024, tn=LANES,
                     out_dtype=F32, name="ssd_in_dt")
    xbc_act = _conv_call(xbc, conv_w[0], conv_b[0], n_p=n_p, lp=lp, ls=ls, tc=256, tcn=2048, name="ssd_conv")
    ys, fins = [], []
    for direction in (0, 1):
        y, fp = _ssd_scan_call(xbc_act, dt_raw, dt_bias[0], a_log[0], None, None, nb=bp, seq=lp, row_off=0,
                               direction=direction, name="ssd_scan_prompt%d" % direction)
        y, _ = _ssd_scan_call(xbc_act, dt_raw, dt_bias[0], a_log[0], state_ssm[:, 0], y, nb=bs, seq=ls,
                              row_off=n_p, direction=direction, name="ssd_scan_sample%d" % direction)
        ys.append(y)
        fins.append(fp)
    d_tot = jnp.repeat(d_skip[0, 0] + d_skip[0, 1], SSD_HEAD_DIM).reshape(1, SSD_D_INNER)
    yn = _ssd_post_call(ys[0], ys[1], xbc_act, z, d_tot, ssd_norm_w[0], tm=256, name="ssd_post")
    x = _matmul(yn, w_out_ssd[0].astype(BF16), n_out=d, tm=1024, tn=512, out_dtype=F32, name="ssd_out",
                resid=resid_of(x, mod, 2))
    xp, xs = _moe_layer(x, mod, n_p, ls, 1, norm2_w[1], w_router[1], b_router[1], w_gate_up, b_gate_up,
                        w_down, b_down, split=True, name="moe1")
    new_s = jnp.stack(fins, axis=1)[:, None]
    return (xp.reshape(bp, lp, d), xs.reshape(bs, ls, d), new_k, new_v, new_s)
```

```python
import functools
import math

import jax
import jax.numpy as jnp
from jax import lax
from jax.experimental import pallas as pl
from jax.experimental.pallas import tpu as pltpu

F32 = jnp.float32
BF16 = jnp.bfloat16

D_MODEL = 2048
NORM_EPS = 1e-6
GRID_W = 64
ROPE_THETA = 10000.0
ATTN_HEADS = 8
QK_DIM = 128
V_DIM = 256
SSD_D_INNER = 4096
SSD_HEAD_DIM = 64
SSD_HEADS = 64
SSD_GROUPS = 8
SSD_STATE = 128
SSD_CONV = 5
SSD_CHUNK = 128
SSD_BC_DIM = SSD_GROUPS * SSD_STATE
SSD_CONV_CH = SSD_D_INNER + 2 * SSD_BC_DIM
N_EXPERTS = 32
TOP_K = 4
D_FF = 2048
SWIGLU_LIMIT = 7.0
SWIGLU_ALPHA = 1.702
MOE_BLOCK = 256
LANES = 128
GROUP_W = SSD_D_INNER // SSD_GROUPS
VMEM_LIMIT = 56 * 1024 * 1024


def _params(*sem):
    return pltpu.CompilerParams(dimension_semantics=sem, vmem_limit_bytes=VMEM_LIMIT)


def _group_of_tile(i, tm, n_p, ls):
    return jnp.where(i * tm < n_p, 0, 1 + (i * tm - n_p) // ls)


def _mod_spec(which, tm, tn, n_p, ls):
    return pl.BlockSpec((1, 1, tn), lambda i, j: (_group_of_tile(i, tm, n_p, ls) * 6 + which, 0, j))


def _mm_kernel(a_ref, w_ref, *rest, epilogue):
    acc = jnp.dot(a_ref[...].astype(BF16), w_ref[...].astype(BF16), preferred_element_type=F32)
    if epilogue == "bias":
        b_ref, o_ref = rest
        o_ref[...] = (acc + b_ref[...]).astype(o_ref.dtype)
    elif epilogue == "resid":
        x_ref, g_ref, o_ref = rest
        o_ref[...] = x_ref[...] + g_ref[0] * acc
    else:
        (o_ref,) = rest
        o_ref[...] = acc.astype(o_ref.dtype)


def _matmul(a, w, *, n_out, col_off=0, tm, tn, out_dtype, name, layer=None, bias=None, resid=None):
    m, k = a.shape
    joff = col_off // tn
    if layer is None:
        w_spec = pl.BlockSpec((k, tn), lambda i, j: (0, j + joff))
        b_spec = pl.BlockSpec((1, tn), lambda i, j: (0, j))
    else:
        w_spec = pl.BlockSpec((None, k, tn), lambda i, j: (layer, 0, j + joff))
        b_spec = pl.BlockSpec((None, 1, tn), lambda i, j: (layer, 0, j))
    in_specs = [pl.BlockSpec((tm, k), lambda i, j: (i, 0)), w_spec]
    args = [a, w]
    if bias is not None:
        epilogue = "bias"
        in_specs.append(b_spec)
        args.append(bias)
    elif resid is not None:
        epilogue = "resid"
        x, mod, which, n_p, ls = resid
        in_specs += [pl.BlockSpec((tm, tn), lambda i, j: (i, j)), _mod_spec(which, tm, tn, n_p, ls)]
        args += [x, mod]
    else:
        epilogue = "plain"
    return pl.pallas_call(
        functools.partial(_mm_kernel, epilogue=epilogue),
        grid=(m // tm, n_out // tn),
        in_specs=in_specs,
        out_specs=pl.BlockSpec((tm, tn), lambda i, j: (i, j)),
        out_shape=jax.ShapeDtypeStruct((m, n_out), out_dtype),
        compiler_params=_params("parallel", "arbitrary"),
        name=name,
    )(*args)


def _modnorm(x, w, shift, scale):
    ms = jnp.mean(x * x, axis=-1, keepdims=True)
    return (x * lax.rsqrt(ms + NORM_EPS) * w) * (1.0 + scale) + shift


def _modnorm_kernel(x_ref, w_ref, sh_ref, sc_ref, o_ref):
    o_ref[...] = _modnorm(x_ref[...], w_ref[...], sh_ref[0], sc_ref[0]).astype(o_ref.dtype)


def _modnorm_call(x, w, mod, which_shift, n_p, ls, *, tm, name):
    m, d = x.shape
    return pl.pallas_call(
        _modnorm_kernel,
        grid=(m // tm, 1),
        in_specs=[pl.BlockSpec((tm, d), lambda i, j: (i, 0)),
                  pl.BlockSpec((1, d), lambda i, j: (0, 0)),
                  _mod_spec(which_shift, tm, d, n_p, ls),
                  _mod_spec(which_shift + 1, tm, d, n_p, ls)],
        out_specs=pl.BlockSpec((tm, d), lambda i, j: (i, 0)),
        out_shape=jax.ShapeDtypeStruct((m, d), BF16),
        compiler_params=_params("parallel", "arbitrary"),
        name=name,
    )(x, w, mod, mod)


def _split2(v):
    hi = v.astype(BF16)
    lo = (v - hi.astype(F32)).astype(BF16)
    return hi, lo


def _router_kernel(x_ref, w_ref, sh_ref, sc_ref, wr_ref, br_ref, h_ref, idx_ref, gate_ref, rank_ref, cnt_ref,
                   run_ref):
    @pl.when(pl.program_id(0) == 0)
    def _():
        run_ref[...] = jnp.zeros_like(run_ref)

    h = _modnorm(x_ref[...], w_ref[...], sh_ref[0], sc_ref[0])
    h_ref[...] = h.astype(BF16)
    h_hi, h_lo = _split2(h)
    w_hi, w_lo = _split2(wr_ref[...])
    logits = (jnp.dot(h_hi, w_hi, preferred_element_type=F32)
              + jnp.dot(h_hi, w_lo, preferred_element_type=F32)
              + jnp.dot(h_lo, w_hi, preferred_element_type=F32)) + br_ref[...]
    lane = lax.broadcasted_iota(jnp.int32, logits.shape, 1).astype(F32)
    cur = jnp.where(lane < N_EXPERTS, logits, -jnp.inf)
    vals, idxs = [], []
    for _ in range(TOP_K):
        m = jnp.max(cur, axis=-1, keepdims=True)
        am = jnp.min(jnp.where(cur == m, lane, float(LANES)), axis=-1, keepdims=True)
        vals.append(m)
        idxs.append(am)
        cur = jnp.where(lane == am, -jnp.inf, cur)
    exps = [jnp.exp(v - vals[0]) for v in vals]
    denom = exps[0] + exps[1] + exps[2] + exps[3]
    idx_out = jnp.zeros(logits.shape, F32)
    gate_out = jnp.zeros(logits.shape, F32)
    for k in range(TOP_K):
        idx_out = jnp.where(lane == k, idxs[k], idx_out)
        gate_out = jnp.where(lane == k, exps[k] / denom, gate_out)
    idx_ref[...] = idx_out.astype(jnp.int32)
    gate_ref[...] = gate_out
    tm = logits.shape[0]
    chosen = functools.reduce(jnp.logical_or, [lane == am for am in idxs])
    onehot = jnp.where(chosen, 1.0, 0.0)
    ti = lax.broadcasted_iota(jnp.int32, (tm, tm), 0)
    tj = lax.broadcasted_iota(jnp.int32, (tm, tm), 1)
    earlier = jnp.where(ti > tj, 1.0, 0.0).astype(BF16)
    base = run_ref[...] + jnp.dot(earlier, onehot.astype(BF16), preferred_element_type=F32)
    rank_out = jnp.zeros(logits.shape, F32)
    for k in range(TOP_K):
        rk = jnp.sum(jnp.where(lane == idxs[k], base, 0.0), axis=-1, keepdims=True)
        rank_out = jnp.where(lane == k, rk, rank_out)
    rank_ref[...] = rank_out.astype(jnp.int32)
    run_ref[...] = run_ref[...] + jnp.sum(onehot, axis=0, keepdims=True)
    cnt_ref[...] = run_ref[...].astype(jnp.int32)


def _router_call(x, w, mod, n_p, ls, w_router, b_router, *, tm, name):
    m, d = x.shape
    wr = jnp.pad(w_router, ((0, 0), (0, LANES - N_EXPERTS)))
    br = jnp.pad(b_router, (0, LANES - N_EXPERTS)).reshape(1, LANES)
    return pl.pallas_call(
        _router_kernel,
        grid=(m // tm, 1),
        in_specs=[pl.BlockSpec((tm, d), lambda i, j: (i, 0)),
                  pl.BlockSpec((1, d), lambda i, j: (0, 0)),
                  _mod_spec(3, tm, d, n_p, ls),
                  _mod_spec(4, tm, d, n_p, ls),
                  pl.BlockSpec((d, LANES), lambda i, j: (0, 0)),
                  pl.BlockSpec((1, LANES), lambda i, j: (0, 0))],
        out_specs=[pl.BlockSpec((tm, d), lambda i, j: (i, 0)),
                   pl.BlockSpec((tm, LANES), lambda i, j: (i, 0)),
                   pl.BlockSpec((tm, LANES), lambda i, j: (i, 0)),
                   pl.BlockSpec((tm, LANES), lambda i, j: (i, 0)),
                   pl.BlockSpec((1, LANES), lambda i, j: (0, 0))],
        out_shape=[jax.ShapeDtypeStruct((m, d), BF16),
                   jax.ShapeDtypeStruct((m, LANES), jnp.int32),
                   jax.ShapeDtypeStruct((m, LANES), F32),
                   jax.ShapeDtypeStruct((m, LANES), jnp.int32),
                   jax.ShapeDtypeStruct((1, LANES), jnp.int32)],
        scratch_shapes=[pltpu.VMEM((1, LANES), F32)],
        compiler_params=_params("arbitrary", "arbitrary"),
        name=name,
    )(x, w, mod, mod, wr, br)


def _qkv_kernel(a_ref, w_ref, nw_ref, *rest, rope, tn):
    if rope:
        cos_ref, sin_ref, q_ref, k_ref, v_ref = rest
    else:
        q_ref, k_ref, v_ref = rest
    j = pl.program_id(1)
    nq = D_MODEL // tn
    acc = jnp.dot(a_ref[...], w_ref[...], preferred_element_type=F32)

    def normed(widx, out_scale):
        w = nw_ref[widx:widx + 1, :]
        outs = []
        for c in range(tn // QK_DIM):
            xc = acc[:, c * QK_DIM:(c + 1) * QK_DIM]
            ms = jnp.mean(xc * xc, axis=-1, keepdims=True)
            y = xc * lax.rsqrt(ms + NORM_EPS) * w
            if rope:
                lane = lax.broadcasted_iota(jnp.int32, y.shape, 1)
                partner = jnp.where(lane % 64 < 32, pltpu.roll(y, QK_DIM - 32, 1), pltpu.roll(y, 32, 1))
                y = y * cos_ref[...] + partner * sin_ref[...]
            outs.append(y * out_scale if out_scale != 1.0 else y)
        return jnp.concatenate(outs, axis=1)

    @pl.when(j < nq)
    def _():
        q_ref[...] = normed(0, QK_DIM ** -0.5).astype(q_ref.dtype)

    @pl.when((j >= nq) & (j < 2 * nq))
    def _():
        k_ref[...] = normed(1, 1.0).astype(k_ref.dtype)

    @pl.when(j >= 2 * nq)
    def _():
        v_ref[...] = acc.astype(v_ref.dtype)


def _qkv_call(h, w_qkv, qk_norm_w, rope_tabs, *, row_off, m, kv_dtype, tm, tn, ls, name):
    d = h.shape[1]
    nq = d // tn
    ioff = row_off // tm
    rope = rope_tabs is not None
    in_specs = [pl.BlockSpec((tm, d), lambda i, j: (i + ioff, 0)),
                pl.BlockSpec((d, tn), lambda i, j: (0, j)),
                pl.BlockSpec((2, QK_DIM), lambda i, j: (0, 0))]
    args = [h, w_qkv, qk_norm_w]
    if rope:
        nt = ls // tm
        in_specs += [pl.BlockSpec((tm, QK_DIM), lambda i, j: (i % nt, 0))] * 2
        args += list(rope_tabs)
    return pl.pallas_call(
        functools.partial(_qkv_kernel, rope=rope, tn=tn),
        grid=(m // tm, 3 * nq),
        in_specs=in_specs,
        out_specs=[pl.BlockSpec((tm, tn), lambda i, j: (i, jnp.minimum(j, nq - 1))),
                   pl.BlockSpec((tm, tn), lambda i, j: (i, jnp.clip(j - nq, 0, nq - 1))),
                   pl.BlockSpec((tm, tn), lambda i, j: (i, jnp.clip(j - 2 * nq, 0, nq - 1)))],
        out_shape=[jax.ShapeDtypeStruct((m, d), BF16),
                   jax.ShapeDtypeStruct((m, d), kv_dtype),
                   jax.ShapeDtypeStruct((m, d), kv_dtype)],
        compiler_params=_params("parallel", "arbitrary"),
        name=name,
    )(*args)


def _rope_tables(ls):
    rows = ls // GRID_W
    row = jnp.repeat(jnp.arange(rows, dtype=F32), GRID_W)
    col = jnp.tile(jnp.arange(GRID_W, dtype=F32), rows)
    axis_dim = QK_DIM // 2
    inv = ROPE_THETA ** (-jnp.arange(0, axis_dim, 2, dtype=F32) / axis_dim)
    ar, ac = row[:, None] * inv, col[:, None] * inv
    cos = jnp.concatenate([jnp.cos(ar), jnp.cos(ar), jnp.cos(ac), jnp.cos(ac)], axis=1)
    sin = jnp.concatenate([-jnp.sin(ar), jnp.sin(ar), -jnp.sin(ac), jnp.sin(ac)], axis=1)
    return cos, sin


def _attn_kernel(lam_ref, subw_ref, q_ref, *rest, nseg, lam_init):
    k_refs, v_refs, o_ref = rest[:nseg], rest[nseg:2 * nseg], rest[-1]
    lp = lam_ref[...]
    lam = (jnp.exp(jnp.sum(lp[0:1] * lp[1:2], axis=-1, keepdims=True))
           - jnp.exp(jnp.sum(lp[2:3] * lp[3:4], axis=-1, keepdims=True)) + lam_init)
    q = q_ref[...]
    ks = [r[...].astype(BF16) for r in k_refs]
    vs = [r[...].astype(BF16) for r in v_refs]
    outs = []
    for mi in range(2):
        qm = q[:, mi * QK_DIM:(mi + 1) * QK_DIM]
        ss = [lax.dot_general(qm, k[:, mi * QK_DIM:(mi + 1) * QK_DIM], (((1,), (1,)), ((), ())),
                              preferred_element_type=F32) for k in ks]
        mx = functools.reduce(jnp.maximum, [jnp.max(s, axis=-1, keepdims=True) for s in ss])
        ps = [jnp.exp(s - mx) for s in ss]
        den = functools.reduce(jnp.add, [jnp.sum(p, axis=-1, keepdims=True) for p in ps])
        pv = functools.reduce(jnp.add, [jnp.dot(p.astype(BF16), v, preferred_element_type=F32)
                                        for p, v in zip(ps, vs)])
        outs.append(pv * (1.0 / den))
    acc = outs[0] - lam * outs[1]
    ms = jnp.mean(acc * acc, axis=-1, keepdims=True)
    o = acc * lax.rsqrt(ms + NORM_EPS) * subw_ref[...] * (1.0 - lam_init)
    o_ref[...] = o.astype(o_ref.dtype)


def _attn_call(q, kv_segs, lam_params, subln_w, o_into, *, n_rows, row_off, nb, lq, tq, lam_init, name):
    nqb = lq // tq
    nseg = len(kv_segs)
    ooff = row_off // tq
    in_specs = [pl.BlockSpec((4, QK_DIM), lambda b, h, i: (0, 0)),
                pl.BlockSpec((1, V_DIM), lambda b, h, i: (0, 0)),
                pl.BlockSpec((tq, V_DIM), lambda b, h, i: (b * nqb + i, h))]
    in_specs += [pl.BlockSpec((lk, V_DIM), lambda b, h, i: (b, h)) for (_, _, lk) in kv_segs] * 2
    args = [lam_params, subln_w.reshape(1, V_DIM), q]
    args += [k for (k, _, _) in kv_segs] + [v for (_, v, _) in kv_segs]
    aliases = {}
    if o_into is not None:
        in_specs.append(pl.BlockSpec(memory_space=pl.ANY))
        aliases = {len(args): 0}
        args.append(o_into)
    return pl.pallas_call(
        functools.partial(_attn_kernel, nseg=nseg, lam_init=lam_init),
        grid=(nb, ATTN_HEADS, nqb),
        in_specs=in_specs,
        out_specs=pl.BlockSpec((tq, V_DIM), lambda b, h, i: (ooff + b * nqb + i, h)),
        out_shape=jax.ShapeDtypeStruct((n_rows, q.shape[1]), BF16),
        input_output_aliases=aliases,
        compiler_params=_params("parallel", "parallel", "arbitrary"),
        name=name,
    )(*args)


def _conv_kernel(xp_ref, x_ref, xn_ref, w_ref, b_ref, o_ref, ext_ref, *, tc, n_p, lp, ls, halo):
    i = pl.program_id(0)
    row0 = i * tc
    in_prompt = row0 < n_p
    first = jnp.where(in_prompt, row0 % lp == 0, (row0 - n_p) % ls == 0)
    last = jnp.where(in_prompt, (row0 + tc) % lp == 0, (row0 + tc - n_p) % ls == 0)
    prev = xp_ref[...].astype(F32)[halo - 8:halo, :]
    nxt = xn_ref[...].astype(F32)[0:8, :]
    ext_ref[0:8, :] = jnp.where(first, 0.0, prev)
    ext_ref[8:8 + tc, :] = x_ref[...].astype(F32)
    ext_ref[8 + tc:16 + tc, :] = jnp.where(last, 0.0, nxt)
    acc = b_ref[...] + w_ref[0:1, :] * ext_ref[6:6 + tc, :]
    for k in range(1, SSD_CONV):
        acc = acc + w_ref[k:k + 1, :] * ext_ref[6 + k:6 + k + tc, :]
    o_ref[...] = (acc / (1.0 + jnp.exp(-acc))).astype(o_ref.dtype)


def _conv_call(xbc, conv_w, conv_b, *, n_p, lp, ls, tc, tcn, name):
    m, ch = xbc.shape
    halo = 16
    hb = tc // halo
    nhb = m // halo
    return pl.pallas_call(
        functools.partial(_conv_kernel, tc=tc, n_p=n_p, lp=lp, ls=ls, halo=halo),
        grid=(m // tc, ch // tcn),
        in_specs=[pl.BlockSpec((halo, tcn), lambda i, j: (jnp.maximum(i * hb - 1, 0), j)),
                  pl.BlockSpec((tc, tcn), lambda i, j: (i, j)),
                  pl.BlockSpec((halo, tcn), lambda i, j: (jnp.minimum((i + 1) * hb, nhb - 1), j)),
                  pl.BlockSpec((SSD_CONV, tcn), lambda i, j: (0, j)),
                  pl.BlockSpec((1, tcn), lambda i, j: (0, j))],
        out_specs=pl.BlockSpec((tc, tcn), lambda i, j: (i, j)),
        out_shape=jax.ShapeDtypeStruct((m, ch), BF16),
        scratch_shapes=[pltpu.VMEM((tc + 16, tcn), F32)],
        compiler_params=_params("parallel", "arbitrary"),
        name=name,
    )(xbc, xbc, xbc, conv_w, conv_b.reshape(1, ch))


def _ssd_kernel(xbc_ref, dtraw_ref, dtb_ref, alog_ref, tri_ref, e_ref, *rest, direction, zero_init, nt):
    y_ref, fin_ref, st_ref = rest[-3:]
    if not zero_init:
        h0_ref = rest[0]
    t = pl.program_id(1)
    q = SSD_CHUNK
    hpg = SSD_HEADS // SSD_GROUPS

    @pl.when(t == 0)
    def _():
        for g in range(SSD_GROUPS):
            if zero_init:
                st_ref[g] = jnp.zeros((SSD_STATE, GROUP_W), F32)
            else:
                hg = h0_ref[0, 0, g * hpg:(g + 1) * hpg].reshape(GROUP_W, SSD_STATE)
                st_ref[g] = hg.T

    x = dtraw_ref[...] + dtb_ref[...]
    dt = jnp.maximum(x, 0.0) + jnp.log1p(jnp.exp(-jnp.abs(x)))
    a = dt * (-jnp.exp(alog_ref[...]))
    tri = tri_ref[...]
    a1 = a.astype(BF16)
    r1 = a - a1.astype(F32)
    a2 = r1.astype(BF16)
    a3 = (r1 - a2.astype(F32)).astype(BF16)
    cs = (jnp.dot(tri, a1, preferred_element_type=F32) + jnp.dot(tri, a2, preferred_element_type=F32)
          + jnp.dot(tri, a3, preferred_element_type=F32))
    cs_t = cs.T
    tot = cs[q - 1:q, :] if direction == 0 else cs[0:1, :]
    dec_end = jnp.exp(tot - cs)
    ecs = jnp.exp(cs)
    cdec = jnp.broadcast_to(jnp.exp(tot), (8, LANES))
    li = lax.broadcasted_iota(jnp.int32, (q, q), 0)
    si = lax.broadcasted_iota(jnp.int32, (q, q), 1)
    mask = (li >= si) if direction == 0 else (li <= si)
    lane = lax.broadcasted_iota(jnp.int32, (q, LANES), 1)

    for g in range(SSD_GROUPS):
        eg = e_ref[g]

        def expand(v, eg=eg):
            hi, lo = _split2(v)
            return jnp.dot(hi, eg, preferred_element_type=F32) + jnp.dot(lo, eg, preferred_element_type=F32)

        xg = xbc_ref[:, g * GROUP_W:(g + 1) * GROUP_W].astype(F32)
        xd = xg * expand(dt)
        xdb = xd.astype(BF16)
        bg = xbc_ref[:, SSD_D_INNER + g * SSD_STATE:SSD_D_INNER + (g + 1) * SSD_STATE]
        cg = xbc_ref[:, SSD_D_INNER + SSD_BC_DIM + g * SSD_STATE:SSD_D_INNER + SSD_BC_DIM + (g + 1) * SSD_STATE]
        cb = lax.dot_general(cg, bg, (((1,), (1,)), ((), ())), preferred_element_type=F32)
        ys = []
        for p in range(hpg // 2):
            ms = []
            for c in (direction * SSD_HEADS + g * hpg + 2 * p, direction * SSD_HEADS + g * hpg + 2 * p + 1):
                seg = cs[:, c:c + 1] - cs_t[c:c + 1, :]
                ms.append((cb * jnp.exp(jnp.where(mask, seg, -jnp.inf))).astype(BF16))
            lhs = jnp.concatenate(ms, axis=1)
            xp = xdb[:, p * LANES:(p + 1) * LANES]
            zero = jnp.zeros_like(xp)
            rhs = jnp.concatenate([jnp.where(lane < SSD_HEAD_DIM, xp, zero),
                                   jnp.where(lane >= SSD_HEAD_DIM, xp, zero)], axis=0)
            ys.append(jnp.dot(lhs, rhs, preferred_element_type=F32))
        y_diag = jnp.concatenate(ys, axis=1)
        s_in = st_ref[g]
        y_off = jnp.dot(cg, s_in.astype(BF16), preferred_element_type=F32) * expand(ecs)
        y_ref[:, g * GROUP_W:(g + 1) * GROUP_W] = (y_diag + y_off).astype(y_ref.dtype)
        xe = (xd * expand(dec_end)).astype(BF16)
        bg_t = bg.astype(F32).T.astype(BF16)
        new = jnp.dot(bg_t, xe, preferred_element_type=F32)
        st_ref[g] = s_in * expand(cdec)[0:1, :] + new

    @pl.when(t == nt - 1)
    def _():
        for g in range(SSD_GROUPS):
            fin_ref[0, g * hpg:(g + 1) * hpg] = st_ref[g].T.reshape(hpg, SSD_HEAD_DIM, SSD_STATE)


def _ssd_scan_call(xbc_act, dt_raw, dt_bias, a_log, h0, y_into, fin_into, *, nb, seq, row_off, direction,
                   name):
    q = SSD_CHUNK
    nt = seq // q
    boff = row_off // q
    li = jnp.arange(q)[:, None]
    si = jnp.arange(q)[None, :]
    tri = ((li >= si) if direction == 0 else (li <= si)).astype(BF16)
    hpg = SSD_HEADS // SSD_GROUPS
    rows = jnp.arange(LANES)[None, :, None]
    cols = jnp.arange(GROUP_W)[None, None, :]
    gs = jnp.arange(SSD_GROUPS)[:, None, None]
    expand_mat = (rows == direction * SSD_HEADS + gs * hpg + cols // SSD_HEAD_DIM).astype(BF16)

    def chunk(b, t):
        return boff + b * nt + (t if direction == 0 else nt - 1 - t)

    zero_init = h0 is None
    in_specs = [pl.BlockSpec((q, SSD_CONV_CH), lambda b, t: (chunk(b, t), 0)),
                pl.BlockSpec((q, LANES), lambda b, t: (chunk(b, t), 0)),
                pl.BlockSpec((1, LANES), lambda b, t: (0, 0)),
                pl.BlockSpec((1, LANES), lambda b, t: (0, 0)),
                pl.BlockSpec((q, q), lambda b, t: (0, 0)),
                pl.BlockSpec((SSD_GROUPS, LANES, GROUP_W), lambda b, t: (0, 0, 0))]
    args = [xbc_act, dt_raw, dt_bias.reshape(1, LANES), a_log.reshape(1, LANES), tri, expand_mat]
    if not zero_init:
        in_specs.append(pl.BlockSpec((1, 1, SSD_HEADS, SSD_HEAD_DIM, SSD_STATE),
                                     lambda b, t: (b, direction, 0, 0, 0)))
        args.append(h0)
    aliases = {}
    if y_into is not None:
        in_specs.append(pl.BlockSpec(memory_space=pl.ANY))
        aliases = {len(args): 0}
        args.append(y_into)
    if fin_into is not None:
        in_specs.append(pl.BlockSpec(memory_space=pl.ANY))
        aliases[len(args)] = 1
        args.append(fin_into)
    return pl.pallas_call(
        functools.partial(_ssd_kernel, direction=direction, zero_init=zero_init, nt=nt),
        grid=(nb, nt),
        in_specs=in_specs,
        out_specs=[pl.BlockSpec((q, SSD_D_INNER), lambda b, t: (chunk(b, t), 0)),
                   pl.BlockSpec((1, None, SSD_HEADS, SSD_HEAD_DIM, SSD_STATE),
                                lambda b, t: (b, direction, 0, 0, 0))],
        out_shape=[jax.ShapeDtypeStruct((xbc_act.shape[0], SSD_D_INNER), BF16),
                   jax.ShapeDtypeStruct((nb, 2, SSD_HEADS, SSD_HEAD_DIM, SSD_STATE), F32)],
        scratch_shapes=[pltpu.VMEM((SSD_GROUPS, SSD_STATE, GROUP_W), F32)],
        input_output_aliases=aliases,
        compiler_params=_params("parallel", "arbitrary"),
        name=name,
    )(*args)


def _ssd_post_kernel(yf_ref, yb_ref, xs_ref, z_ref, d_ref, nw_ref, o_ref):
    z = z_ref[...].astype(F32)
    y = yf_ref[...].astype(F32) + yb_ref[...].astype(F32) + d_ref[...] * xs_ref[...].astype(F32)
    y = y * (z / (1.0 + jnp.exp(-z)))
    outs = []
    for g in range(SSD_GROUPS):
        yg = y[:, g * GROUP_W:(g + 1) * GROUP_W]
        ms = jnp.mean(yg * yg, axis=-1, keepdims=True)
        outs.append(yg * lax.rsqrt(ms + NORM_EPS) * nw_ref[:, g * GROUP_W:(g + 1) * GROUP_W])
    o_ref[...] = jnp.concatenate(outs, axis=1).astype(o_ref.dtype)


def _ssd_post_call(y_f, y_b, xbc_act, z, d_tot, norm_w, *, tm, name):
    m = y_f.shape[0]
    di = SSD_D_INNER
    row = pl.BlockSpec((tm, di), lambda i: (i, 0))
    vec = pl.BlockSpec((1, di), lambda i: (0, 0))
    return pl.pallas_call(
        _ssd_post_kernel,
        grid=(m // tm,),
        in_specs=[row, row, row, row, vec, vec],
        out_specs=row,
        out_shape=jax.ShapeDtypeStruct((m, di), BF16),
        compiler_params=_params("parallel"),
        name=name,
    )(y_f, y_b, xbc_act, z, d_tot, norm_w.reshape(1, di))


def _expert_changed(be_ref, i):
    return (i == 0) | (be_ref[i] != be_ref[jnp.maximum(i - 1, 0)])


def _moe_up_kernel(be_ref, nu_ref, x_ref, wg_ref, wu_ref, bg_ref, bu_ref, o_ref, wg_s, wu_s):
    i = pl.program_id(1)

    @pl.when(_expert_changed(be_ref, i))
    def _():
        wg_s[...] = wg_ref[...].astype(BF16)
        wu_s[...] = wu_ref[...].astype(BF16)

    @pl.when(i < nu_ref[0])
    def _():
        x = x_ref[...]
        g = jnp.dot(x, wg_s[...], preferred_element_type=F32) + bg_ref[...]
        u = jnp.dot(x, wu_s[...], preferred_element_type=F32) + bu_ref[...]
        g = jnp.minimum(g, SWIGLU_LIMIT)
        u = jnp.clip(u, -SWIGLU_LIMIT, SWIGLU_LIMIT)
        act = (u + 1.0) * (g / (1.0 + jnp.exp(-SWIGLU_ALPHA * g)))
        o_ref[...] = act.astype(o_ref.dtype)


def _moe_down_kernel(be_ref, nu_ref, a_ref, w_ref, b_ref, o_ref, w_s):
    i = pl.program_id(1)

    @pl.when(_expert_changed(be_ref, i))
    def _():
        w_s[...] = w_ref[...].astype(BF16)

    @pl.when(i < nu_ref[0])
    def _():
        y = jnp.dot(a_ref[...], w_s[...], preferred_element_type=F32) + b_ref[...]
        o_ref[...] = y.astype(o_ref.dtype)


def _moe_experts(x_sorted, block_e, n_used, layer, w_gate_up, b_gate_up, w_down, b_down, *, tn_up, tn_down,
                 name):
    slots, d = x_sorted.shape
    nblk = slots // MOE_BLOCK
    nj = D_FF // tn_up
    nl = w_gate_up.shape[0]
    b_gu = b_gate_up.reshape(nl, N_EXPERTS, 1, 2 * D_FF)
    act = pl.pallas_call(
        _moe_up_kernel,
        grid_spec=pltpu.PrefetchScalarGridSpec(
            num_scalar_prefetch=2,
            grid=(nj, nblk),
            in_specs=[pl.BlockSpec((MOE_BLOCK, d), lambda j, i, be, nu: (i, 0)),
                      pl.BlockSpec((None, None, d, tn_up), lambda j, i, be, nu: (layer, be[i], 0, j)),
                      pl.BlockSpec((None, None, d, tn_up), lambda j, i, be, nu: (layer, be[i], 0, nj + j)),
                      pl.BlockSpec((None, None, 1, tn_up), lambda j, i, be, nu: (layer, be[i], 0, j)),
                      pl.BlockSpec((None, None, 1, tn_up), lambda j, i, be, nu: (layer, be[i], 0, nj + j))],
            out_specs=pl.BlockSpec((MOE_BLOCK, tn_up), lambda j, i, be, nu: (i, j)),
            scratch_shapes=[pltpu.VMEM((d, tn_up), BF16), pltpu.VMEM((d, tn_up), BF16)]),
        out_shape=jax.ShapeDtypeStruct((slots, D_FF), BF16),
        compiler_params=_params("arbitrary", "arbitrary"),
        name=name + "_up",
    )(block_e, n_used, x_sorted, w_gate_up, w_gate_up, b_gu, b_gu)
    nj2 = d // tn_down
    return pl.pallas_call(
        _moe_down_kernel,
        grid_spec=pltpu.PrefetchScalarGridSpec(
            num_scalar_prefetch=2,
            grid=(nj2, nblk),
            in_specs=[pl.BlockSpec((MOE_BLOCK, D_FF), lambda j, i, be, nu: (i, 0)),
                      pl.BlockSpec((None, None, D_FF, tn_down), lambda j, i, be, nu: (layer, be[i], 0, j)),
                      pl.BlockSpec((None, None, 1, tn_down), lambda j, i, be, nu: (layer, be[i], 0, j))],
            out_specs=pl.BlockSpec((MOE_BLOCK, tn_down), lambda j, i, be, nu: (i, j)),
            scratch_shapes=[pltpu.VMEM((D_FF, tn_down), BF16)]),
        out_shape=jax.ShapeDtypeStruct((slots, d), BF16),
        compiler_params=_params("arbitrary", "arbitrary"),
        name=name + "_down",
    )(block_e, n_used, act, w_down, b_down.reshape(nl, N_EXPERTS, 1, d))


def _combine_kernel(*refs, split_tile):
    y_refs = refs[:TOP_K]
    gate_ref, x_ref, g2_ref = refs[TOP_K:TOP_K + 3]
    o_refs = refs[TOP_K + 3:]
    gates = gate_ref[...]
    f = None
    for k in range(TOP_K):
        term = gates[:, k:k + 1] * y_refs[k][...].astype(F32)
        f = term if f is None else f + term
    out = x_ref[...] + g2_ref[0] * f
    if split_tile is None:
        o_refs[0][...] = out
    else:
        i = pl.program_id(0)

        @pl.when(i < split_tile)
        def _():
            o_refs[0][...] = out

        @pl.when(i >= split_tile)
        def _():
            o_refs[1][...] = out


def _combine_call(y4, gate_pad, x, mod, n_p, ls, *, tm, split, name):
    n, d = x.shape
    st = n_p // tm
    if split:
        out_specs = [pl.BlockSpec((tm, d), lambda i, j: (jnp.minimum(i, st - 1), 0)),
                     pl.BlockSpec((tm, d), lambda i, j: (jnp.maximum(i - st, 0), 0))]
        out_shape = [jax.ShapeDtypeStruct((n_p, d), F32), jax.ShapeDtypeStruct((n - n_p, d), F32)]
    else:
        out_specs = pl.BlockSpec((tm, d), lambda i, j: (i, 0))
        out_shape = jax.ShapeDtypeStruct((n, d), F32)
    nt = n // tm
    y_specs = [pl.BlockSpec((tm, d), functools.partial(lambda i, j, k: (k * nt + i, 0), k=k))
               for k in range(TOP_K)]
    return pl.pallas_call(
        functools.partial(_combine_kernel, split_tile=st if split else None),
        grid=(nt, 1),
        in_specs=y_specs + [pl.BlockSpec((tm, LANES), lambda i, j: (i, 0)),
                            pl.BlockSpec((tm, d), lambda i, j: (i, 0)),
                            _mod_spec(5, tm, d, n_p, ls)],
        out_specs=out_specs,
        out_shape=out_shape,
        compiler_params=_params("arbitrary", "arbitrary"),
        name=name,
    )(*([y4] * TOP_K), gate_pad, x, mod)


def _moe_layer(x, mod, n_p, ls, layer, norm_w, w_router, b_router, w_gate_up, b_gate_up, w_down, b_down, *,
               split, name):
    n, d = x.shape
    h, idx_pad, gate_pad, rank_pad, cnt = _router_call(x, norm_w.reshape(1, d), mod, n_p, ls, w_router,
                                                       b_router, tm=256, name=name + "_router")
    n_assign = n * TOP_K
    flat_e = idx_pad[:, :TOP_K].reshape(-1)
    counts = cnt[0, :N_EXPERTS]
    padded = (counts + MOE_BLOCK - 1) // MOE_BLOCK * MOE_BLOCK
    pad_end = jnp.cumsum(padded)
    pad_start = pad_end - padded
    dest = pad_start[flat_e] + rank_pad[:, :TOP_K].reshape(-1)
    n_blocks = -(-n_assign // MOE_BLOCK) + N_EXPERTS
    slots = n_blocks * MOE_BLOCK
    slot_tok = (jnp.arange(slots, dtype=jnp.int32) % n).at[dest].set(
        jnp.arange(n_assign, dtype=jnp.int32) // TOP_K, unique_indices=True, mode="promise_in_bounds")
    block_e = jnp.minimum(jnp.searchsorted(pad_end, jnp.arange(n_blocks, dtype=jnp.int32) * MOE_BLOCK,
                                           side="right"), N_EXPERTS - 1).astype(jnp.int32)
    n_used = (pad_end[-1] // MOE_BLOCK).astype(jnp.int32).reshape(1)
    x_sorted = h.at[slot_tok].get(mode="promise_in_bounds")
    y_sorted = _moe_experts(x_sorted, block_e, n_used, layer, w_gate_up, b_gate_up, w_down, b_down,
                            tn_up=1024, tn_down=2048, name=name)
    dest_by_choice = dest.reshape(n, TOP_K).T.reshape(-1)
    y4 = y_sorted.at[dest_by_choice].get(mode="promise_in_bounds")
    return _combine_call(y4, gate_pad, x, mod, n_p, ls, tm=256, split=split, name=name + "_combine")


def _ada_call(cond, w_ada, b_ada, layer, *, name):
    g = cond.shape[0]
    a = jnp.pad(jax.nn.silu(cond), ((0, 16 - g), (0, 0))).astype(BF16)
    m = _matmul(a, w_ada, n_out=6 * D_MODEL, tm=16, tn=1024, out_dtype=F32, name=name, layer=layer,
                bias=b_ada.reshape(b_ada.shape[0], 1, 6 * D_MODEL))
    return m[:g].reshape(g * 6, 1, D_MODEL)


def kernel(x_prompt, x_sample, c, c_ctx, cache_k, cache_v, state_ssm, norm1_w, norm2_w, w_ada, b_ada, w_qkv, q_norm_w, k_norm_w, lambda_q1, lambda_k1, lambda_q2, lambda_k2, subln_w, w_o, w_in_ssd, conv_w, conv_b, dt_bias, a_log, d_skip, ssd_norm_w, w_out_ssd, w_router, b_router, w_gate_up, b_gate_up, w_down, b_down):
    bp, lp, d = x_prompt.shape
    bs, ls, _ = x_sample.shape
    past = cache_k.shape[2]
    n_p, n_s = bp * lp, bs * ls
    x = jnp.concatenate([x_prompt.reshape(n_p, d), x_sample.reshape(n_s, d)], axis=0)
    cond = jnp.concatenate([c_ctx[None], c], axis=0)
    resid_of = lambda xx, mod, which: (xx, mod, which, n_p, ls)

    n = n_p + n_s
    mod = _ada_call(cond, w_ada, b_ada, 0, name="ada0")
    h = _modnorm_call(x, norm1_w[0].reshape(1, d), mod, 0, n_p, ls, tm=256, name="norm1_0")
    wq = w_qkv[0].astype(BF16)
    qk_w = jnp.stack([q_norm_w[0], k_norm_w[0]], axis=0)
    lam_init = 0.8 - 0.6 * math.exp(-0.3 * 0)
    lam_params = jnp.stack([lambda_q1[0], lambda_k1[0], lambda_q2[0], lambda_k2[0]], axis=0)
    qp, kp, vp = _qkv_call(h, wq, qk_w, None, row_off=0, m=n_p, kv_dtype=F32, tm=512, tn=512, ls=ls,
                           name="qkv_prompt")
    qs, ks, vs = _qkv_call(h, wq, qk_w, _rope_tables(ls), row_off=n_p, m=n_s, kv_dtype=BF16, tm=512, tn=512,
                           ls=ls, name="qkv_sample")
    o = _attn_call(qp, [(kp, vp, lp)], lam_params, subln_w[0], None, n_rows=n, row_off=0, nb=bp, lq=lp,
                   tq=lp, lam_init=lam_init, name="attn_prompt")
    ck = cache_k[:, 0].reshape(bs * past, d)
    cv = cache_v[:, 0].reshape(bs * past, d)
    o = _attn_call(qs, [(ck, cv, past), (ks, vs, ls)], lam_params, subln_w[0], o, n_rows=n, row_off=n_p,
                   nb=bs, lq=ls, tq=256, lam_init=lam_init, name="attn_sample")
    x = _matmul(o, w_o[0].astype(BF16), n_out=d, tm=1024, tn=512, out_dtype=F32, name="attn_out",
                resid=resid_of(x, mod, 2))
    x = _moe_layer(x, mod, n_p, ls, 0, norm2_w[0], w_router[0], b_router[0], w_gate_up, b_gate_up,
                   w_down, b_down, split=False, name="moe0")
    new_k = kp.reshape(bp, 1, lp, ATTN_HEADS, 2, QK_DIM)
    new_v = vp.reshape(bp, 1, lp, ATTN_HEADS, V_DIM)

    mod = _ada_call(cond, w_ada, b_ada, 1, name="ada1")
    h = _modnorm_call(x, norm1_w[1].reshape(1, d), mod, 0, n_p, ls, tm=256, name="norm1_1")
    w_in = w_in_ssd[0].astype(BF16)
    z = _matmul(h, w_in, n_out=SSD_D_INNER, tm=1024, tn=512, out_dtype=BF16, name="ssd_in_z")
    xbc = _matmul(h, w_in, n_out=SSD_CONV_CH, col_off=SSD_D_INNER, tm=1024, tn=512, out_dtype=BF16,
                  name="ssd_in_xbc")
    dt_raw = _matmul(h, w_in, n_out=2 * SSD_HEADS, col_off=SSD_D_INNER + SSD_CONV_CH, tm=1024, tn=LANES,
                     out_dtype=F32, name="ssd_in_dt")
    xbc_act = _conv_call(xbc, conv_w[0], conv_b[0], n_p=n_p, lp=lp, ls=ls, tc=256, tcn=2048, name="ssd_conv")
    ys, fin = [], None
    for direction in (0, 1):
        y, fin = _ssd_scan_call(xbc_act, dt_raw, dt_bias[0], a_log[0], None, None, fin, nb=bp, seq=lp,
                                row_off=0, direction=direction, name="ssd_scan_prompt%d" % direction)
        y, _ = _ssd_scan_call(xbc_act, dt_raw, dt_bias[0], a_log[0], state_ssm[:, 0], y, None, nb=bs, seq=ls,
                              row_off=n_p, direction=direction, name="ssd_scan_sample%d" % direction)
        ys.append(y)
    d_tot = jnp.repeat(d_skip[0, 0] + d_skip[0, 1], SSD_HEAD_DIM).reshape(1, SSD_D_INNER)
    yn = _ssd_post_call(ys[0], ys[1], xbc_act, z, d_tot, ssd_norm_w[0], tm=256, name="ssd_post")
    x = _matmul(yn, w_out_ssd[0].astype(BF16), n_out=d, tm=1024, tn=512, out_dtype=F32, name="ssd_out",
                resid=resid_of(x, mod, 2))
    xp, xs = _moe_layer(x, mod, n_p, ls, 1, norm2_w[1], w_router[1], b_router[1], w_gate_up, b_gate_up,
                        w_down, b_down, split=True, name="moe1")
    new_s = fin[:, None]
    return (xp.reshape(bp, lp, d), xs.reshape(bs, ls, d), new_k, new_v, new_s)
```

```python
import functools
import math

import jax
import jax.numpy as jnp
from jax import lax
from jax.experimental import pallas as pl
from jax.experimental.pallas import tpu as pltpu

F32 = jnp.float32
BF16 = jnp.bfloat16

D_MODEL = 2048
NORM_EPS = 1e-6
GRID_W = 64
ROPE_THETA = 10000.0
ATTN_HEADS = 8
QK_DIM = 128
V_DIM = 256
SSD_D_INNER = 4096
SSD_HEAD_DIM = 64
SSD_HEADS = 64
SSD_GROUPS = 8
SSD_STATE = 128
SSD_CONV = 5
SSD_CHUNK = 128
SSD_BC_DIM = SSD_GROUPS * SSD_STATE
SSD_CONV_CH = SSD_D_INNER + 2 * SSD_BC_DIM
N_EXPERTS = 32
TOP_K = 4
D_FF = 2048
SWIGLU_LIMIT = 7.0
SWIGLU_ALPHA = 1.702
MOE_BLOCK = 256
LANES = 128
GROUP_W = SSD_D_INNER // SSD_GROUPS
assert SSD_CHUNK == SSD_STATE == LANES and 2 * SSD_HEAD_DIM == LANES
VMEM_LIMIT = 56 * 1024 * 1024


def _params(*sem):
    return pltpu.CompilerParams(dimension_semantics=sem, vmem_limit_bytes=VMEM_LIMIT)


def _group_of_tile(i, tm, n_p, ls):
    return jnp.where(i * tm < n_p, 0, 1 + (i * tm - n_p) // ls)


def _mod_spec(which, tm, tn, n_p, ls):
    return pl.BlockSpec((1, 1, tn), lambda i, j: (_group_of_tile(i, tm, n_p, ls) * 6 + which, 0, j))


def _mm_kernel(a_ref, w_ref, *rest, epilogue):
    acc = jnp.dot(a_ref[...].astype(BF16), w_ref[...].astype(BF16), preferred_element_type=F32)
    if epilogue == "bias":
        b_ref, o_ref = rest
        o_ref[...] = (acc + b_ref[...]).astype(o_ref.dtype)
    elif epilogue == "resid":
        x_ref, g_ref, o_ref = rest
        o_ref[...] = x_ref[...] + g_ref[0] * acc
    else:
        (o_ref,) = rest
        o_ref[...] = acc.astype(o_ref.dtype)


def _matmul(a, w, *, n_out, col_off=0, tm, tn, out_dtype, name, layer=None, bias=None, resid=None):
    m, k = a.shape
    joff = col_off // tn
    if layer is None:
        w_spec = pl.BlockSpec((k, tn), lambda i, j: (0, j + joff))
        b_spec = pl.BlockSpec((1, tn), lambda i, j: (0, j))
    else:
        w_spec = pl.BlockSpec((None, k, tn), lambda i, j: (layer, 0, j + joff))
        b_spec = pl.BlockSpec((None, 1, tn), lambda i, j: (layer, 0, j))
    in_specs = [pl.BlockSpec((tm, k), lambda i, j: (i, 0)), w_spec]
    args = [a, w]
    if bias is not None:
        epilogue = "bias"
        in_specs.append(b_spec)
        args.append(bias)
    elif resid is not None:
        epilogue = "resid"
        x, mod, which, n_p, ls = resid
        in_specs += [pl.BlockSpec((tm, tn), lambda i, j: (i, j)), _mod_spec(which, tm, tn, n_p, ls)]
        args += [x, mod]
    else:
        epilogue = "plain"
    return pl.pallas_call(
        functools.partial(_mm_kernel, epilogue=epilogue),
        grid=(m // tm, n_out // tn),
        in_specs=in_specs,
        out_specs=pl.BlockSpec((tm, tn), lambda i, j: (i, j)),
        out_shape=jax.ShapeDtypeStruct((m, n_out), out_dtype),
        compiler_params=_params("parallel", "arbitrary"),
        name=name,
    )(*args)


def _modnorm(x, w, shift, scale):
    ms = jnp.mean(x * x, axis=-1, keepdims=True)
    return (x * lax.rsqrt(ms + NORM_EPS) * w) * (1.0 + scale) + shift


def _modnorm_kernel(x_ref, w_ref, sh_ref, sc_ref, o_ref):
    o_ref[...] = _modnorm(x_ref[...], w_ref[...], sh_ref[0], sc_ref[0]).astype(o_ref.dtype)


def _modnorm_call(x, w, mod, which_shift, n_p, ls, *, tm, name):
    m, d = x.shape
    return pl.pallas_call(
        _modnorm_kernel,
        grid=(m // tm, 1),
        in_specs=[pl.BlockSpec((tm, d), lambda i, j: (i, 0)),
                  pl.BlockSpec((1, d), lambda i, j: (0, 0)),
                  _mod_spec(which_shift, tm, d, n_p, ls),
                  _mod_spec(which_shift + 1, tm, d, n_p, ls)],
        out_specs=pl.BlockSpec((tm, d), lambda i, j: (i, 0)),
        out_shape=jax.ShapeDtypeStruct((m, d), BF16),
        compiler_params=_params("parallel", "arbitrary"),
        name=name,
    )(x, w, mod, mod)


def _split2(v):
    hi = v.astype(BF16)
    lo = (v - hi.astype(F32)).astype(BF16)
    return hi, lo


def _router_kernel(x_ref, w_ref, sh_ref, sc_ref, wr_ref, br_ref, h_ref, idx_ref, gate_ref, rank_ref, cnt_ref,
                   run_ref):
    @pl.when(pl.program_id(0) == 0)
    def _():
        run_ref[...] = jnp.zeros_like(run_ref)

    h = _modnorm(x_ref[...], w_ref[...], sh_ref[0], sc_ref[0])
    h_ref[...] = h.astype(BF16)
    h_hi, h_lo = _split2(h)
    w_hi, w_lo = _split2(wr_ref[...])
    logits = (jnp.dot(h_hi, w_hi, preferred_element_type=F32)
              + jnp.dot(h_hi, w_lo, preferred_element_type=F32)
              + jnp.dot(h_lo, w_hi, preferred_element_type=F32)) + br_ref[...]
    lane = lax.broadcasted_iota(jnp.int32, logits.shape, 1).astype(F32)
    cur = jnp.where(lane < N_EXPERTS, logits, -jnp.inf)
    vals, idxs = [], []
    for _ in range(TOP_K):
        m = jnp.max(cur, axis=-1, keepdims=True)
        am = jnp.min(jnp.where(cur == m, lane, float(LANES)), axis=-1, keepdims=True)
        vals.append(m)
        idxs.append(am)
        cur = jnp.where(lane == am, -jnp.inf, cur)
    exps = [jnp.exp(v - vals[0]) for v in vals]
    denom = exps[0] + exps[1] + exps[2] + exps[3]
    idx_out = jnp.zeros(logits.shape, F32)
    gate_out = jnp.zeros(logits.shape, F32)
    for k in range(TOP_K):
        idx_out = jnp.where(lane == k, idxs[k], idx_out)
        gate_out = jnp.where(lane == k, exps[k] / denom, gate_out)
    idx_ref[...] = idx_out.astype(jnp.int32)
    gate_ref[...] = gate_out
    tm = logits.shape[0]
    chosen = functools.reduce(jnp.logical_or, [lane == am for am in idxs])
    onehot = jnp.where(chosen, 1.0, 0.0)
    ti = lax.broadcasted_iota(jnp.int32, (tm, tm), 0)
    tj = lax.broadcasted_iota(jnp.int32, (tm, tm), 1)
    earlier = jnp.where(ti > tj, 1.0, 0.0).astype(BF16)
    base = run_ref[...] + jnp.dot(earlier, onehot.astype(BF16), preferred_element_type=F32)
    rank_out = jnp.zeros(logits.shape, F32)
    for k in range(TOP_K):
        rk = jnp.sum(jnp.where(lane == idxs[k], base, 0.0), axis=-1, keepdims=True)
        rank_out = jnp.where(lane == k, rk, rank_out)
    rank_ref[...] = rank_out.astype(jnp.int32)
    run_ref[...] = run_ref[...] + jnp.sum(onehot, axis=0, keepdims=True)
    cnt_ref[...] = run_ref[...].astype(jnp.int32)


def _router_call(x, w, mod, n_p, ls, w_router, b_router, *, tm, name):
    m, d = x.shape
    wr = jnp.pad(w_router, ((0, 0), (0, LANES - N_EXPERTS)))
    br = jnp.pad(b_router, (0, LANES - N_EXPERTS)).reshape(1, LANES)
    return pl.pallas_call(
        _router_kernel,
        grid=(m // tm, 1),
        in_specs=[pl.BlockSpec((tm, d), lambda i, j: (i, 0)),
                  pl.BlockSpec((1, d), lambda i, j: (0, 0)),
                  _mod_spec(3, tm, d, n_p, ls),
                  _mod_spec(4, tm, d, n_p, ls),
                  pl.BlockSpec((d, LANES), lambda i, j: (0, 0)),
                  pl.BlockSpec((1, LANES), lambda i, j: (0, 0))],
        out_specs=[pl.BlockSpec((tm, d), lambda i, j: (i, 0)),
                   pl.BlockSpec((tm, LANES), lambda i, j: (i, 0)),
                   pl.BlockSpec((tm, LANES), lambda i, j: (i, 0)),
                   pl.BlockSpec((tm, LANES), lambda i, j: (i, 0)),
                   pl.BlockSpec((1, LANES), lambda i, j: (0, 0))],
        out_shape=[jax.ShapeDtypeStruct((m, d), BF16),
                   jax.ShapeDtypeStruct((m, LANES), jnp.int32),
                   jax.ShapeDtypeStruct((m, LANES), F32),
                   jax.ShapeDtypeStruct((m, LANES), jnp.int32),
                   jax.ShapeDtypeStruct((1, LANES), jnp.int32)],
        scratch_shapes=[pltpu.VMEM((1, LANES), F32)],
        compiler_params=_params("arbitrary", "arbitrary"),
        name=name,
    )(x, w, mod, mod, wr, br)


def _qkv_kernel(a_ref, w_ref, nw_ref, *rest, rope, tn):
    if rope:
        cos_ref, sin_ref, q_ref, k_ref, v_ref = rest
    else:
        q_ref, k_ref, v_ref = rest
    j = pl.program_id(1)
    nq = D_MODEL // tn
    acc = jnp.dot(a_ref[...], w_ref[...], preferred_element_type=F32)

    def normed(widx, out_scale):
        w = nw_ref[widx:widx + 1, :]
        outs = []
        for c in range(tn // QK_DIM):
            xc = acc[:, c * QK_DIM:(c + 1) * QK_DIM]
            ms = jnp.mean(xc * xc, axis=-1, keepdims=True)
            y = xc * lax.rsqrt(ms + NORM_EPS) * w
            if rope:
                lane = lax.broadcasted_iota(jnp.int32, y.shape, 1)
                partner = jnp.where(lane % 64 < 32, pltpu.roll(y, QK_DIM - 32, 1), pltpu.roll(y, 32, 1))
                y = y * cos_ref[...] + partner * sin_ref[...]
            outs.append(y * out_scale if out_scale != 1.0 else y)
        return jnp.concatenate(outs, axis=1)

    @pl.when(j < nq)
    def _():
        q_ref[...] = normed(0, QK_DIM ** -0.5).astype(q_ref.dtype)

    @pl.when((j >= nq) & (j < 2 * nq))
    def _():
        k_ref[...] = normed(1, 1.0).astype(k_ref.dtype)

    @pl.when(j >= 2 * nq)
    def _():
        v_ref[...] = acc.astype(v_ref.dtype)


def _qkv_call(h, w_qkv, qk_norm_w, rope_tabs, *, row_off, m, kv_dtype, tm, tn, ls, name):
    d = h.shape[1]
    nq = d // tn
    ioff = row_off // tm
    rope = rope_tabs is not None
    in_specs = [pl.BlockSpec((tm, d), lambda i, j: (i + ioff, 0)),
                pl.BlockSpec((d, tn), lambda i, j: (0, j)),
                pl.BlockSpec((2, QK_DIM), lambda i, j: (0, 0))]
    args = [h, w_qkv, qk_norm_w]
    if rope:
        nt = ls // tm
        in_specs += [pl.BlockSpec((tm, QK_DIM), lambda i, j: (i % nt, 0))] * 2
        args += list(rope_tabs)
    return pl.pallas_call(
        functools.partial(_qkv_kernel, rope=rope, tn=tn),
        grid=(m // tm, 3 * nq),
        in_specs=in_specs,
        out_specs=[pl.BlockSpec((tm, tn), lambda i, j: (i, jnp.minimum(j, nq - 1))),
                   pl.BlockSpec((tm, tn), lambda i, j: (i, jnp.clip(j - nq, 0, nq - 1))),
                   pl.BlockSpec((tm, tn), lambda i, j: (i, jnp.clip(j - 2 * nq, 0, nq - 1)))],
        out_shape=[jax.ShapeDtypeStruct((m, d), BF16),
                   jax.ShapeDtypeStruct((m, d), kv_dtype),
                   jax.ShapeDtypeStruct((m, d), kv_dtype)],
        compiler_params=_params("parallel", "arbitrary"),
        name=name,
    )(*args)


def _rope_tables(ls):
    rows = ls // GRID_W
    row = jnp.repeat(jnp.arange(rows, dtype=F32), GRID_W)
    col = jnp.tile(jnp.arange(GRID_W, dtype=F32), rows)
    axis_dim = QK_DIM // 2
    inv = ROPE_THETA ** (-jnp.arange(0, axis_dim, 2, dtype=F32) / axis_dim)
    ar, ac = row[:, None] * inv, col[:, None] * inv
    cos = jnp.concatenate([jnp.cos(ar), jnp.cos(ar), jnp.cos(ac), jnp.cos(ac)], axis=1)
    sin = jnp.concatenate([-jnp.sin(ar), jnp.sin(ar), -jnp.sin(ac), jnp.sin(ac)], axis=1)
    return cos, sin


def _attn_kernel(lam_ref, subw_ref, q_ref, *rest, nseg, lam_init):
    k_refs, v_refs, o_ref = rest[:nseg], rest[nseg:2 * nseg], rest[-1]
    lp = lam_ref[...]
    lam = (jnp.exp(jnp.sum(lp[0:1] * lp[1:2], axis=-1, keepdims=True))
           - jnp.exp(jnp.sum(lp[2:3] * lp[3:4], axis=-1, keepdims=True)) + lam_init)
    q = q_ref[...]
    ks = [r[...].astype(BF16) for r in k_refs]
    vs = [r[...].astype(BF16) for r in v_refs]
    outs = []
    for mi in range(2):
        qm = q[:, mi * QK_DIM:(mi + 1) * QK_DIM]
        ss = [lax.dot_general(qm, k[:, mi * QK_DIM:(mi + 1) * QK_DIM], (((1,), (1,)), ((), ())),
                              preferred_element_type=F32) for k in ks]
        mx = functools.reduce(jnp.maximum, [jnp.max(s, axis=-1, keepdims=True) for s in ss])
        ps = [jnp.exp(s - mx) for s in ss]
        den = functools.reduce(jnp.add, [jnp.sum(p, axis=-1, keepdims=True) for p in ps])
        pv = functools.reduce(jnp.add, [jnp.dot(p.astype(BF16), v, preferred_element_type=F32)
                                        for p, v in zip(ps, vs)])
        outs.append(pv * (1.0 / den))
    acc = outs[0] - lam * outs[1]
    ms = jnp.mean(acc * acc, axis=-1, keepdims=True)
    o = acc * lax.rsqrt(ms + NORM_EPS) * subw_ref[...] * (1.0 - lam_init)
    o_ref[...] = o.astype(o_ref.dtype)


def _attn_call(q, kv_segs, lam_params, subln_w, o_into, *, n_rows, row_off, nb, lq, tq, lam_init, name):
    nqb = lq // tq
    nseg = len(kv_segs)
    ooff = row_off // tq
    in_specs = [pl.BlockSpec((4, QK_DIM), lambda b, h, i: (0, 0)),
                pl.BlockSpec((1, V_DIM), lambda b, h, i: (0, 0)),
                pl.BlockSpec((tq, V_DIM), lambda b, h, i: (b * nqb + i, h))]
    in_specs += [pl.BlockSpec((lk, V_DIM), lambda b, h, i: (b, h)) for (_, _, lk) in kv_segs] * 2
    args = [lam_params, subln_w.reshape(1, V_DIM), q]
    args += [k for (k, _, _) in kv_segs] + [v for (_, v, _) in kv_segs]
    aliases = {}
    if o_into is not None:
        in_specs.append(pl.BlockSpec(memory_space=pl.ANY))
        aliases = {len(args): 0}
        args.append(o_into)
    return pl.pallas_call(
        functools.partial(_attn_kernel, nseg=nseg, lam_init=lam_init),
        grid=(nb, ATTN_HEADS, nqb),
        in_specs=in_specs,
        out_specs=pl.BlockSpec((tq, V_DIM), lambda b, h, i: (ooff + b * nqb + i, h)),
        out_shape=jax.ShapeDtypeStruct((n_rows, q.shape[1]), BF16),
        input_output_aliases=aliases,
        compiler_params=_params("parallel", "parallel", "arbitrary"),
        name=name,
    )(*args)


def _conv_kernel(xp_ref, x_ref, xn_ref, w_ref, b_ref, o_ref, ext_ref, *, tc, n_p, lp, ls, halo):
    i = pl.program_id(0)
    row0 = i * tc
    in_prompt = row0 < n_p
    first = jnp.where(in_prompt, row0 % lp == 0, (row0 - n_p) % ls == 0)
    last = jnp.where(in_prompt, (row0 + tc) % lp == 0, (row0 + tc - n_p) % ls == 0)
    prev = xp_ref[...].astype(F32)[halo - 8:halo, :]
    nxt = xn_ref[...].astype(F32)[0:8, :]
    ext_ref[0:8, :] = jnp.where(first, 0.0, prev)
    ext_ref[8:8 + tc, :] = x_ref[...].astype(F32)
    ext_ref[8 + tc:16 + tc, :] = jnp.where(last, 0.0, nxt)
    acc = b_ref[...] + w_ref[0:1, :] * ext_ref[6:6 + tc, :]
    for k in range(1, SSD_CONV):
        acc = acc + w_ref[k:k + 1, :] * ext_ref[6 + k:6 + k + tc, :]
    o_ref[...] = (acc / (1.0 + jnp.exp(-acc))).astype(o_ref.dtype)


def _conv_call(xbc, conv_w, conv_b, *, n_p, lp, ls, tc, tcn, name):
    m, ch = xbc.shape
    halo = 16
    hb = tc // halo
    nhb = m // halo
    return pl.pallas_call(
        functools.partial(_conv_kernel, tc=tc, n_p=n_p, lp=lp, ls=ls, halo=halo),
        grid=(m // tc, ch // tcn),
        in_specs=[pl.BlockSpec((halo, tcn), lambda i, j: (jnp.maximum(i * hb - 1, 0), j)),
                  pl.BlockSpec((tc, tcn), lambda i, j: (i, j)),
                  pl.BlockSpec((halo, tcn), lambda i, j: (jnp.minimum((i + 1) * hb, nhb - 1), j)),
                  pl.BlockSpec((SSD_CONV, tcn), lambda i, j: (0, j)),
                  pl.BlockSpec((1, tcn), lambda i, j: (0, j))],
        out_specs=pl.BlockSpec((tc, tcn), lambda i, j: (i, j)),
        out_shape=jax.ShapeDtypeStruct((m, ch), BF16),
        scratch_shapes=[pltpu.VMEM((tc + 16, tcn), F32)],
        compiler_params=_params("parallel", "arbitrary"),
        name=name,
    )(xbc, xbc, xbc, conv_w, conv_b.reshape(1, ch))


def _ssd_kernel(xbc_ref, dtraw_ref, dtb_ref, alog_ref, tri_ref, *rest, direction, zero_init, nt):
    y_ref, fin_ref, st_ref = rest[-3:]
    if not zero_init:
        h0_ref = rest[0]
    t = pl.program_id(1)
    q = SSD_CHUNK
    hpg = SSD_HEADS // SSD_GROUPS

    @pl.when(t == 0)
    def _():
        for g in range(SSD_GROUPS):
            if zero_init:
                st_ref[g] = jnp.zeros((SSD_STATE, GROUP_W), F32)
            else:
                hg = h0_ref[0, 0, g * hpg:(g + 1) * hpg].reshape(GROUP_W, SSD_STATE)
                st_ref[g] = hg.T

    x = dtraw_ref[...] + dtb_ref[...]
    dt = jnp.maximum(x, 0.0) + jnp.log1p(jnp.exp(-jnp.abs(x)))
    a = dt * (-jnp.exp(alog_ref[...]))
    tri = tri_ref[...]
    a1 = a.astype(BF16)
    r1 = a - a1.astype(F32)
    a2 = r1.astype(BF16)
    a3 = (r1 - a2.astype(F32)).astype(BF16)
    cs = (jnp.dot(tri, a1, preferred_element_type=F32) + jnp.dot(tri, a2, preferred_element_type=F32)
          + jnp.dot(tri, a3, preferred_element_type=F32))
    tot = cs[q - 1:q, :] if direction == 0 else cs[0:1, :]
    cdec = jnp.exp(tot)
    cs_t = cs.T
    dt_t = dt.T
    wend_t = (dt * jnp.exp(tot - cs)).T
    li = lax.broadcasted_iota(jnp.int32, (q, q), 0)
    si = lax.broadcasted_iota(jnp.int32, (q, q), 1)
    mask = (li >= si) if direction == 0 else (li <= si)
    low = lax.broadcasted_iota(jnp.int32, (q, LANES), 1) < SSD_HEAD_DIM
    low_row = low[0:1, :]

    for g in range(SSD_GROUPS):
        bg = xbc_ref[:, SSD_D_INNER + g * SSD_STATE:SSD_D_INNER + (g + 1) * SSD_STATE]
        cg = xbc_ref[:, SSD_D_INNER + SSD_BC_DIM + g * SSD_STATE:SSD_D_INNER + SSD_BC_DIM + (g + 1) * SSD_STATE]
        cg32 = cg.astype(F32)
        bg_t = bg.astype(F32).T
        cb = lax.dot_general(cg, bg, (((1,), (1,)), ((), ())), preferred_element_type=F32)
        for p in range(hpg // 2):
            col0 = g * GROUP_W + p * LANES
            xp = xbc_ref[:, col0:col0 + LANES]
            zero = jnp.zeros_like(xp)
            x_lo = jnp.where(low, xp, zero)
            x_hi = jnp.where(low, zero, xp)
            s_in = st_ref[g, :, p * LANES:(p + 1) * LANES]
            s_bf = s_in.astype(BF16)
            s_lo = jnp.where(low, s_bf, zero)
            s_hi = jnp.where(low, zero, s_bf)
            c0 = direction * SSD_HEADS + g * hpg + 2 * p
            diag, offd, bw = [], [], []
            for c in (c0, c0 + 1):
                col = jnp.broadcast_to(cs[:, c:c + 1], (q, q))
                seg = col - cs_t[c:c + 1, :]
                diag.append((cb * jnp.exp(jnp.where(mask, seg, -jnp.inf)) * dt_t[c:c + 1, :]).astype(BF16))
                offd.append((cg32 * jnp.exp(col)).astype(BF16))
                bw.append((bg_t * wend_t[c:c + 1, :]).astype(BF16))
            lhs = jnp.concatenate(diag + offd, axis=1)
            rhs = jnp.concatenate([x_lo, x_hi, s_lo, s_hi], axis=0)
            y_ref[:, col0:col0 + LANES] = jnp.dot(lhs, rhs, preferred_element_type=F32).astype(y_ref.dtype)
            new = jnp.dot(jnp.concatenate(bw, axis=1), jnp.concatenate([x_lo, x_hi], axis=0),
                          preferred_element_type=F32)
            keep = jnp.where(low_row, cdec[:, c0:c0 + 1], cdec[:, c0 + 1:c0 + 2])
            st_ref[g, :, p * LANES:(p + 1) * LANES] = s_in * keep + new

    @pl.when(t == nt - 1)
    def _():
        for g in range(SSD_GROUPS):
            fin_ref[0, g * hpg:(g + 1) * hpg] = st_ref[g].T.reshape(hpg, SSD_HEAD_DIM, SSD_STATE)


def _ssd_scan_call(xbc_act, dt_raw, dt_bias, a_log, h0, y_into, fin_into, *, nb, seq, row_off, direction,
                   name):
    q = SSD_CHUNK
    nt = seq // q
    boff = row_off // q
    li = jnp.arange(q)[:, None]
    si = jnp.arange(q)[None, :]
    tri = ((li >= si) if direction == 0 else (li <= si)).astype(BF16)

    def chunk(b, t):
        return boff + b * nt + (t if direction == 0 else nt - 1 - t)

    zero_init = h0 is None
    in_specs = [pl.BlockSpec((q, SSD_CONV_CH), lambda b, t: (chunk(b, t), 0)),
                pl.BlockSpec((q, LANES), lambda b, t: (chunk(b, t), 0)),
                pl.BlockSpec((1, LANES), lambda b, t: (0, 0)),
                pl.BlockSpec((1, LANES), lambda b, t: (0, 0)),
                pl.BlockSpec((q, q), lambda b, t: (0, 0))]
    args = [xbc_act, dt_raw, dt_bias.reshape(1, LANES), a_log.reshape(1, LANES), tri]
    if not zero_init:
        in_specs.append(pl.BlockSpec((1, 1, SSD_HEADS, SSD_HEAD_DIM, SSD_STATE),
                                     lambda b, t: (b, direction, 0, 0, 0)))
        args.append(h0)
    aliases = {}
    if y_into is not None:
        in_specs.append(pl.BlockSpec(memory_space=pl.ANY))
        aliases = {len(args): 0}
        args.append(y_into)
    if fin_into is not None:
        in_specs.append(pl.BlockSpec(memory_space=pl.ANY))
        aliases[len(args)] = 1
        args.append(fin_into)
    return pl.pallas_call(
        functools.partial(_ssd_kernel, direction=direction, zero_init=zero_init, nt=nt),
        grid=(nb, nt),
        in_specs=in_specs,
        out_specs=[pl.BlockSpec((q, SSD_D_INNER), lambda b, t: (chunk(b, t), 0)),
                   pl.BlockSpec((1, None, SSD_HEADS, SSD_HEAD_DIM, SSD_STATE),
                                lambda b, t: (b, direction, 0, 0, 0))],
        out_shape=[jax.ShapeDtypeStruct((xbc_act.shape[0], SSD_D_INNER), BF16),
                   jax.ShapeDtypeStruct((nb, 2, SSD_HEADS, SSD_HEAD_DIM, SSD_STATE), F32)],
        scratch_shapes=[pltpu.VMEM((SSD_GROUPS, SSD_STATE, GROUP_W), F32)],
        input_output_aliases=aliases,
        compiler_params=_params("parallel", "arbitrary"),
        name=name,
    )(*args)


def _ssd_post_kernel(yf_ref, yb_ref, xs_ref, z_ref, d_ref, nw_ref, o_ref):
    z = z_ref[...].astype(F32)
    y = yf_ref[...].astype(F32) + yb_ref[...].astype(F32) + d_ref[...] * xs_ref[...].astype(F32)
    y = y * (z / (1.0 + jnp.exp(-z)))
    outs = []
    for g in range(SSD_GROUPS):
        yg = y[:, g * GROUP_W:(g + 1) * GROUP_W]
        ms = jnp.mean(yg * yg, axis=-1, keepdims=True)
        outs.append(yg * lax.rsqrt(ms + NORM_EPS) * nw_ref[:, g * GROUP_W:(g + 1) * GROUP_W])
    o_ref[...] = jnp.concatenate(outs, axis=1).astype(o_ref.dtype)


def _ssd_post_call(y_f, y_b, xbc_act, z, d_tot, norm_w, *, tm, name):
    m = y_f.shape[0]
    di = SSD_D_INNER
    row = pl.BlockSpec((tm, di), lambda i: (i, 0))
    vec = pl.BlockSpec((1, di), lambda i: (0, 0))
    return pl.pallas_call(
        _ssd_post_kernel,
        grid=(m // tm,),
        in_specs=[row, row, row, row, vec, vec],
        out_specs=row,
        out_shape=jax.ShapeDtypeStruct((m, di), BF16),
        compiler_params=_params("parallel"),
        name=name,
    )(y_f, y_b, xbc_act, z, d_tot, norm_w.reshape(1, di))


def _expert_changed(be_ref, i):
    return (i == 0) | (be_ref[i] != be_ref[jnp.maximum(i - 1, 0)])


def _moe_up_kernel(be_ref, nu_ref, x_ref, wg_ref, wu_ref, bg_ref, bu_ref, o_ref, wg_s, wu_s):
    i = pl.program_id(1)

    @pl.when(_expert_changed(be_ref, i))
    def _():
        wg_s[...] = wg_ref[...].astype(BF16)
        wu_s[...] = wu_ref[...].astype(BF16)

    @pl.when(i < nu_ref[0])
    def _():
        x = x_ref[...]
        g = jnp.dot(x, wg_s[...], preferred_element_type=F32) + bg_ref[...]
        u = jnp.dot(x, wu_s[...], preferred_element_type=F32) + bu_ref[...]
        g = jnp.minimum(g, SWIGLU_LIMIT)
        u = jnp.clip(u, -SWIGLU_LIMIT, SWIGLU_LIMIT)
        act = (u + 1.0) * (g / (1.0 + jnp.exp(-SWIGLU_ALPHA * g)))
        o_ref[...] = act.astype(o_ref.dtype)

    @pl.when(i >= nu_ref[0])
    def _():
        o_ref[...] = jnp.zeros_like(o_ref)


def _moe_down_kernel(be_ref, nu_ref, a_ref, w_ref, b_ref, o_ref, w_s):
    i = pl.program_id(1)

    @pl.when(_expert_changed(be_ref, i))
    def _():
        w_s[...] = w_ref[...].astype(BF16)

    @pl.when(i < nu_ref[0])
    def _():
        y = jnp.dot(a_ref[...], w_s[...], preferred_element_type=F32) + b_ref[...]
        o_ref[...] = y.astype(o_ref.dtype)

    @pl.when(i >= nu_ref[0])
    def _():
        o_ref[...] = jnp.zeros_like(o_ref)


def _moe_experts(x_sorted, block_e, n_used, layer, w_gate_up, b_gate_up, w_down, b_down, *, tn_up, tn_down,
                 name):
    slots, d = x_sorted.shape
    nblk = slots // MOE_BLOCK
    nj = D_FF // tn_up
    nl = w_gate_up.shape[0]
    b_gu = b_gate_up.reshape(nl, N_EXPERTS, 1, 2 * D_FF)
    act = pl.pallas_call(
        _moe_up_kernel,
        grid_spec=pltpu.PrefetchScalarGridSpec(
            num_scalar_prefetch=2,
            grid=(nj, nblk),
            in_specs=[pl.BlockSpec((MOE_BLOCK, d), lambda j, i, be, nu: (i, 0)),
                      pl.BlockSpec((None, None, d, tn_up), lambda j, i, be, nu: (layer, be[i], 0, j)),
                      pl.BlockSpec((None, None, d, tn_up), lambda j, i, be, nu: (layer, be[i], 0, nj + j)),
                      pl.BlockSpec((None, None, 1, tn_up), lambda j, i, be, nu: (layer, be[i], 0, j)),
                      pl.BlockSpec((None, None, 1, tn_up), lambda j, i, be, nu: (layer, be[i], 0, nj + j))],
            out_specs=pl.BlockSpec((MOE_BLOCK, tn_up), lambda j, i, be, nu: (i, j)),
            scratch_shapes=[pltpu.VMEM((d, tn_up), BF16), pltpu.VMEM((d, tn_up), BF16)]),
        out_shape=jax.ShapeDtypeStruct((slots, D_FF), BF16),
        compiler_params=_params("arbitrary", "arbitrary"),
        name=name + "_up",
    )(block_e, n_used, x_sorted, w_gate_up, w_gate_up, b_gu, b_gu)
    nj2 = d // tn_down
    return pl.pallas_call(
        _moe_down_kernel,
        grid_spec=pltpu.PrefetchScalarGridSpec(
            num_scalar_prefetch=2,
            grid=(nj2, nblk),
            in_specs=[pl.BlockSpec((MOE_BLOCK, D_FF), lambda j, i, be, nu: (i, 0)),
                      pl.BlockSpec((None, None, D_FF, tn_down), lambda j, i, be, nu: (layer, be[i], 0, j)),
                      pl.BlockSpec((None, None, 1, tn_down), lambda j, i, be, nu: (layer, be[i], 0, j))],
            out_specs=pl.BlockSpec((MOE_BLOCK, tn_down), lambda j, i, be, nu: (i, j)),
            scratch_shapes=[pltpu.VMEM((D_FF, tn_down), BF16)]),
        out_shape=jax.ShapeDtypeStruct((slots, d), BF16),
        compiler_params=_params("arbitrary", "arbitrary"),
        name=name + "_down",
    )(block_e, n_used, act, w_down, b_down.reshape(nl, N_EXPERTS, 1, d))


def _combine_kernel(*refs, split_tile):
    y_refs = refs[:TOP_K]
    gate_ref, x_ref, g2_ref = refs[TOP_K:TOP_K + 3]
    o_refs = refs[TOP_K + 3:]
    gates = gate_ref[...]
    f = None
    for k in range(TOP_K):
        term = gates[:, k:k + 1] * y_refs[k][...].astype(F32)
        f = term if f is None else f + term
    out = x_ref[...] + g2_ref[0] * f
    if split_tile is None:
        o_refs[0][...] = out
    else:
        i = pl.program_id(0)

        @pl.when(i < split_tile)
        def _():
            o_refs[0][...] = out

        @pl.when(i >= split_tile)
        def _():
            o_refs[1][...] = out


def _combine_call(y4, gate_pad, x, mod, n_p, ls, *, tm, split, name):
    n, d = x.shape
    st = n_p // tm
    if split:
        out_specs = [pl.BlockSpec((tm, d), lambda i, j: (jnp.minimum(i, st - 1), 0)),
                     pl.BlockSpec((tm, d), lambda i, j: (jnp.maximum(i - st, 0), 0))]
        out_shape = [jax.ShapeDtypeStruct((n_p, d), F32), jax.ShapeDtypeStruct((n - n_p, d), F32)]
    else:
        out_specs = pl.BlockSpec((tm, d), lambda i, j: (i, 0))
        out_shape = jax.ShapeDtypeStruct((n, d), F32)
    nt = n // tm
    y_specs = [pl.BlockSpec((tm, d), functools.partial(lambda i, j, k: (k * nt + i, 0), k=k))
               for k in range(TOP_K)]
    return pl.pallas_call(
        functools.partial(_combine_kernel, split_tile=st if split else None),
        grid=(nt, 1),
        in_specs=y_specs + [pl.BlockSpec((tm, LANES), lambda i, j: (i, 0)),
                            pl.BlockSpec((tm, d), lambda i, j: (i, 0)),
                            _mod_spec(5, tm, d, n_p, ls)],
        out_specs=out_specs,
        out_shape=out_shape,
        compiler_params=_params("arbitrary", "arbitrary"),
        name=name,
    )(*([y4] * TOP_K), gate_pad, x, mod)


def _moe_layer(x, mod, n_p, ls, layer, norm_w, w_router, b_router, w_gate_up, b_gate_up, w_down, b_down, *,
               split, name):
    n, d = x.shape
    h, idx_pad, gate_pad, rank_pad, cnt = _router_call(x, norm_w.reshape(1, d), mod, n_p, ls, w_router,
                                                       b_router, tm=256, name=name + "_router")
    n_assign = n * TOP_K
    flat_e = idx_pad[:, :TOP_K].reshape(-1)
    counts = cnt[0, :N_EXPERTS]
    padded = (counts + MOE_BLOCK - 1) // MOE_BLOCK * MOE_BLOCK
    pad_end = jnp.cumsum(padded)
    pad_start = pad_end - padded
    dest = pad_start[flat_e] + rank_pad[:, :TOP_K].reshape(-1)
    n_blocks = -(-n_assign // MOE_BLOCK) + N_EXPERTS
    slots = n_blocks * MOE_BLOCK
    slot_tok = (jnp.arange(slots, dtype=jnp.int32) % n).at[dest].set(
        jnp.arange(n_assign, dtype=jnp.int32) // TOP_K, unique_indices=True, mode="promise_in_bounds")
    block_start = jnp.arange(n_blocks, dtype=jnp.int32) * MOE_BLOCK
    block_e = jnp.minimum(jnp.sum((pad_end[None, :] <= block_start[:, None]).astype(jnp.int32), axis=1),
                          N_EXPERTS - 1)
    n_used = (pad_end[-1] // MOE_BLOCK).astype(jnp.int32).reshape(1)
    x_sorted = h.at[slot_tok].get(mode="promise_in_bounds")
    y_sorted = _moe_experts(x_sorted, block_e, n_used, layer, w_gate_up, b_gate_up, w_down, b_down,
                            tn_up=1024, tn_down=2048, name=name)
    dest_by_choice = dest.reshape(n, TOP_K).T.reshape(-1)
    y4 = y_sorted.at[dest_by_choice].get(mode="promise_in_bounds")
    return _combine_call(y4, gate_pad, x, mod, n_p, ls, tm=256, split=split, name=name + "_combine")


def _ada_call(cond, w_ada, b_ada, layer, *, name):
    g = cond.shape[0]
    a = jnp.pad(jax.nn.silu(cond), ((0, 16 - g), (0, 0))).astype(BF16)
    m = _matmul(a, w_ada, n_out=6 * D_MODEL, tm=16, tn=1024, out_dtype=F32, name=name, layer=layer,
                bias=b_ada.reshape(b_ada.shape[0], 1, 6 * D_MODEL))
    return m[:g].reshape(g * 6, 1, D_MODEL)


def kernel(x_prompt, x_sample, c, c_ctx, cache_k, cache_v, state_ssm, norm1_w, norm2_w, w_ada, b_ada, w_qkv, q_norm_w, k_norm_w, lambda_q1, lambda_k1, lambda_q2, lambda_k2, subln_w, w_o, w_in_ssd, conv_w, conv_b, dt_bias, a_log, d_skip, ssd_norm_w, w_out_ssd, w_router, b_router, w_gate_up, b_gate_up, w_down, b_down):
    bp, lp, d = x_prompt.shape
    bs, ls, _ = x_sample.shape
    past = cache_k.shape[2]
    n_p, n_s = bp * lp, bs * ls
    x = jnp.concatenate([x_prompt.reshape(n_p, d), x_sample.reshape(n_s, d)], axis=0)
    cond = jnp.concatenate([c_ctx[None], c], axis=0)
    resid_of = lambda xx, mod, which: (xx, mod, which, n_p, ls)

    n = n_p + n_s
    mod = _ada_call(cond, w_ada, b_ada, 0, name="ada0")
    h = _modnorm_call(x, norm1_w[0].reshape(1, d), mod, 0, n_p, ls, tm=256, name="norm1_0")
    wq = w_qkv[0].astype(BF16)
    qk_w = jnp.stack([q_norm_w[0], k_norm_w[0]], axis=0)
    lam_init = 0.8 - 0.6 * math.exp(-0.3 * 0)
    lam_params = jnp.stack([lambda_q1[0], lambda_k1[0], lambda_q2[0], lambda_k2[0]], axis=0)
    qp, kp, vp = _qkv_call(h, wq, qk_w, None, row_off=0, m=n_p, kv_dtype=F32, tm=512, tn=512, ls=ls,
                           name="qkv_prompt")
    qs, ks, vs = _qkv_call(h, wq, qk_w, _rope_tables(ls), row_off=n_p, m=n_s, kv_dtype=BF16, tm=512, tn=512,
                           ls=ls, name="qkv_sample")
    o = _attn_call(qp, [(kp, vp, lp)], lam_params, subln_w[0], None, n_rows=n, row_off=0, nb=bp, lq=lp,
                   tq=lp, lam_init=lam_init, name="attn_prompt")
    ck = cache_k[:, 0].reshape(bs * past, d)
    cv = cache_v[:, 0].reshape(bs * past, d)
    o = _attn_call(qs, [(ck, cv, past), (ks, vs, ls)], lam_params, subln_w[0], o, n_rows=n, row_off=n_p,
                   nb=bs, lq=ls, tq=256, lam_init=lam_init, name="attn_sample")
    x = _matmul(o, w_o[0].astype(BF16), n_out=d, tm=1024, tn=512, out_dtype=F32, name="attn_out",
                resid=resid_of(x, mod, 2))
    x = _moe_layer(x, mod, n_p, ls, 0, norm2_w[0], w_router[0], b_router[0], w_gate_up, b_gate_up,
                   w_down, b_down, split=False, name="moe0")
    new_k = kp.reshape(bp, 1, lp, ATTN_HEADS, 2, QK_DIM)
    new_v = vp.reshape(bp, 1, lp, ATTN_HEADS, V_DIM)

    mod = _ada_call(cond, w_ada, b_ada, 1, name="ada1")
    h = _modnorm_call(x, norm1_w[1].reshape(1, d), mod, 0, n_p, ls, tm=256, name="norm1_1")
    w_in = w_in_ssd[0].astype(BF16)
    z = _matmul(h, w_in, n_out=SSD_D_INNER, tm=1024, tn=512, out_dtype=BF16, name="ssd_in_z")
    xbc = _matmul(h, w_in, n_out=SSD_CONV_CH, col_off=SSD_D_INNER, tm=1024, tn=512, out_dtype=BF16,
                  name="ssd_in_xbc")
    dt_raw = _matmul(h, w_in, n_out=2 * SSD_HEADS, col_off=SSD_D_INNER + SSD_CONV_CH, tm=1024, tn=LANES,
                     out_dtype=F32, name="ssd_in_dt")
    xbc_act = _conv_call(xbc, conv_w[0], conv_b[0], n_p=n_p, lp=lp, ls=ls, tc=256, tcn=2048, name="ssd_conv")
    ys, fin = [], None
    for direction in (0, 1):
        y, fin = _ssd_scan_call(xbc_act, dt_raw, dt_bias[0], a_log[0], None, None, fin, nb=bp, seq=lp,
                                row_off=0, direction=direction, name="ssd_scan_prompt%d" % direction)
        y, _ = _ssd_scan_call(xbc_act, dt_raw, dt_bias[0], a_log[0], state_ssm[:, 0], y, None, nb=bs, seq=ls,
                              row_off=n_p, direction=direction, name="ssd_scan_sample%d" % direction)
        ys.append(y)
    d_tot = jnp.repeat(d_skip[0, 0] + d_skip[0, 1], SSD_HEAD_DIM).reshape(1, SSD_D_INNER)
    yn = _ssd_post_call(ys[0], ys[1], xbc_act, z, d_tot, ssd_norm_w[0], tm=256, name="ssd_post")
    x = _matmul(yn, w_out_ssd[0].astype(BF16), n_out=d, tm=1024, tn=512, out_dtype=F32, name="ssd_out",
                resid=resid_of(x, mod, 2))
    xp, xs = _moe_layer(x, mod, n_p, ls, 1, norm2_w[1], w_router[1], b_router[1], w_gate_up, b_gate_up,
                        w_down, b_down, split=True, name="moe1")
    new_s = fin[:, None]
    return (xp.reshape(bp, lp, d), xs.reshape(bs, ls, d), new_k, new_v, new_s)
```

```python
import functools
import math

import jax
import jax.numpy as jnp
from jax import lax
from jax.experimental import pallas as pl
from jax.experimental.pallas import tpu as pltpu

F32 = jnp.float32
BF16 = jnp.bfloat16

D_MODEL = 2048
NORM_EPS = 1e-6
GRID_W = 64
ROPE_THETA = 10000.0
ATTN_HEADS = 8
QK_DIM = 128
V_DIM = 256
SSD_D_INNER = 4096
SSD_HEAD_DIM = 64
SSD_HEADS = 64
SSD_GROUPS = 8
SSD_STATE = 128
SSD_CONV = 5
SSD_CHUNK = 128
SSD_BC_DIM = SSD_GROUPS * SSD_STATE
SSD_CONV_CH = SSD_D_INNER + 2 * SSD_BC_DIM
N_EXPERTS = 32
TOP_K = 4
D_FF = 2048
SWIGLU_LIMIT = 7.0
SWIGLU_ALPHA = 1.702
MOE_BLOCK = 256
ATTN_KEY_CHUNK = 512
CAST_ROWS = 256
LANES = 128
GROUP_W = SSD_D_INNER // SSD_GROUPS
assert SSD_CHUNK == SSD_STATE == LANES and 2 * SSD_HEAD_DIM == LANES
VMEM_LIMIT = 56 * 1024 * 1024


def _params(*sem):
    return pltpu.CompilerParams(dimension_semantics=sem, vmem_limit_bytes=VMEM_LIMIT)


def _group_of_tile(i, tm, n_p, ls):
    return jnp.where(i * tm < n_p, 0, 1 + (i * tm - n_p) // ls)


def _mod_spec(which, tm, tn, n_p, ls):
    return pl.BlockSpec((1, 1, tn), lambda i, j: (_group_of_tile(i, tm, n_p, ls) * 6 + which, 0, j))


def _mm_kernel(a_ref, w_ref, *rest, epilogue):
    acc = jnp.dot(a_ref[...].astype(BF16), w_ref[...].astype(BF16), preferred_element_type=F32)
    if epilogue == "bias":
        b_ref, o_ref = rest
        o_ref[...] = (acc + b_ref[...]).astype(o_ref.dtype)
    elif epilogue == "resid":
        x_ref, g_ref, o_ref = rest
        o_ref[...] = x_ref[...] + g_ref[0] * acc
    else:
        (o_ref,) = rest
        o_ref[...] = acc.astype(o_ref.dtype)


def _matmul(a, w, *, n_out, col_off=0, tm, tn, out_dtype, name, layer=None, bias=None, resid=None):
    m, k = a.shape
    joff = col_off // tn
    if layer is None:
        w_spec = pl.BlockSpec((k, tn), lambda i, j: (0, j + joff))
        b_spec = pl.BlockSpec((1, tn), lambda i, j: (0, j))
    else:
        w_spec = pl.BlockSpec((None, k, tn), lambda i, j: (layer, 0, j + joff))
        b_spec = pl.BlockSpec((None, 1, tn), lambda i, j: (layer, 0, j))
    in_specs = [pl.BlockSpec((tm, k), lambda i, j: (i, 0)), w_spec]
    args = [a, w]
    if bias is not None:
        epilogue = "bias"
        in_specs.append(b_spec)
        args.append(bias)
    elif resid is not None:
        epilogue = "resid"
        x, mod, which, n_p, ls = resid
        in_specs += [pl.BlockSpec((tm, tn), lambda i, j: (i, j)), _mod_spec(which, tm, tn, n_p, ls)]
        args += [x, mod]
    else:
        epilogue = "plain"
    return pl.pallas_call(
        functools.partial(_mm_kernel, epilogue=epilogue),
        grid=(m // tm, n_out // tn),
        in_specs=in_specs,
        out_specs=pl.BlockSpec((tm, tn), lambda i, j: (i, j)),
        out_shape=jax.ShapeDtypeStruct((m, n_out), out_dtype),
        compiler_params=_params("parallel", "arbitrary"),
        name=name,
    )(*args)


def _modnorm(x, w, shift, scale):
    ms = jnp.mean(x * x, axis=-1, keepdims=True)
    return (x * lax.rsqrt(ms + NORM_EPS) * w) * (1.0 + scale) + shift


def _modnorm_kernel(x_ref, w_ref, sh_ref, sc_ref, o_ref):
    o_ref[...] = _modnorm(x_ref[...], w_ref[...], sh_ref[0], sc_ref[0]).astype(o_ref.dtype)


def _modnorm_call(x, w, mod, which_shift, n_p, ls, *, tm, name):
    m, d = x.shape
    return pl.pallas_call(
        _modnorm_kernel,
        grid=(m // tm, 1),
        in_specs=[pl.BlockSpec((tm, d), lambda i, j: (i, 0)),
                  pl.BlockSpec((1, d), lambda i, j: (0, 0)),
                  _mod_spec(which_shift, tm, d, n_p, ls),
                  _mod_spec(which_shift + 1, tm, d, n_p, ls)],
        out_specs=pl.BlockSpec((tm, d), lambda i, j: (i, 0)),
        out_shape=jax.ShapeDtypeStruct((m, d), BF16),
        compiler_params=_params("parallel", "arbitrary"),
        name=name,
    )(x, w, mod, mod)


def _split2(v):
    hi = v.astype(BF16)
    lo = (v - hi.astype(F32)).astype(BF16)
    return hi, lo


def _router_kernel(x_ref, w_ref, sh_ref, sc_ref, wr_ref, br_ref, h_ref, idx_ref, gate_ref, rank_ref, cnt_ref,
                   run_ref):
    @pl.when(pl.program_id(0) == 0)
    def _():
        run_ref[...] = jnp.zeros_like(run_ref)

    h = _modnorm(x_ref[...], w_ref[...], sh_ref[0], sc_ref[0])
    h_ref[...] = h.astype(BF16)
    h_hi, h_lo = _split2(h)
    w_hi, w_lo = _split2(wr_ref[...])
    logits = (jnp.dot(h_hi, w_hi, preferred_element_type=F32)
              + jnp.dot(h_hi, w_lo, preferred_element_type=F32)
              + jnp.dot(h_lo, w_hi, preferred_element_type=F32)) + br_ref[...]
    lane = lax.broadcasted_iota(jnp.int32, logits.shape, 1).astype(F32)
    cur = jnp.where(lane < N_EXPERTS, logits, -jnp.inf)
    vals, idxs = [], []
    for _ in range(TOP_K):
        m = jnp.max(cur, axis=-1, keepdims=True)
        am = jnp.min(jnp.where(cur == m, lane, float(LANES)), axis=-1, keepdims=True)
        vals.append(m)
        idxs.append(am)
        cur = jnp.where(lane == am, -jnp.inf, cur)
    exps = [jnp.exp(v - vals[0]) for v in vals]
    denom = exps[0] + exps[1] + exps[2] + exps[3]
    idx_out = jnp.zeros(logits.shape, F32)
    gate_out = jnp.zeros(logits.shape, F32)
    for k in range(TOP_K):
        idx_out = jnp.where(lane == k, idxs[k], idx_out)
        gate_out = jnp.where(lane == k, exps[k] / denom, gate_out)
    idx_ref[...] = idx_out.astype(jnp.int32)
    gate_ref[...] = gate_out
    tm = logits.shape[0]
    chosen = functools.reduce(jnp.logical_or, [lane == am for am in idxs])
    onehot = jnp.where(chosen, 1.0, 0.0)
    ti = lax.broadcasted_iota(jnp.int32, (tm, tm), 0)
    tj = lax.broadcasted_iota(jnp.int32, (tm, tm), 1)
    earlier = jnp.where(ti > tj, 1.0, 0.0).astype(BF16)
    base = run_ref[...] + jnp.dot(earlier, onehot.astype(BF16), preferred_element_type=F32)
    rank_out = jnp.zeros(logits.shape, F32)
    for k in range(TOP_K):
        rk = jnp.sum(jnp.where(lane == idxs[k], base, 0.0), axis=-1, keepdims=True)
        rank_out = jnp.where(lane == k, rk, rank_out)
    rank_ref[...] = rank_out.astype(jnp.int32)
    run_ref[...] = run_ref[...] + jnp.sum(onehot, axis=0, keepdims=True)
    cnt_ref[...] = run_ref[...].astype(jnp.int32)


def _router_call(x, w, mod, n_p, ls, w_router, b_router, *, tm, name):
    m, d = x.shape
    wr = jnp.pad(w_router, ((0, 0), (0, LANES - N_EXPERTS)))
    br = jnp.pad(b_router, (0, LANES - N_EXPERTS)).reshape(1, LANES)
    return pl.pallas_call(
        _router_kernel,
        grid=(m // tm, 1),
        in_specs=[pl.BlockSpec((tm, d), lambda i, j: (i, 0)),
                  pl.BlockSpec((1, d), lambda i, j: (0, 0)),
                  _mod_spec(3, tm, d, n_p, ls),
                  _mod_spec(4, tm, d, n_p, ls),
                  pl.BlockSpec((d, LANES), lambda i, j: (0, 0)),
                  pl.BlockSpec((1, LANES), lambda i, j: (0, 0))],
        out_specs=[pl.BlockSpec((tm, d), lambda i, j: (i, 0)),
                   pl.BlockSpec((tm, LANES), lambda i, j: (i, 0)),
                   pl.BlockSpec((tm, LANES), lambda i, j: (i, 0)),
                   pl.BlockSpec((tm, LANES), lambda i, j: (i, 0)),
                   pl.BlockSpec((1, LANES), lambda i, j: (0, 0))],
        out_shape=[jax.ShapeDtypeStruct((m, d), BF16),
                   jax.ShapeDtypeStruct((m, LANES), jnp.int32),
                   jax.ShapeDtypeStruct((m, LANES), F32),
                   jax.ShapeDtypeStruct((m, LANES), jnp.int32),
                   jax.ShapeDtypeStruct((1, LANES), jnp.int32)],
        scratch_shapes=[pltpu.VMEM((1, LANES), F32)],
        compiler_params=_params("arbitrary", "arbitrary"),
        name=name,
    )(x, w, mod, mod, wr, br)


def _qkv_kernel(a_ref, w_ref, nw_ref, *rest, rope, tn):
    if rope:
        cos_ref, sin_ref, q_ref, k_ref, v_ref = rest
    else:
        q_ref, k_ref, v_ref = rest
    j = pl.program_id(1)
    nq = D_MODEL // tn
    acc = jnp.dot(a_ref[...], w_ref[...], preferred_element_type=F32)

    def normed(widx, out_scale):
        w = nw_ref[widx:widx + 1, :]
        outs = []
        for c in range(tn // QK_DIM):
            xc = acc[:, c * QK_DIM:(c + 1) * QK_DIM]
            ms = jnp.mean(xc * xc, axis=-1, keepdims=True)
            y = xc * lax.rsqrt(ms + NORM_EPS) * w
            if rope:
                lane = lax.broadcasted_iota(jnp.int32, y.shape, 1)
                partner = jnp.where(lane % 64 < 32, pltpu.roll(y, QK_DIM - 32, 1), pltpu.roll(y, 32, 1))
                y = y * cos_ref[...] + partner * sin_ref[...]
            outs.append(y * out_scale if out_scale != 1.0 else y)
        return jnp.concatenate(outs, axis=1)

    @pl.when(j < nq)
    def _():
        q_ref[...] = normed(0, QK_DIM ** -0.5).astype(q_ref.dtype)

    @pl.when((j >= nq) & (j < 2 * nq))
    def _():
        k_ref[...] = normed(1, 1.0).astype(k_ref.dtype)

    @pl.when(j >= 2 * nq)
    def _():
        v_ref[...] = acc.astype(v_ref.dtype)


def _qkv_call(h, w_qkv, qk_norm_w, rope_tabs, *, row_off, m, kv_dtype, tm, tn, ls, name):
    d = h.shape[1]
    nq = d // tn
    ioff = row_off // tm
    rope = rope_tabs is not None
    in_specs = [pl.BlockSpec((tm, d), lambda i, j: (i + ioff, 0)),
                pl.BlockSpec((d, tn), lambda i, j: (0, j)),
                pl.BlockSpec((2, QK_DIM), lambda i, j: (0, 0))]
    args = [h, w_qkv, qk_norm_w]
    if rope:
        nt = ls // tm
        in_specs += [pl.BlockSpec((tm, QK_DIM), lambda i, j: (i % nt, 0))] * 2
        args += list(rope_tabs)
    return pl.pallas_call(
        functools.partial(_qkv_kernel, rope=rope, tn=tn),
        grid=(m // tm, 3 * nq),
        in_specs=in_specs,
        out_specs=[pl.BlockSpec((tm, tn), lambda i, j: (i, jnp.minimum(j, nq - 1))),
                   pl.BlockSpec((tm, tn), lambda i, j: (i, jnp.clip(j - nq, 0, nq - 1))),
                   pl.BlockSpec((tm, tn), lambda i, j: (i, jnp.clip(j - 2 * nq, 0, nq - 1)))],
        out_shape=[jax.ShapeDtypeStruct((m, d), BF16),
                   jax.ShapeDtypeStruct((m, d), kv_dtype),
                   jax.ShapeDtypeStruct((m, d), kv_dtype)],
        compiler_params=_params("parallel", "arbitrary"),
        name=name,
    )(*args)


def _rope_tables(ls):
    rows = ls // GRID_W
    row = jnp.repeat(jnp.arange(rows, dtype=F32), GRID_W)
    col = jnp.tile(jnp.arange(GRID_W, dtype=F32), rows)
    axis_dim = QK_DIM // 2
    inv = ROPE_THETA ** (-jnp.arange(0, axis_dim, 2, dtype=F32) / axis_dim)
    ar, ac = row[:, None] * inv, col[:, None] * inv
    cos = jnp.concatenate([jnp.cos(ar), jnp.cos(ar), jnp.cos(ac), jnp.cos(ac)], axis=1)
    sin = jnp.concatenate([-jnp.sin(ar), jnp.sin(ar), -jnp.sin(ac), jnp.sin(ac)], axis=1)
    return cos, sin


def _attn_kernel(lam_ref, subw_ref, q_ref, *rest, nseg, lam_init):
    k_refs, v_refs, o_ref = rest[:nseg], rest[nseg:2 * nseg], rest[-1]
    lp = lam_ref[...]
    lam = (jnp.exp(jnp.sum(lp[0:1] * lp[1:2], axis=-1, keepdims=True))
           - jnp.exp(jnp.sum(lp[2:3] * lp[3:4], axis=-1, keepdims=True)) + lam_init)
    q = q_ref[...]
    chunks = [(k_ref, v_ref, c0) for k_ref, v_ref in zip(k_refs, v_refs)
              for c0 in range(0, k_ref.shape[0], min(ATTN_KEY_CHUNK, k_ref.shape[0]))]
    outs = []
    for mi in range(2):
        qm = q[:, mi * QK_DIM:(mi + 1) * QK_DIM]
        mx = den = pv = None
        for k_ref, v_ref, c0 in chunks:
            kc = min(ATTN_KEY_CHUNK, k_ref.shape[0])
            k = k_ref[c0:c0 + kc, mi * QK_DIM:(mi + 1) * QK_DIM].astype(BF16)
            v = v_ref[c0:c0 + kc, :].astype(BF16)
            s = lax.dot_general(qm, k, (((1,), (1,)), ((), ())), preferred_element_type=F32)
            cmax = jnp.max(s, axis=-1, keepdims=True)
            if mx is None:
                mx = cmax
                p = jnp.exp(s - mx)
                den = jnp.sum(p, axis=-1, keepdims=True)
                pv = jnp.dot(p.astype(BF16), v, preferred_element_type=F32)
            else:
                new_mx = jnp.maximum(mx, cmax)
                alpha = jnp.exp(mx - new_mx)
                p = jnp.exp(s - new_mx)
                den = den * alpha + jnp.sum(p, axis=-1, keepdims=True)
                pv = pv * alpha + jnp.dot(p.astype(BF16), v, preferred_element_type=F32)
                mx = new_mx
        outs.append(pv * (1.0 / den))
    acc = outs[0] - lam * outs[1]
    ms = jnp.mean(acc * acc, axis=-1, keepdims=True)
    o = acc * lax.rsqrt(ms + NORM_EPS) * subw_ref[...] * (1.0 - lam_init)
    o_ref[...] = o.astype(o_ref.dtype)


def _attn_call(q, kv_segs, lam_params, subln_w, o_into, *, n_rows, row_off, nb, lq, tq, lam_init, name):
    nqb = lq // tq
    nseg = len(kv_segs)
    ooff = row_off // tq
    in_specs = [pl.BlockSpec((4, QK_DIM), lambda b, h, i: (0, 0)),
                pl.BlockSpec((1, V_DIM), lambda b, h, i: (0, 0)),
                pl.BlockSpec((tq, V_DIM), lambda b, h, i: (b * nqb + i, h))]
    in_specs += [pl.BlockSpec((lk, V_DIM), lambda b, h, i: (b, h)) for (_, _, lk) in kv_segs] * 2
    args = [lam_params, subln_w.reshape(1, V_DIM), q]
    args += [k for (k, _, _) in kv_segs] + [v for (_, v, _) in kv_segs]
    aliases = {}
    if o_into is not None:
        in_specs.append(pl.BlockSpec(memory_space=pl.ANY))
        aliases = {len(args): 0}
        args.append(o_into)
    return pl.pallas_call(
        functools.partial(_attn_kernel, nseg=nseg, lam_init=lam_init),
        grid=(nb, ATTN_HEADS, nqb),
        in_specs=in_specs,
        out_specs=pl.BlockSpec((tq, V_DIM), lambda b, h, i: (ooff + b * nqb + i, h)),
        out_shape=jax.ShapeDtypeStruct((n_rows, q.shape[1]), BF16),
        input_output_aliases=aliases,
        compiler_params=_params("parallel", "parallel", "arbitrary"),
        name=name,
    )(*args)


def _conv_kernel(xp_ref, x_ref, xn_ref, w_ref, b_ref, o_ref, ext_ref, *, tc, n_p, lp, ls, halo):
    i = pl.program_id(0)
    row0 = i * tc
    in_prompt = row0 < n_p
    first = jnp.where(in_prompt, row0 % lp == 0, (row0 - n_p) % ls == 0)
    last = jnp.where(in_prompt, (row0 + tc) % lp == 0, (row0 + tc - n_p) % ls == 0)
    prev = xp_ref[...].astype(F32)[halo - 8:halo, :]
    nxt = xn_ref[...].astype(F32)[0:8, :]
    ext_ref[0:8, :] = jnp.where(first, 0.0, prev)
    ext_ref[8:8 + tc, :] = x_ref[...].astype(F32)
    ext_ref[8 + tc:16 + tc, :] = jnp.where(last, 0.0, nxt)
    acc = b_ref[...] + w_ref[0:1, :] * ext_ref[6:6 + tc, :]
    for k in range(1, SSD_CONV):
        acc = acc + w_ref[k:k + 1, :] * ext_ref[6 + k:6 + k + tc, :]
    o_ref[...] = (acc / (1.0 + jnp.exp(-acc))).astype(o_ref.dtype)


def _conv_call(xbc, conv_w, conv_b, *, n_p, lp, ls, tc, tcn, name):
    m, ch = xbc.shape
    halo = 16
    hb = tc // halo
    nhb = m // halo
    return pl.pallas_call(
        functools.partial(_conv_kernel, tc=tc, n_p=n_p, lp=lp, ls=ls, halo=halo),
        grid=(m // tc, ch // tcn),
        in_specs=[pl.BlockSpec((halo, tcn), lambda i, j: (jnp.maximum(i * hb - 1, 0), j)),
                  pl.BlockSpec((tc, tcn), lambda i, j: (i, j)),
                  pl.BlockSpec((halo, tcn), lambda i, j: (jnp.minimum((i + 1) * hb, nhb - 1), j)),
                  pl.BlockSpec((SSD_CONV, tcn), lambda i, j: (0, j)),
                  pl.BlockSpec((1, tcn), lambda i, j: (0, j))],
        out_specs=pl.BlockSpec((tc, tcn), lambda i, j: (i, j)),
        out_shape=jax.ShapeDtypeStruct((m, ch), BF16),
        scratch_shapes=[pltpu.VMEM((tc + 16, tcn), F32)],
        compiler_params=_params("parallel", "arbitrary"),
        name=name,
    )(xbc, xbc, xbc, conv_w, conv_b.reshape(1, ch))


def _ssd_kernel(xbc_ref, dtraw_ref, dtb_ref, alog_ref, tri_ref, *rest, direction, zero_init, nt):
    y_ref, fin_ref, st_ref = rest[-3:]
    if not zero_init:
        h0_ref = rest[0]
    t = pl.program_id(1)
    q = SSD_CHUNK
    hpg = SSD_HEADS // SSD_GROUPS

    @pl.when(t == 0)
    def _():
        for g in range(SSD_GROUPS):
            if zero_init:
                st_ref[g] = jnp.zeros((SSD_STATE, GROUP_W), F32)
            else:
                hg = h0_ref[0, 0, g * hpg:(g + 1) * hpg].reshape(GROUP_W, SSD_STATE)
                st_ref[g] = hg.T

    x = dtraw_ref[...] + dtb_ref[...]
    dt = jnp.maximum(x, 0.0) + jnp.log1p(jnp.exp(-jnp.abs(x)))
    a = dt * (-jnp.exp(alog_ref[...]))
    tri = tri_ref[...]
    a1 = a.astype(BF16)
    r1 = a - a1.astype(F32)
    a2 = r1.astype(BF16)
    a3 = (r1 - a2.astype(F32)).astype(BF16)
    cs = (jnp.dot(tri, a1, preferred_element_type=F32) + jnp.dot(tri, a2, preferred_element_type=F32)
          + jnp.dot(tri, a3, preferred_element_type=F32))
    tot = cs[q - 1:q, :] if direction == 0 else cs[0:1, :]
    cdec = jnp.exp(tot)
    cs_t = cs.T
    dt_t = dt.T
    wend_t = (dt * jnp.exp(tot - cs)).T
    li = lax.broadcasted_iota(jnp.int32, (q, q), 0)
    si = lax.broadcasted_iota(jnp.int32, (q, q), 1)
    mask = (li >= si) if direction == 0 else (li <= si)
    low = lax.broadcasted_iota(jnp.int32, (q, LANES), 1) < SSD_HEAD_DIM
    low_row = low[0:1, :]

    for g in range(SSD_GROUPS):
        bg = xbc_ref[:, SSD_D_INNER + g * SSD_STATE:SSD_D_INNER + (g + 1) * SSD_STATE]
        cg = xbc_ref[:, SSD_D_INNER + SSD_BC_DIM + g * SSD_STATE:SSD_D_INNER + SSD_BC_DIM + (g + 1) * SSD_STATE]
        cg32 = cg.astype(F32)
        bg_t = bg.astype(F32).T
        cb = lax.dot_general(cg, bg, (((1,), (1,)), ((), ())), preferred_element_type=F32)
        for p in range(hpg // 2):
            col0 = g * GROUP_W + p * LANES
            xp = xbc_ref[:, col0:col0 + LANES]
            zero = jnp.zeros_like(xp)
            x_lo = jnp.where(low, xp, zero)
            x_hi = jnp.where(low, zero, xp)
            s_in = st_ref[g, :, p * LANES:(p + 1) * LANES]
            s_bf = s_in.astype(BF16)
            s_lo = jnp.where(low, s_bf, zero)
            s_hi = jnp.where(low, zero, s_bf)
            c0 = direction * SSD_HEADS + g * hpg + 2 * p
            diag, offd, bw = [], [], []
            for c in (c0, c0 + 1):
                col = jnp.broadcast_to(cs[:, c:c + 1], (q, q))
                seg = col - cs_t[c:c + 1, :]
                diag.append((cb * jnp.exp(jnp.where(mask, seg, -jnp.inf)) * dt_t[c:c + 1, :]).astype(BF16))
                offd.append((cg32 * jnp.exp(col)).astype(BF16))
                bw.append((bg_t * wend_t[c:c + 1, :]).astype(BF16))
            lhs = jnp.concatenate(diag + offd, axis=1)
            rhs = jnp.concatenate([x_lo, x_hi, s_lo, s_hi], axis=0)
            y_ref[:, col0:col0 + LANES] = jnp.dot(lhs, rhs, preferred_element_type=F32).astype(y_ref.dtype)
            new = jnp.dot(jnp.concatenate(bw, axis=1), jnp.concatenate([x_lo, x_hi], axis=0),
                          preferred_element_type=F32)
            keep = jnp.where(low_row, cdec[:, c0:c0 + 1], cdec[:, c0 + 1:c0 + 2])
            st_ref[g, :, p * LANES:(p + 1) * LANES] = s_in * keep + new

    @pl.when(t == nt - 1)
    def _():
        for g in range(SSD_GROUPS):
            fin_ref[0, g * hpg:(g + 1) * hpg] = st_ref[g].T.reshape(hpg, SSD_HEAD_DIM, SSD_STATE)


def _ssd_scan_call(xbc_act, dt_raw, dt_bias, a_log, h0, y_into, fin_into, *, nb, seq, row_off, direction,
                   name):
    q = SSD_CHUNK
    nt = seq // q
    boff = row_off // q
    li = jnp.arange(q)[:, None]
    si = jnp.arange(q)[None, :]
    tri = ((li >= si) if direction == 0 else (li <= si)).astype(BF16)

    def chunk(b, t):
        return boff + b * nt + (t if direction == 0 else nt - 1 - t)

    zero_init = h0 is None
    in_specs = [pl.BlockSpec((q, SSD_CONV_CH), lambda b, t: (chunk(b, t), 0)),
                pl.BlockSpec((q, LANES), lambda b, t: (chunk(b, t), 0)),
                pl.BlockSpec((1, LANES), lambda b, t: (0, 0)),
                pl.BlockSpec((1, LANES), lambda b, t: (0, 0)),
                pl.BlockSpec((q, q), lambda b, t: (0, 0))]
    args = [xbc_act, dt_raw, dt_bias.reshape(1, LANES), a_log.reshape(1, LANES), tri]
    if not zero_init:
        in_specs.append(pl.BlockSpec((1, 1, SSD_HEADS, SSD_HEAD_DIM, SSD_STATE),
                                     lambda b, t: (b, direction, 0, 0, 0)))
        args.append(h0)
    aliases = {}
    if y_into is not None:
        in_specs.append(pl.BlockSpec(memory_space=pl.ANY))
        aliases = {len(args): 0}
        args.append(y_into)
    if fin_into is not None:
        in_specs.append(pl.BlockSpec(memory_space=pl.ANY))
        aliases[len(args)] = 1
        args.append(fin_into)
    return pl.pallas_call(
        functools.partial(_ssd_kernel, direction=direction, zero_init=zero_init, nt=nt),
        grid=(nb, nt),
        in_specs=in_specs,
        out_specs=[pl.BlockSpec((q, SSD_D_INNER), lambda b, t: (chunk(b, t), 0)),
                   pl.BlockSpec((1, None, SSD_HEADS, SSD_HEAD_DIM, SSD_STATE),
                                lambda b, t: (b, direction, 0, 0, 0))],
        out_shape=[jax.ShapeDtypeStruct((xbc_act.shape[0], SSD_D_INNER), BF16),
                   jax.ShapeDtypeStruct((nb, 2, SSD_HEADS, SSD_HEAD_DIM, SSD_STATE), F32)],
        scratch_shapes=[pltpu.VMEM((SSD_GROUPS, SSD_STATE, GROUP_W), F32)],
        input_output_aliases=aliases,
        compiler_params=_params("parallel", "arbitrary"),
        name=name,
    )(*args)


def _ssd_post_kernel(yf_ref, yb_ref, xs_ref, z_ref, d_ref, nw_ref, o_ref):
    z = z_ref[...].astype(F32)
    y = yf_ref[...].astype(F32) + yb_ref[...].astype(F32) + d_ref[...] * xs_ref[...].astype(F32)
    y = y * (z / (1.0 + jnp.exp(-z)))
    outs = []
    for g in range(SSD_GROUPS):
        yg = y[:, g * GROUP_W:(g + 1) * GROUP_W]
        ms = jnp.mean(yg * yg, axis=-1, keepdims=True)
        outs.append(yg * lax.rsqrt(ms + NORM_EPS) * nw_ref[:, g * GROUP_W:(g + 1) * GROUP_W])
    o_ref[...] = jnp.concatenate(outs, axis=1).astype(o_ref.dtype)


def _ssd_post_call(y_f, y_b, xbc_act, z, d_tot, norm_w, *, tm, name):
    m = y_f.shape[0]
    di = SSD_D_INNER
    row = pl.BlockSpec((tm, di), lambda i: (i, 0))
    vec = pl.BlockSpec((1, di), lambda i: (0, 0))
    return pl.pallas_call(
        _ssd_post_kernel,
        grid=(m // tm,),
        in_specs=[row, row, row, row, vec, vec],
        out_specs=row,
        out_shape=jax.ShapeDtypeStruct((m, di), BF16),
        compiler_params=_params("parallel"),
        name=name,
    )(y_f, y_b, xbc_act, z, d_tot, norm_w.reshape(1, di))


def _run_schedule(block_e, n_used):
    nblk = block_e.shape[0]
    idx = jnp.arange(nblk, dtype=jnp.int32)
    prev = jnp.concatenate([block_e[:1] - 1, block_e[:-1]])
    first = ((idx < n_used[0]) & (block_e != prev)).astype(jnp.int32)
    ridx = jnp.cumsum(first) - 1
    is_next = (first[None, :] == 1) & (ridx[None, :] == ridx[:, None] + 1)
    next_e = jnp.where(jnp.any(is_next, axis=1), jnp.sum(jnp.where(is_next, block_e[None, :], 0), axis=1), -1)
    counts = jnp.stack([n_used[0], jnp.sum(first)])
    return (block_e, counts.astype(jnp.int32), first, ridx.astype(jnp.int32), next_e.astype(jnp.int32))


def _weight_copies(w_hbm, stage, sems, layer, e, jj, slot, col_tiles, tn):
    return [pltpu.make_async_copy(w_hbm.at[layer, e, :, pl.ds(pl.multiple_of((off + jj) * tn, tn), tn)],
                                  stage.at[slot, wi], sems.at[slot, wi])
            for wi, off in enumerate(col_tiles)]


def _stage_run_weights(sched, w_hbm, stage, sems, w_bf, *, layer, col_tiles, tn, nj):
    be_ref, cnt_ref, first_ref, ridx_ref, next_ref = sched
    j = pl.program_id(0)
    i = pl.program_id(1)
    copies = functools.partial(_weight_copies, w_hbm, stage, sems, layer, col_tiles=col_tiles, tn=tn)

    @pl.when(first_ref[i] == 1)
    def _():
        slot = (j * cnt_ref[1] + ridx_ref[i]) & 1
        e = be_ref[i]

        @pl.when((i == 0) & (j == 0))
        def _():
            for c in copies(e, j, slot):
                c.start()

        for c in copies(e, j, slot):
            c.wait()
        def cast_rows(r, carry):
            rows = pl.ds(pl.multiple_of(r * CAST_ROWS, CAST_ROWS), CAST_ROWS)
            for wi in range(len(col_tiles)):
                w_bf[wi, rows, :] = stage[slot, wi, rows, :].astype(BF16)
            return carry

        lax.fori_loop(0, w_bf.shape[1] // CAST_ROWS, cast_rows, 0)
        nxt = next_ref[i]

        @pl.when(nxt >= 0)
        def _():
            for c in copies(nxt, j, 1 - slot):
                c.start()

        @pl.when((nxt < 0) & (j + 1 < nj))
        def _():
            for c in copies(be_ref[0], j + 1, 1 - slot):
                c.start()


def _moe_up_kernel(*refs, layer, tn, nj):
    sched, (x_ref, w_hbm, bg_ref, bu_ref, o_ref, stage, w_bf, sems) = refs[:5], refs[5:]
    i = pl.program_id(1)
    _stage_run_weights(sched, w_hbm, stage, sems, w_bf, layer=layer, col_tiles=(0, nj), tn=tn, nj=nj)
    nu_ref = sched[1]

    @pl.when(i < nu_ref[0])
    def _():
        x = x_ref[...]
        g = jnp.dot(x, w_bf[0], preferred_element_type=F32) + bg_ref[...]
        u = jnp.dot(x, w_bf[1], preferred_element_type=F32) + bu_ref[...]
        g = jnp.minimum(g, SWIGLU_LIMIT)
        u = jnp.clip(u, -SWIGLU_LIMIT, SWIGLU_LIMIT)
        act = (u + 1.0) * (g / (1.0 + jnp.exp(-SWIGLU_ALPHA * g)))
        o_ref[...] = act.astype(o_ref.dtype)

    @pl.when(i >= nu_ref[0])
    def _():
        o_ref[...] = jnp.zeros_like(o_ref)


def _moe_down_kernel(*refs, layer, tn, nj):
    sched, (a_ref, w_hbm, b_ref, o_ref, stage, w_bf, sems) = refs[:5], refs[5:]
    i = pl.program_id(1)
    _stage_run_weights(sched, w_hbm, stage, sems, w_bf, layer=layer, col_tiles=(0,), tn=tn, nj=nj)
    nu_ref = sched[1]

    @pl.when(i < nu_ref[0])
    def _():
        y = jnp.dot(a_ref[...], w_bf[0], preferred_element_type=F32) + b_ref[...]
        o_ref[...] = y.astype(o_ref.dtype)

    @pl.when(i >= nu_ref[0])
    def _():
        o_ref[...] = jnp.zeros_like(o_ref)


def _moe_experts(x_sorted, block_e, n_used, layer, w_gate_up, b_gate_up, w_down, b_down, *, tn_up, tn_down,
                 name):
    slots, d = x_sorted.shape
    nblk = slots // MOE_BLOCK
    nj = D_FF // tn_up
    nl = w_gate_up.shape[0]
    b_gu = b_gate_up.reshape(nl, N_EXPERTS, 1, 2 * D_FF)
    sched = _run_schedule(block_e, n_used)
    act = pl.pallas_call(
        functools.partial(_moe_up_kernel, layer=layer, tn=tn_up, nj=nj),
        grid_spec=pltpu.PrefetchScalarGridSpec(
            num_scalar_prefetch=len(sched),
            grid=(nj, nblk),
            in_specs=[pl.BlockSpec((MOE_BLOCK, d), lambda j, i, *s: (i, 0)),
                      pl.BlockSpec(memory_space=pl.ANY),
                      pl.BlockSpec((None, None, 1, tn_up), lambda j, i, be, *s: (layer, be[i], 0, j)),
                      pl.BlockSpec((None, None, 1, tn_up), lambda j, i, be, *s: (layer, be[i], 0, nj + j))],
            out_specs=pl.BlockSpec((MOE_BLOCK, tn_up), lambda j, i, *s: (i, j)),
            scratch_shapes=[pltpu.VMEM((2, 2, d, tn_up), F32), pltpu.VMEM((2, d, tn_up), BF16),
                            pltpu.SemaphoreType.DMA((2, 2))]),
        out_shape=jax.ShapeDtypeStruct((slots, D_FF), BF16),
        compiler_params=_params("arbitrary", "arbitrary"),
        name=name + "_up",
    )(*sched, x_sorted, w_gate_up, b_gu, b_gu)
    nj2 = d // tn_down
    return pl.pallas_call(
        functools.partial(_moe_down_kernel, layer=layer, tn=tn_down, nj=nj2),
        grid_spec=pltpu.PrefetchScalarGridSpec(
            num_scalar_prefetch=len(sched),
            grid=(nj2, nblk),
            in_specs=[pl.BlockSpec((MOE_BLOCK, D_FF), lambda j, i, *s: (i, 0)),
                      pl.BlockSpec(memory_space=pl.ANY),
                      pl.BlockSpec((None, None, 1, tn_down), lambda j, i, be, *s: (layer, be[i], 0, j))],
            out_specs=pl.BlockSpec((MOE_BLOCK, tn_down), lambda j, i, *s: (i, j)),
            scratch_shapes=[pltpu.VMEM((2, 1, D_FF, tn_down), F32), pltpu.VMEM((1, D_FF, tn_down), BF16),
                            pltpu.SemaphoreType.DMA((2, 1))]),
        out_shape=jax.ShapeDtypeStruct((slots, d), BF16),
        compiler_params=_params("arbitrary", "arbitrary"),
        name=name + "_down",
    )(*sched, act, w_down, b_down.reshape(nl, N_EXPERTS, 1, d))


def _combine_kernel(*refs, split_tile):
    y_refs = refs[:TOP_K]
    gate_ref, x_ref, g2_ref = refs[TOP_K:TOP_K + 3]
    o_refs = refs[TOP_K + 3:]
    gates = gate_ref[...]
    f = None
    for k in range(TOP_K):
        term = gates[:, k:k + 1] * y_refs[k][...].astype(F32)
        f = term if f is None else f + term
    out = x_ref[...] + g2_ref[0] * f
    if split_tile is None:
        o_refs[0][...] = out
    else:
        i = pl.program_id(0)

        @pl.when(i < split_tile)
        def _():
            o_refs[0][...] = out

        @pl.when(i >= split_tile)
        def _():
            o_refs[1][...] = out


def _combine_call(y4, gate_pad, x, mod, n_p, ls, *, tm, split, name):
    n, d = x.shape
    st = n_p // tm
    if split:
        out_specs = [pl.BlockSpec((tm, d), lambda i, j: (jnp.minimum(i, st - 1), 0)),
                     pl.BlockSpec((tm, d), lambda i, j: (jnp.maximum(i - st, 0), 0))]
        out_shape = [jax.ShapeDtypeStruct((n_p, d), F32), jax.ShapeDtypeStruct((n - n_p, d), F32)]
    else:
        out_specs = pl.BlockSpec((tm, d), lambda i, j: (i, 0))
        out_shape = jax.ShapeDtypeStruct((n, d), F32)
    nt = n // tm
    y_specs = [pl.BlockSpec((tm, d), functools.partial(lambda i, j, k: (k * nt + i, 0), k=k))
               for k in range(TOP_K)]
    return pl.pallas_call(
        functools.partial(_combine_kernel, split_tile=st if split else None),
        grid=(nt, 1),
        in_specs=y_specs + [pl.BlockSpec((tm, LANES), lambda i, j: (i, 0)),
                            pl.BlockSpec((tm, d), lambda i, j: (i, 0)),
                            _mod_spec(5, tm, d, n_p, ls)],
        out_specs=out_specs,
        out_shape=out_shape,
        compiler_params=_params("arbitrary", "arbitrary"),
        name=name,
    )(*([y4] * TOP_K), gate_pad, x, mod)


def _moe_layer(x, mod, n_p, ls, layer, norm_w, w_router, b_router, w_gate_up, b_gate_up, w_down, b_down, *,
               split, name):
    n, d = x.shape
    h, idx_pad, gate_pad, rank_pad, cnt = _router_call(x, norm_w.reshape(1, d), mod, n_p, ls, w_router,
                                                       b_router, tm=256, name=name + "_router")
    n_assign = n * TOP_K
    flat_e = idx_pad[:, :TOP_K].reshape(-1)
    counts = cnt[0, :N_EXPERTS]
    padded = (counts + MOE_BLOCK - 1) // MOE_BLOCK * MOE_BLOCK
    pad_end = jnp.cumsum(padded)
    pad_start = pad_end - padded
    dest = pad_start[flat_e] + rank_pad[:, :TOP_K].reshape(-1)
    n_blocks = -(-n_assign // MOE_BLOCK) + N_EXPERTS
    slots = n_blocks * MOE_BLOCK
    slot_tok = (jnp.arange(slots, dtype=jnp.int32) % n).at[dest].set(
        jnp.arange(n_assign, dtype=jnp.int32) // TOP_K, unique_indices=True, mode="promise_in_bounds")
    block_start = jnp.arange(n_blocks, dtype=jnp.int32) * MOE_BLOCK
    block_e = jnp.minimum(jnp.sum((pad_end[None, :] <= block_start[:, None]).astype(jnp.int32), axis=1),
                          N_EXPERTS - 1)
    n_used = (pad_end[-1] // MOE_BLOCK).astype(jnp.int32).reshape(1)
    x_sorted = h.at[slot_tok].get(mode="promise_in_bounds")
    y_sorted = _moe_experts(x_sorted, block_e, n_used, layer, w_gate_up, b_gate_up, w_down, b_down,
                            tn_up=1024, tn_down=2048, name=name)
    dest_by_choice = dest.reshape(n, TOP_K).T.reshape(-1)
    y4 = y_sorted.at[dest_by_choice].get(mode="promise_in_bounds")
    return _combine_call(y4, gate_pad, x, mod, n_p, ls, tm=256, split=split, name=name + "_combine")


def _ada_call(cond, w_ada, b_ada, layer, *, name):
    g = cond.shape[0]
    a = jnp.pad(jax.nn.silu(cond), ((0, 16 - g), (0, 0))).astype(BF16)
    m = _matmul(a, w_ada, n_out=6 * D_MODEL, tm=16, tn=1024, out_dtype=F32, name=name, layer=layer,
                bias=b_ada.reshape(b_ada.shape[0], 1, 6 * D_MODEL))
    return m[:g].reshape(g * 6, 1, D_MODEL)


def kernel(x_prompt, x_sample, c, c_ctx, cache_k, cache_v, state_ssm, norm1_w, norm2_w, w_ada, b_ada, w_qkv, q_norm_w, k_norm_w, lambda_q1, lambda_k1, lambda_q2, lambda_k2, subln_w, w_o, w_in_ssd, conv_w, conv_b, dt_bias, a_log, d_skip, ssd_norm_w, w_out_ssd, w_router, b_router, w_gate_up, b_gate_up, w_down, b_down):
    bp, lp, d = x_prompt.shape
    bs, ls, _ = x_sample.shape
    past = cache_k.shape[2]
    n_p, n_s = bp * lp, bs * ls
    x = jnp.concatenate([x_prompt.reshape(n_p, d), x_sample.reshape(n_s, d)], axis=0)
    cond = jnp.concatenate([c_ctx[None], c], axis=0)
    resid_of = lambda xx, mod, which: (xx, mod, which, n_p, ls)

    n = n_p + n_s
    mod = _ada_call(cond, w_ada, b_ada, 0, name="ada0")
    h = _modnorm_call(x, norm1_w[0].reshape(1, d), mod, 0, n_p, ls, tm=256, name="norm1_0")
    wq = w_qkv[0].astype(BF16)
    qk_w = jnp.stack([q_norm_w[0], k_norm_w[0]], axis=0)
    lam_init = 0.8 - 0.6 * math.exp(-0.3 * 0)
    lam_params = jnp.stack([lambda_q1[0], lambda_k1[0], lambda_q2[0], lambda_k2[0]], axis=0)
    qp, kp, vp = _qkv_call(h, wq, qk_w, None, row_off=0, m=n_p, kv_dtype=F32, tm=512, tn=512, ls=ls,
                           name="qkv_prompt")
    qs, ks, vs = _qkv_call(h, wq, qk_w, _rope_tables(ls), row_off=n_p, m=n_s, kv_dtype=BF16, tm=512, tn=512,
                           ls=ls, name="qkv_sample")
    o = _attn_call(qp, [(kp, vp, lp)], lam_params, subln_w[0], None, n_rows=n, row_off=0, nb=bp, lq=lp,
                   tq=lp, lam_init=lam_init, name="attn_prompt")
    ck = cache_k[:, 0].reshape(bs * past, d)
    cv = cache_v[:, 0].reshape(bs * past, d)
    o = _attn_call(qs, [(ck, cv, past), (ks, vs, ls)], lam_params, subln_w[0], o, n_rows=n, row_off=n_p,
                   nb=bs, lq=ls, tq=256, lam_init=lam_init, name="attn_sample")
    x = _matmul(o, w_o[0].astype(BF16), n_out=d, tm=1024, tn=512, out_dtype=F32, name="attn_out",
                resid=resid_of(x, mod, 2))
    x = _moe_layer(x, mod, n_p, ls, 0, norm2_w[0], w_router[0], b_router[0], w_gate_up, b_gate_up,
                   w_down, b_down, split=False, name="moe0")
    new_k = kp.reshape(bp, 1, lp, ATTN_HEADS, 2, QK_DIM)
    new_v = vp.reshape(bp, 1, lp, ATTN_HEADS, V_DIM)

    mod = _ada_call(cond, w_ada, b_ada, 1, name="ada1")
    h = _modnorm_call(x, norm1_w[1].reshape(1, d), mod, 0, n_p, ls, tm=256, name="norm1_1")
    w_in = w_in_ssd[0].astype(BF16)
    z = _matmul(h, w_in, n_out=SSD_D_INNER, tm=1024, tn=512, out_dtype=BF16, name="ssd_in_z")
    xbc = _matmul(h, w_in, n_out=SSD_CONV_CH, col_off=SSD_D_INNER, tm=1024, tn=512, out_dtype=BF16,
                  name="ssd_in_xbc")
    dt_raw = _matmul(h, w_in, n_out=2 * SSD_HEADS, col_off=SSD_D_INNER + SSD_CONV_CH, tm=1024, tn=LANES,
                     out_dtype=F32, name="ssd_in_dt")
    xbc_act = _conv_call(xbc, conv_w[0], conv_b[0], n_p=n_p, lp=lp, ls=ls, tc=256, tcn=2048, name="ssd_conv")
    ys, fin = [], None
    for direction in (0, 1):
        y, fin = _ssd_scan_call(xbc_act, dt_raw, dt_bias[0], a_log[0], None, None, fin, nb=bp, seq=lp,
                                row_off=0, direction=direction, name="ssd_scan_prompt%d" % direction)
        y, _ = _ssd_scan_call(xbc_act, dt_raw, dt_bias[0], a_log[0], state_ssm[:, 0], y, None, nb=bs, seq=ls,
                              row_off=n_p, direction=direction, name="ssd_scan_sample%d" % direction)
        ys.append(y)
    d_tot = jnp.repeat(d_skip[0, 0] + d_skip[0, 1], SSD_HEAD_DIM).reshape(1, SSD_D_INNER)
    yn = _ssd_post_call(ys[0], ys[1], xbc_act, z, d_tot, ssd_norm_w[0], tm=256, name="ssd_post")
    x = _matmul(yn, w_out_ssd[0].astype(BF16), n_out=d, tm=1024, tn=512, out_dtype=F32, name="ssd_out",
                resid=resid_of(x, mod, 2))
    xp, xs = _moe_layer(x, mod, n_p, ls, 1, norm2_w[1], w_router[1], b_router[1], w_gate_up, b_gate_up,
                        w_down, b_down, split=True, name="moe1")
    new_s = fin[:, None]
    return (xp.reshape(bp, lp, d), xs.reshape(bs, ls, d), new_k, new_v, new_s)
```

```python
import functools
import math

import jax
import jax.numpy as jnp
from jax import lax
from jax.experimental import pallas as pl
from jax.experimental.pallas import tpu as pltpu

F32 = jnp.float32
BF16 = jnp.bfloat16

D_MODEL = 2048
NORM_EPS = 1e-6
GRID_W = 64
ROPE_THETA = 10000.0
ATTN_HEADS = 8
QK_DIM = 128
V_DIM = 256
SSD_D_INNER = 4096
SSD_HEAD_DIM = 64
SSD_HEADS = 64
SSD_GROUPS = 8
SSD_STATE = 128
SSD_CONV = 5
SSD_CHUNK = 128
SSD_BC_DIM = SSD_GROUPS * SSD_STATE
SSD_CONV_CH = SSD_D_INNER + 2 * SSD_BC_DIM
N_EXPERTS = 32
TOP_K = 4
D_FF = 2048
SWIGLU_LIMIT = 7.0
SWIGLU_ALPHA = 1.702
MOE_BLOCK = 256
ATTN_KEY_CHUNK = 512
CAST_ROWS = 256
LOG2_E = math.log2(math.e)
LANES = 128
GROUP_W = SSD_D_INNER // SSD_GROUPS
assert SSD_CHUNK == SSD_STATE == LANES and 2 * SSD_HEAD_DIM == LANES
VMEM_LIMIT = 56 * 1024 * 1024


def _params(*sem):
    return pltpu.CompilerParams(dimension_semantics=sem, vmem_limit_bytes=VMEM_LIMIT)


def _group_of_tile(i, tm, n_p, ls):
    return jnp.where(i * tm < n_p, 0, 1 + (i * tm - n_p) // ls)


def _mod_spec(which, tm, tn, n_p, ls):
    return pl.BlockSpec((1, 1, tn), lambda i, j: (_group_of_tile(i, tm, n_p, ls) * 6 + which, 0, j))


def _mm_kernel(a_ref, w_ref, *rest, epilogue):
    acc = jnp.dot(a_ref[...].astype(BF16), w_ref[...].astype(BF16), preferred_element_type=F32)
    if epilogue == "bias":
        b_ref, o_ref = rest
        o_ref[...] = (acc + b_ref[...]).astype(o_ref.dtype)
    elif epilogue == "resid":
        x_ref, g_ref, o_ref = rest[0], rest[1], rest[-1]
        o_ref[...] = x_ref[...] + g_ref[0] * acc
    else:
        (o_ref,) = rest
        o_ref[...] = acc.astype(o_ref.dtype)


def _matmul(a, w, *, n_out, col_off=0, tm, tn, out_dtype, name, layer=None, bias=None, resid=None):
    m, k = a.shape
    joff = col_off // tn
    if layer is None:
        w_spec = pl.BlockSpec((k, tn), lambda i, j: (0, j + joff))
        b_spec = pl.BlockSpec((1, tn), lambda i, j: (0, j))
    else:
        w_spec = pl.BlockSpec((None, k, tn), lambda i, j: (layer, 0, j + joff))
        b_spec = pl.BlockSpec((None, 1, tn), lambda i, j: (layer, 0, j))
    in_specs = [pl.BlockSpec((tm, k), lambda i, j: (i, 0)), w_spec]
    args = [a, w]
    if bias is not None:
        epilogue = "bias"
        in_specs.append(b_spec)
        args.append(bias)
    elif resid is not None:
        epilogue = "resid"
        x, mod, which, n_p, ls = resid
        in_specs += [pl.BlockSpec((tm, tn), lambda i, j: (i, j)), _mod_spec(which, tm, tn, n_p, ls)]
        args += [x, mod]
    else:
        epilogue = "plain"
    return pl.pallas_call(
        functools.partial(_mm_kernel, epilogue=epilogue),
        grid=(m // tm, n_out // tn),
        in_specs=in_specs,
        out_specs=pl.BlockSpec((tm, tn), lambda i, j: (i, j)),
        out_shape=jax.ShapeDtypeStruct((m, n_out), out_dtype),
        compiler_params=_params("parallel", "arbitrary"),
        name=name,
    )(*args)


def _resid_rows_matmul(a_table, w, x_rows, mod, which, out_into, *, row_off, n_p, ls, tm, tn, name):
    n_rows, k = a_table.shape
    m, n_out = x_rows.shape
    ioff = row_off // tm
    in_specs = [pl.BlockSpec((tm, k), lambda i, j: (i + ioff, 0)),
                pl.BlockSpec((k, tn), lambda i, j: (0, j)),
                pl.BlockSpec((tm, tn), lambda i, j: (i, j)),
                pl.BlockSpec((1, 1, tn), lambda i, j: (_group_of_tile(i + ioff, tm, n_p, ls) * 6 + which, 0, j))]
    args = [a_table, w, x_rows, mod]
    aliases = {}
    if out_into is not None:
        in_specs.append(pl.BlockSpec(memory_space=pl.ANY))
        aliases = {len(args): 0}
        args.append(out_into)
    return pl.pallas_call(
        functools.partial(_mm_kernel, epilogue="resid"),
        grid=(m // tm, n_out // tn),
        in_specs=in_specs,
        out_specs=pl.BlockSpec((tm, tn), lambda i, j: (i + ioff, j)),
        out_shape=jax.ShapeDtypeStruct((n_rows, n_out), F32),
        input_output_aliases=aliases,
        compiler_params=_params("parallel", "arbitrary"),
        name=name,
    )(*args)


def _modnorm(x, w, shift, scale):
    ms = jnp.mean(x * x, axis=-1, keepdims=True)
    return (x * lax.rsqrt(ms + NORM_EPS) * w) * (1.0 + scale) + shift


def _modnorm_kernel(x_ref, w_ref, sh_ref, sc_ref, o_ref):
    o_ref[...] = _modnorm(x_ref[...], w_ref[...], sh_ref[0], sc_ref[0]).astype(o_ref.dtype)


def _modnorm_call(x, w, mod, which_shift, n_p, ls, *, tm, name):
    m, d = x.shape
    return pl.pallas_call(
        _modnorm_kernel,
        grid=(m // tm, 1),
        in_specs=[pl.BlockSpec((tm, d), lambda i, j: (i, 0)),
                  pl.BlockSpec((1, d), lambda i, j: (0, 0)),
                  _mod_spec(which_shift, tm, d, n_p, ls),
                  _mod_spec(which_shift + 1, tm, d, n_p, ls)],
        out_specs=pl.BlockSpec((tm, d), lambda i, j: (i, 0)),
        out_shape=jax.ShapeDtypeStruct((m, d), BF16),
        compiler_params=_params("parallel", "arbitrary"),
        name=name,
    )(x, w, mod, mod)


def _split2(v):
    hi = v.astype(BF16)
    lo = (v - hi.astype(F32)).astype(BF16)
    return hi, lo


def _router_kernel(x_ref, w_ref, sh_ref, sc_ref, wr_ref, br_ref, h_ref, idx_ref, gate_ref, rank_ref, cnt_ref,
                   run_ref):
    @pl.when(pl.program_id(0) == 0)
    def _():
        run_ref[...] = jnp.zeros_like(run_ref)

    h = _modnorm(x_ref[...], w_ref[...], sh_ref[0], sc_ref[0])
    h_ref[...] = h.astype(BF16)
    h_hi, h_lo = _split2(h)
    w_hi, w_lo = _split2(wr_ref[...])
    logits = (jnp.dot(h_hi, w_hi, preferred_element_type=F32)
              + jnp.dot(h_hi, w_lo, preferred_element_type=F32)
              + jnp.dot(h_lo, w_hi, preferred_element_type=F32)) + br_ref[...]
    lane = lax.broadcasted_iota(jnp.int32, logits.shape, 1).astype(F32)
    cur = jnp.where(lane < N_EXPERTS, logits, -jnp.inf)
    vals, idxs = [], []
    for _ in range(TOP_K):
        m = jnp.max(cur, axis=-1, keepdims=True)
        am = jnp.min(jnp.where(cur == m, lane, float(LANES)), axis=-1, keepdims=True)
        vals.append(m)
        idxs.append(am)
        cur = jnp.where(lane == am, -jnp.inf, cur)
    exps = [jnp.exp(v - vals[0]) for v in vals]
    denom = exps[0] + exps[1] + exps[2] + exps[3]
    idx_out = jnp.zeros(logits.shape, F32)
    gate_out = jnp.zeros(logits.shape, F32)
    for k in range(TOP_K):
        idx_out = jnp.where(lane == k, idxs[k], idx_out)
        gate_out = jnp.where(lane == k, exps[k] / denom, gate_out)
    idx_ref[...] = idx_out.astype(jnp.int32)
    gate_ref[...] = gate_out
    tm = logits.shape[0]
    chosen = functools.reduce(jnp.logical_or, [lane == am for am in idxs])
    onehot = jnp.where(chosen, 1.0, 0.0)
    ti = lax.broadcasted_iota(jnp.int32, (tm, tm), 0)
    tj = lax.broadcasted_iota(jnp.int32, (tm, tm), 1)
    earlier = jnp.where(ti > tj, 1.0, 0.0).astype(BF16)
    base = run_ref[...] + jnp.dot(earlier, onehot.astype(BF16), preferred_element_type=F32)
    rank_out = jnp.zeros(logits.shape, F32)
    for k in range(TOP_K):
        rk = jnp.sum(jnp.where(lane == idxs[k], base, 0.0), axis=-1, keepdims=True)
        rank_out = jnp.where(lane == k, rk, rank_out)
    rank_ref[...] = rank_out.astype(jnp.int32)
    run_ref[...] = run_ref[...] + jnp.sum(onehot, axis=0, keepdims=True)
    cnt_ref[...] = run_ref[...].astype(jnp.int32)


def _router_call(x, w, mod, n_p, ls, w_router, b_router, *, tm, name):
    m, d = x.shape
    wr = jnp.pad(w_router, ((0, 0), (0, LANES - N_EXPERTS)))
    br = jnp.pad(b_router, (0, LANES - N_EXPERTS)).reshape(1, LANES)
    return pl.pallas_call(
        _router_kernel,
        grid=(m // tm, 1),
        in_specs=[pl.BlockSpec((tm, d), lambda i, j: (i, 0)),
                  pl.BlockSpec((1, d), lambda i, j: (0, 0)),
                  _mod_spec(3, tm, d, n_p, ls),
                  _mod_spec(4, tm, d, n_p, ls),
                  pl.BlockSpec((d, LANES), lambda i, j: (0, 0)),
                  pl.BlockSpec((1, LANES), lambda i, j: (0, 0))],
        out_specs=[pl.BlockSpec((tm, d), lambda i, j: (i, 0)),
                   pl.BlockSpec((tm, LANES), lambda i, j: (i, 0)),
                   pl.BlockSpec((tm, LANES), lambda i, j: (i, 0)),
                   pl.BlockSpec((tm, LANES), lambda i, j: (i, 0)),
                   pl.BlockSpec((1, LANES), lambda i, j: (0, 0))],
        out_shape=[jax.ShapeDtypeStruct((m, d), BF16),
                   jax.ShapeDtypeStruct((m, LANES), jnp.int32),
                   jax.ShapeDtypeStruct((m, LANES), F32),
                   jax.ShapeDtypeStruct((m, LANES), jnp.int32),
                   jax.ShapeDtypeStruct((1, LANES), jnp.int32)],
        scratch_shapes=[pltpu.VMEM((1, LANES), F32)],
        compiler_params=_params("arbitrary", "arbitrary"),
        name=name,
    )(x, w, mod, mod, wr, br)


def _qkv_kernel(x_ref, n1_ref, sh_ref, sc_ref, w_ref, nw_ref, *rest, rope, tn):
    if rope:
        cos_ref, sin_ref, q_ref, k_ref, v_ref, h_s = rest
    else:
        q_ref, k_ref, v_ref, h_s = rest
    j = pl.program_id(1)
    nq = D_MODEL // tn

    @pl.when(j == 0)
    def _():
        h_s[...] = _modnorm(x_ref[...], n1_ref[...], sh_ref[0], sc_ref[0]).astype(BF16)

    acc = jnp.dot(h_s[...], w_ref[...], preferred_element_type=F32)

    def normed(widx, out_scale):
        w = nw_ref[widx:widx + 1, :]
        outs = []
        for c in range(tn // QK_DIM):
            xc = acc[:, c * QK_DIM:(c + 1) * QK_DIM]
            ms = jnp.mean(xc * xc, axis=-1, keepdims=True)
            y = xc * lax.rsqrt(ms + NORM_EPS) * w
            if rope:
                lane = lax.broadcasted_iota(jnp.int32, y.shape, 1)
                partner = jnp.where(lane % 64 < 32, pltpu.roll(y, QK_DIM - 32, 1), pltpu.roll(y, 32, 1))
                y = y * cos_ref[...] + partner * sin_ref[...]
            outs.append(y * out_scale if out_scale != 1.0 else y)
        return jnp.concatenate(outs, axis=1)

    @pl.when(j < nq)
    def _():
        q_ref[...] = normed(0, QK_DIM ** -0.5 * LOG2_E).astype(q_ref.dtype)

    @pl.when((j >= nq) & (j < 2 * nq))
    def _():
        k_ref[...] = normed(1, 1.0).astype(k_ref.dtype)

    @pl.when(j >= 2 * nq)
    def _():
        v_ref[...] = acc.astype(v_ref.dtype)


def _qkv_call(x_rows, norm1_w, mod, w_qkv, qk_norm_w, rope_tabs, *, row_off, n_p, kv_dtype, tm, tn, ls, name):
    m, d = x_rows.shape
    nq = d // tn
    ioff = row_off // tm
    rope = rope_tabs is not None

    def mod_spec(which):
        return pl.BlockSpec((1, 1, d), lambda i, j: (_group_of_tile(i + ioff, tm, n_p, ls) * 6 + which, 0, 0))

    in_specs = [pl.BlockSpec((tm, d), lambda i, j: (i, 0)),
                pl.BlockSpec((1, d), lambda i, j: (0, 0)),
                mod_spec(0), mod_spec(1),
                pl.BlockSpec((d, tn), lambda i, j: (0, j)),
                pl.BlockSpec((2, QK_DIM), lambda i, j: (0, 0))]
    args = [x_rows, norm1_w.reshape(1, d), mod, mod, w_qkv, qk_norm_w]
    if rope:
        nt = ls // tm
        in_specs += [pl.BlockSpec((tm, QK_DIM), lambda i, j: (i % nt, 0))] * 2
        args += list(rope_tabs)
    return pl.pallas_call(
        functools.partial(_qkv_kernel, rope=rope, tn=tn),
        grid=(m // tm, 3 * nq),
        in_specs=in_specs,
        out_specs=[pl.BlockSpec((tm, tn), lambda i, j: (i, jnp.minimum(j, nq - 1))),
                   pl.BlockSpec((tm, tn), lambda i, j: (i, jnp.clip(j - nq, 0, nq - 1))),
                   pl.BlockSpec((tm, tn), lambda i, j: (i, jnp.clip(j - 2 * nq, 0, nq - 1)))],
        out_shape=[jax.ShapeDtypeStruct((m, d), BF16),
                   jax.ShapeDtypeStruct((m, d), kv_dtype),
                   jax.ShapeDtypeStruct((m, d), kv_dtype)],
        scratch_shapes=[pltpu.VMEM((tm, d), BF16)],
        compiler_params=_params("parallel", "arbitrary"),
        name=name,
    )(*args)


def _rope_tables(ls):
    rows = ls // GRID_W
    row = jnp.repeat(jnp.arange(rows, dtype=F32), GRID_W)
    col = jnp.tile(jnp.arange(GRID_W, dtype=F32), rows)
    axis_dim = QK_DIM // 2
    inv = ROPE_THETA ** (-jnp.arange(0, axis_dim, 2, dtype=F32) / axis_dim)
    ar, ac = row[:, None] * inv, col[:, None] * inv
    cos = jnp.concatenate([jnp.cos(ar), jnp.cos(ar), jnp.cos(ac), jnp.cos(ac)], axis=1)
    sin = jnp.concatenate([-jnp.sin(ar), jnp.sin(ar), -jnp.sin(ac), jnp.sin(ac)], axis=1)
    return cos, sin


def _attn_kernel(lam_ref, subw_ref, q_ref, *rest, nseg, lam_init):
    k_refs, v_refs, o_ref = rest[:nseg], rest[nseg:2 * nseg], rest[-1]
    lp = lam_ref[...]
    lam = (jnp.exp(jnp.sum(lp[0:1] * lp[1:2], axis=-1, keepdims=True))
           - jnp.exp(jnp.sum(lp[2:3] * lp[3:4], axis=-1, keepdims=True)) + lam_init)
    q = q_ref[...]
    chunks = [(k_ref, v_ref, c0) for k_ref, v_ref in zip(k_refs, v_refs)
              for c0 in range(0, k_ref.shape[0], min(ATTN_KEY_CHUNK, k_ref.shape[0]))]
    outs = []
    for mi in range(2):
        qm = q[:, mi * QK_DIM:(mi + 1) * QK_DIM]
        mx = den = pv = None
        for k_ref, v_ref, c0 in chunks:
            kc = min(ATTN_KEY_CHUNK, k_ref.shape[0])
            k = k_ref[c0:c0 + kc, mi * QK_DIM:(mi + 1) * QK_DIM].astype(BF16)
            v = v_ref[c0:c0 + kc, :].astype(BF16)
            s = lax.dot_general(qm, k, (((1,), (1,)), ((), ())), preferred_element_type=F32)
            cmax = jnp.max(s, axis=-1, keepdims=True)
            if mx is None:
                mx = cmax
                p = jnp.exp2(s - mx)
                den = jnp.sum(p, axis=-1, keepdims=True)
                pv = jnp.dot(p.astype(BF16), v, preferred_element_type=F32)
            else:
                new_mx = jnp.maximum(mx, cmax)
                alpha = jnp.exp2(mx - new_mx)
                p = jnp.exp2(s - new_mx)
                den = den * alpha + jnp.sum(p, axis=-1, keepdims=True)
                pv = pv * alpha + jnp.dot(p.astype(BF16), v, preferred_element_type=F32)
                mx = new_mx
        outs.append(pv * (1.0 / den))
    acc = outs[0] - lam * outs[1]
    ms = jnp.mean(acc * acc, axis=-1, keepdims=True)
    o = acc * lax.rsqrt(ms + NORM_EPS) * subw_ref[...] * (1.0 - lam_init)
    o_ref[...] = o.astype(o_ref.dtype)


def _attn_call(q, kv_segs, lam_params, subln_w, o_into, *, n_rows, row_off, nb, lq, tq, lam_init, name):
    nqb = lq // tq
    nseg = len(kv_segs)
    ooff = row_off // tq
    in_specs = [pl.BlockSpec((4, QK_DIM), lambda b, h, i: (0, 0)),
                pl.BlockSpec((1, V_DIM), lambda b, h, i: (0, 0)),
                pl.BlockSpec((tq, V_DIM), lambda b, h, i: (b * nqb + i, h))]
    in_specs += [pl.BlockSpec((lk, V_DIM), lambda b, h, i: (b, h)) for (_, _, lk) in kv_segs] * 2
    args = [lam_params, subln_w.reshape(1, V_DIM), q]
    args += [k for (k, _, _) in kv_segs] + [v for (_, v, _) in kv_segs]
    aliases = {}
    if o_into is not None:
        in_specs.append(pl.BlockSpec(memory_space=pl.ANY))
        aliases = {len(args): 0}
        args.append(o_into)
    return pl.pallas_call(
        functools.partial(_attn_kernel, nseg=nseg, lam_init=lam_init),
        grid=(nb, ATTN_HEADS, nqb),
        in_specs=in_specs,
        out_specs=pl.BlockSpec((tq, V_DIM), lambda b, h, i: (ooff + b * nqb + i, h)),
        out_shape=jax.ShapeDtypeStruct((n_rows, q.shape[1]), BF16),
        input_output_aliases=aliases,
        compiler_params=_params("parallel", "parallel", "arbitrary"),
        name=name,
    )(*args)


def _conv_kernel(xp_ref, x_ref, xn_ref, w_ref, b_ref, o_ref, ext_ref, *, tc, n_p, lp, ls, halo):
    i = pl.program_id(0)
    row0 = i * tc
    in_prompt = row0 < n_p
    first = jnp.where(in_prompt, row0 % lp == 0, (row0 - n_p) % ls == 0)
    last = jnp.where(in_prompt, (row0 + tc) % lp == 0, (row0 + tc - n_p) % ls == 0)
    prev = xp_ref[...].astype(F32)[halo - 8:halo, :]
    nxt = xn_ref[...].astype(F32)[0:8, :]
    ext_ref[0:8, :] = jnp.where(first, 0.0, prev)
    ext_ref[8:8 + tc, :] = x_ref[...].astype(F32)
    ext_ref[8 + tc:16 + tc, :] = jnp.where(last, 0.0, nxt)
    acc = b_ref[...] + w_ref[0:1, :] * ext_ref[6:6 + tc, :]
    for k in range(1, SSD_CONV):
        acc = acc + w_ref[k:k + 1, :] * ext_ref[6 + k:6 + k + tc, :]
    o_ref[...] = (acc / (1.0 + jnp.exp(-acc))).astype(o_ref.dtype)


def _conv_call(xbc, conv_w, conv_b, *, n_p, lp, ls, tc, tcn, name):
    m, ch = xbc.shape
    halo = 16
    hb = tc // halo
    nhb = m // halo
    return pl.pallas_call(
        functools.partial(_conv_kernel, tc=tc, n_p=n_p, lp=lp, ls=ls, halo=halo),
        grid=(m // tc, ch // tcn),
        in_specs=[pl.BlockSpec((halo, tcn), lambda i, j: (jnp.maximum(i * hb - 1, 0), j)),
                  pl.BlockSpec((tc, tcn), lambda i, j: (i, j)),
                  pl.BlockSpec((halo, tcn), lambda i, j: (jnp.minimum((i + 1) * hb, nhb - 1), j)),
                  pl.BlockSpec((SSD_CONV, tcn), lambda i, j: (0, j)),
                  pl.BlockSpec((1, tcn), lambda i, j: (0, j))],
        out_specs=pl.BlockSpec((tc, tcn), lambda i, j: (i, j)),
        out_shape=jax.ShapeDtypeStruct((m, ch), BF16),
        scratch_shapes=[pltpu.VMEM((tc + 16, tcn), F32)],
        compiler_params=_params("parallel", "arbitrary"),
        name=name,
    )(xbc, xbc, xbc, conv_w, conv_b.reshape(1, ch))


def _ssd_kernel(xbc_ref, dtraw_ref, dtb_ref, alog_ref, tri_ref, *rest, direction, zero_init, nt):
    y_ref, fin_ref, st_ref = rest[-3:]
    if not zero_init:
        h0_ref = rest[0]
    t = pl.program_id(1)
    q = SSD_CHUNK
    hpg = SSD_HEADS // SSD_GROUPS

    @pl.when(t == 0)
    def _():
        for g in range(SSD_GROUPS):
            if zero_init:
                st_ref[g] = jnp.zeros((SSD_STATE, GROUP_W), F32)
            else:
                hg = h0_ref[0, 0, g * hpg:(g + 1) * hpg].reshape(GROUP_W, SSD_STATE)
                st_ref[g] = hg.T

    x = dtraw_ref[...] + dtb_ref[...]
    dt = jnp.maximum(x, 0.0) + jnp.log1p(jnp.exp(-jnp.abs(x)))
    a = dt * (-jnp.exp(alog_ref[...]) * LOG2_E)
    tri = tri_ref[...]
    a1 = a.astype(BF16)
    r1 = a - a1.astype(F32)
    a2 = r1.astype(BF16)
    a3 = (r1 - a2.astype(F32)).astype(BF16)
    cs = (jnp.dot(tri, a1, preferred_element_type=F32) + jnp.dot(tri, a2, preferred_element_type=F32)
          + jnp.dot(tri, a3, preferred_element_type=F32))
    tot = cs[q - 1:q, :] if direction == 0 else cs[0:1, :]
    cdec = jnp.exp2(tot)
    cs_t = cs.T
    dt_t = dt.T
    wend_t = (dt * jnp.exp2(tot - cs)).T
    li = lax.broadcasted_iota(jnp.int32, (q, q), 0)
    si = lax.broadcasted_iota(jnp.int32, (q, q), 1)
    mask = (li >= si) if direction == 0 else (li <= si)
    low = lax.broadcasted_iota(jnp.int32, (q, LANES), 1) < SSD_HEAD_DIM
    low_row = low[0:1, :]

    for g in range(SSD_GROUPS):
        bg = xbc_ref[:, SSD_D_INNER + g * SSD_STATE:SSD_D_INNER + (g + 1) * SSD_STATE]
        cg = xbc_ref[:, SSD_D_INNER + SSD_BC_DIM + g * SSD_STATE:SSD_D_INNER + SSD_BC_DIM + (g + 1) * SSD_STATE]
        cg32 = cg.astype(F32)
        bg_t = bg.astype(F32).T
        cb = lax.dot_general(cg, bg, (((1,), (1,)), ((), ())), preferred_element_type=F32)
        for p in range(hpg // 2):
            col0 = g * GROUP_W + p * LANES
            xp = xbc_ref[:, col0:col0 + LANES]
            zero = jnp.zeros_like(xp)
            x_lo = jnp.where(low, xp, zero)
            x_hi = jnp.where(low, zero, xp)
            s_in = st_ref[g, :, p * LANES:(p + 1) * LANES]
            s_bf = s_in.astype(BF16)
            s_lo = jnp.where(low, s_bf, zero)
            s_hi = jnp.where(low, zero, s_bf)
            c0 = direction * SSD_HEADS + g * hpg + 2 * p
            diag, offd, bw = [], [], []
            for c in (c0, c0 + 1):
                col = jnp.broadcast_to(cs[:, c:c + 1], (q, q))
                seg = col - cs_t[c:c + 1, :]
                diag.append((cb * jnp.exp2(jnp.where(mask, seg, -jnp.inf)) * dt_t[c:c + 1, :]).astype(BF16))
                offd.append((cg32 * jnp.exp2(col)).astype(BF16))
                bw.append((bg_t * wend_t[c:c + 1, :]).astype(BF16))
            lhs = jnp.concatenate(diag + offd, axis=1)
            rhs = jnp.concatenate([x_lo, x_hi, s_lo, s_hi], axis=0)
            y_ref[:, col0:col0 + LANES] = jnp.dot(lhs, rhs, preferred_element_type=F32).astype(y_ref.dtype)
            new = jnp.dot(jnp.concatenate(bw, axis=1), jnp.concatenate([x_lo, x_hi], axis=0),
                          preferred_element_type=F32)
            keep = jnp.where(low_row, cdec[:, c0:c0 + 1], cdec[:, c0 + 1:c0 + 2])
            st_ref[g, :, p * LANES:(p + 1) * LANES] = s_in * keep + new

    @pl.when(t == nt - 1)
    def _():
        for g in range(SSD_GROUPS):
            fin_ref[0, g * hpg:(g + 1) * hpg] = st_ref[g].T.reshape(hpg, SSD_HEAD_DIM, SSD_STATE)


def _ssd_scan_call(xbc_act, dt_raw, dt_bias, a_log, h0, y_into, fin_into, *, nb, seq, row_off, direction,
                   name):
    q = SSD_CHUNK
    nt = seq // q
    boff = row_off // q
    li = jnp.arange(q)[:, None]
    si = jnp.arange(q)[None, :]
    tri = ((li >= si) if direction == 0 else (li <= si)).astype(BF16)

    def chunk(b, t):
        return boff + b * nt + (t if direction == 0 else nt - 1 - t)

    zero_init = h0 is None
    in_specs = [pl.BlockSpec((q, SSD_CONV_CH), lambda b, t: (chunk(b, t), 0)),
                pl.BlockSpec((q, LANES), lambda b, t: (chunk(b, t), 0)),
                pl.BlockSpec((1, LANES), lambda b, t: (0, 0)),
                pl.BlockSpec((1, LANES), lambda b, t: (0, 0)),
                pl.BlockSpec((q, q), lambda b, t: (0, 0))]
    args = [xbc_act, dt_raw, dt_bias.reshape(1, LANES), a_log.reshape(1, LANES), tri]
    if not zero_init:
        in_specs.append(pl.BlockSpec((1, 1, SSD_HEADS, SSD_HEAD_DIM, SSD_STATE),
                                     lambda b, t: (b, direction, 0, 0, 0)))
        args.append(h0)
    aliases = {}
    if y_into is not None:
        in_specs.append(pl.BlockSpec(memory_space=pl.ANY))
        aliases = {len(args): 0}
        args.append(y_into)
    if fin_into is not None:
        in_specs.append(pl.BlockSpec(memory_space=pl.ANY))
        aliases[len(args)] = 1
        args.append(fin_into)
    return pl.pallas_call(
        functools.partial(_ssd_kernel, direction=direction, zero_init=zero_init, nt=nt),
        grid=(nb, nt),
        in_specs=in_specs,
        out_specs=[pl.BlockSpec((q, SSD_D_INNER), lambda b, t: (chunk(b, t), 0)),
                   pl.BlockSpec((1, None, SSD_HEADS, SSD_HEAD_DIM, SSD_STATE),
                                lambda b, t: (b, direction, 0, 0, 0))],
        out_shape=[jax.ShapeDtypeStruct((xbc_act.shape[0], SSD_D_INNER), BF16),
                   jax.ShapeDtypeStruct((nb, 2, SSD_HEADS, SSD_HEAD_DIM, SSD_STATE), F32)],
        scratch_shapes=[pltpu.VMEM((SSD_GROUPS, SSD_STATE, GROUP_W), F32)],
        input_output_aliases=aliases,
        compiler_params=_params("parallel", "arbitrary"),
        name=name,
    )(*args)


def _ssd_post_kernel(yf_ref, yb_ref, xs_ref, z_ref, d_ref, nw_ref, o_ref):
    z = z_ref[...].astype(F32)
    y = yf_ref[...].astype(F32) + yb_ref[...].astype(F32) + d_ref[...] * xs_ref[...].astype(F32)
    y = y * (z / (1.0 + jnp.exp(-z)))
    outs = []
    for g in range(SSD_GROUPS):
        yg = y[:, g * GROUP_W:(g + 1) * GROUP_W]
        ms = jnp.mean(yg * yg, axis=-1, keepdims=True)
        outs.append(yg * lax.rsqrt(ms + NORM_EPS) * nw_ref[:, g * GROUP_W:(g + 1) * GROUP_W])
    o_ref[...] = jnp.concatenate(outs, axis=1).astype(o_ref.dtype)


def _ssd_post_call(y_f, y_b, xbc_act, z, d_tot, norm_w, *, tm, name):
    m = y_f.shape[0]
    di = SSD_D_INNER
    row = pl.BlockSpec((tm, di), lambda i: (i, 0))
    vec = pl.BlockSpec((1, di), lambda i: (0, 0))
    return pl.pallas_call(
        _ssd_post_kernel,
        grid=(m // tm,),
        in_specs=[row, row, row, row, vec, vec],
        out_specs=row,
        out_shape=jax.ShapeDtypeStruct((m, di), BF16),
        compiler_params=_params("parallel"),
        name=name,
    )(y_f, y_b, xbc_act, z, d_tot, norm_w.reshape(1, di))


def _run_schedule(block_e, n_used):
    nblk = block_e.shape[0]
    idx = jnp.arange(nblk, dtype=jnp.int32)
    prev = jnp.concatenate([block_e[:1] - 1, block_e[:-1]])
    first = ((idx < n_used[0]) & (block_e != prev)).astype(jnp.int32)
    ridx = jnp.cumsum(first) - 1
    is_next = (first[None, :] == 1) & (ridx[None, :] == ridx[:, None] + 1)
    next_e = jnp.where(jnp.any(is_next, axis=1), jnp.sum(jnp.where(is_next, block_e[None, :], 0), axis=1), -1)
    counts = jnp.stack([n_used[0], jnp.sum(first)])
    return (block_e, counts.astype(jnp.int32), first, ridx.astype(jnp.int32), next_e.astype(jnp.int32))


def _weight_copies(w_hbm, stage, sems, layer, e, jj, slot, col_tiles, tn):
    return [pltpu.make_async_copy(w_hbm.at[layer, e, :, pl.ds(pl.multiple_of((off + jj) * tn, tn), tn)],
                                  stage.at[slot, wi], sems.at[slot, wi])
            for wi, off in enumerate(col_tiles)]


def _stage_run_weights(sched, w_hbm, stage, sems, w_bf, *, layer, col_tiles, tn, nj):
    be_ref, cnt_ref, first_ref, ridx_ref, next_ref = sched
    j = pl.program_id(0)
    i = pl.program_id(1)
    copies = functools.partial(_weight_copies, w_hbm, stage, sems, layer, col_tiles=col_tiles, tn=tn)

    @pl.when(first_ref[i] == 1)
    def _():
        slot = (j * cnt_ref[1] + ridx_ref[i]) & 1
        e = be_ref[i]

        @pl.when((i == 0) & (j == 0))
        def _():
            for c in copies(e, j, slot):
                c.start()

        for c in copies(e, j, slot):
            c.wait()
        def cast_rows(r, carry):
            rows = pl.ds(pl.multiple_of(r * CAST_ROWS, CAST_ROWS), CAST_ROWS)
            for wi in range(len(col_tiles)):
                w_bf[wi, rows, :] = stage[slot, wi, rows, :].astype(BF16)
            return carry

        lax.fori_loop(0, w_bf.shape[1] // CAST_ROWS, cast_rows, 0)
        nxt = next_ref[i]

        @pl.when(nxt >= 0)
        def _():
            for c in copies(nxt, j, 1 - slot):
                c.start()

        @pl.when((nxt < 0) & (j + 1 < nj))
        def _():
            for c in copies(be_ref[0], j + 1, 1 - slot):
                c.start()


def _moe_up_kernel(*refs, layer, tn, nj):
    sched, (x_ref, w_hbm, bg_ref, bu_ref, o_ref, stage, w_bf, sems) = refs[:5], refs[5:]
    i = pl.program_id(1)
    _stage_run_weights(sched, w_hbm, stage, sems, w_bf, layer=layer, col_tiles=(0, nj), tn=tn, nj=nj)
    nu_ref = sched[1]

    @pl.when(i < nu_ref[0])
    def _():
        x = x_ref[...]
        g = jnp.dot(x, w_bf[0], preferred_element_type=F32) + bg_ref[...]
        u = jnp.dot(x, w_bf[1], preferred_element_type=F32) + bu_ref[...]
        g = jnp.minimum(g, SWIGLU_LIMIT)
        u = jnp.clip(u, -SWIGLU_LIMIT, SWIGLU_LIMIT)
        act = (u + 1.0) * (g / (1.0 + jnp.exp2((-SWIGLU_ALPHA * LOG2_E) * g)))
        o_ref[...] = act.astype(o_ref.dtype)

    @pl.when(i >= nu_ref[0])
    def _():
        o_ref[...] = jnp.zeros_like(o_ref)


def _moe_down_kernel(*refs, layer, tn, nj):
    sched, (a_ref, w_hbm, b_ref, o_ref, stage, w_bf, sems) = refs[:5], refs[5:]
    i = pl.program_id(1)
    _stage_run_weights(sched, w_hbm, stage, sems, w_bf, layer=layer, col_tiles=(0,), tn=tn, nj=nj)
    nu_ref = sched[1]

    @pl.when(i < nu_ref[0])
    def _():
        y = jnp.dot(a_ref[...], w_bf[0], preferred_element_type=F32) + b_ref[...]
        o_ref[...] = y.astype(o_ref.dtype)

    @pl.when(i >= nu_ref[0])
    def _():
        o_ref[...] = jnp.zeros_like(o_ref)


def _moe_experts(x_sorted, block_e, n_used, layer, w_gate_up, b_gate_up, w_down, b_down, *, tn_up, tn_down,
                 name):
    slots, d = x_sorted.shape
    nblk = slots // MOE_BLOCK
    nj = D_FF // tn_up
    nl = w_gate_up.shape[0]
    b_gu = b_gate_up.reshape(nl, N_EXPERTS, 1, 2 * D_FF)
    sched = _run_schedule(block_e, n_used)
    act = pl.pallas_call(
        functools.partial(_moe_up_kernel, layer=layer, tn=tn_up, nj=nj),
        grid_spec=pltpu.PrefetchScalarGridSpec(
            num_scalar_prefetch=len(sched),
            grid=(nj, nblk),
            in_specs=[pl.BlockSpec((MOE_BLOCK, d), lambda j, i, *s: (i, 0)),
                      pl.BlockSpec(memory_space=pl.ANY),
                      pl.BlockSpec((None, None, 1, tn_up), lambda j, i, be, *s: (layer, be[i], 0, j)),
                      pl.BlockSpec((None, None, 1, tn_up), lambda j, i, be, *s: (layer, be[i], 0, nj + j))],
            out_specs=pl.BlockSpec((MOE_BLOCK, tn_up), lambda j, i, *s: (i, j)),
            scratch_shapes=[pltpu.VMEM((2, 2, d, tn_up), F32), pltpu.VMEM((2, d, tn_up), BF16),
                            pltpu.SemaphoreType.DMA((2, 2))]),
        out_shape=jax.ShapeDtypeStruct((slots, D_FF), BF16),
        compiler_params=_params("arbitrary", "arbitrary"),
        name=name + "_up",
    )(*sched, x_sorted, w_gate_up, b_gu, b_gu)
    nj2 = d // tn_down
    return pl.pallas_call(
        functools.partial(_moe_down_kernel, layer=layer, tn=tn_down, nj=nj2),
        grid_spec=pltpu.PrefetchScalarGridSpec(
            num_scalar_prefetch=len(sched),
            grid=(nj2, nblk),
            in_specs=[pl.BlockSpec((MOE_BLOCK, D_FF), lambda j, i, *s: (i, 0)),
                      pl.BlockSpec(memory_space=pl.ANY),
                      pl.BlockSpec((None, None, 1, tn_down), lambda j, i, be, *s: (layer, be[i], 0, j))],
            out_specs=pl.BlockSpec((MOE_BLOCK, tn_down), lambda j, i, *s: (i, j)),
            scratch_shapes=[pltpu.VMEM((2, 1, D_FF, tn_down), F32), pltpu.VMEM((1, D_FF, tn_down), BF16),
                            pltpu.SemaphoreType.DMA((2, 1))]),
        out_shape=jax.ShapeDtypeStruct((slots, d), BF16),
        compiler_params=_params("arbitrary", "arbitrary"),
        name=name + "_down",
    )(*sched, act, w_down, b_down.reshape(nl, N_EXPERTS, 1, d))


def _combine_kernel(*refs, split_tile):
    y_refs = refs[:TOP_K]
    gate_ref, x_ref, g2_ref = refs[TOP_K:TOP_K + 3]
    o_refs = refs[TOP_K + 3:]
    gates = gate_ref[...]
    f = None
    for k in range(TOP_K):
        term = gates[:, k:k + 1] * y_refs[k][...].astype(F32)
        f = term if f is None else f + term
    out = x_ref[...] + g2_ref[0] * f
    if split_tile is None:
        o_refs[0][...] = out
    else:
        i = pl.program_id(0)

        @pl.when(i < split_tile)
        def _():
            o_refs[0][...] = out

        @pl.when(i >= split_tile)
        def _():
            o_refs[1][...] = out


def _combine_call(y4, gate_pad, x, mod, n_p, ls, *, tm, split, name):
    n, d = x.shape
    st = n_p // tm
    if split:
        out_specs = [pl.BlockSpec((tm, d), lambda i, j: (jnp.minimum(i, st - 1), 0)),
                     pl.BlockSpec((tm, d), lambda i, j: (jnp.maximum(i - st, 0), 0))]
        out_shape = [jax.ShapeDtypeStruct((n_p, d), F32), jax.ShapeDtypeStruct((n - n_p, d), F32)]
    else:
        out_specs = pl.BlockSpec((tm, d), lambda i, j: (i, 0))
        out_shape = jax.ShapeDtypeStruct((n, d), F32)
    nt = n // tm
    y_specs = [pl.BlockSpec((tm, d), functools.partial(lambda i, j, k: (k * nt + i, 0), k=k))
               for k in range(TOP_K)]
    return pl.pallas_call(
        functools.partial(_combine_kernel, split_tile=st if split else None),
        grid=(nt, 1),
        in_specs=y_specs + [pl.BlockSpec((tm, LANES), lambda i, j: (i, 0)),
                            pl.BlockSpec((tm, d), lambda i, j: (i, 0)),
                            _mod_spec(5, tm, d, n_p, ls)],
        out_specs=out_specs,
        out_shape=out_shape,
        compiler_params=_params("arbitrary", "arbitrary"),
        name=name,
    )(*([y4] * TOP_K), gate_pad, x, mod)


def _moe_layer(x, mod, n_p, ls, layer, norm_w, w_router, b_router, w_gate_up, b_gate_up, w_down, b_down, *,
               split, name):
    n, d = x.shape
    h, idx_pad, gate_pad, rank_pad, cnt = _router_call(x, norm_w.reshape(1, d), mod, n_p, ls, w_router,
                                                       b_router, tm=256, name=name + "_router")
    n_assign = n * TOP_K
    flat_e = idx_pad[:, :TOP_K].reshape(-1)
    counts = cnt[0, :N_EXPERTS]
    padded = (counts + MOE_BLOCK - 1) // MOE_BLOCK * MOE_BLOCK
    pad_end = jnp.cumsum(padded)
    pad_start = pad_end - padded
    dest = pad_start[flat_e] + rank_pad[:, :TOP_K].reshape(-1)
    n_blocks = -(-n_assign // MOE_BLOCK) + N_EXPERTS
    slots = n_blocks * MOE_BLOCK
    slot_tok = (jnp.arange(slots, dtype=jnp.int32) % n).at[dest].set(
        jnp.arange(n_assign, dtype=jnp.int32) // TOP_K, unique_indices=True, mode="promise_in_bounds")
    block_start = jnp.arange(n_blocks, dtype=jnp.int32) * MOE_BLOCK
    block_e = jnp.minimum(jnp.sum((pad_end[None, :] <= block_start[:, None]).astype(jnp.int32), axis=1),
                          N_EXPERTS - 1)
    n_used = (pad_end[-1] // MOE_BLOCK).astype(jnp.int32).reshape(1)
    x_sorted = h.at[slot_tok].get(mode="promise_in_bounds")
    y_sorted = _moe_experts(x_sorted, block_e, n_used, layer, w_gate_up, b_gate_up, w_down, b_down,
                            tn_up=1024, tn_down=2048, name=name)
    dest_by_choice = dest.reshape(n, TOP_K).T.reshape(-1)
    y4 = y_sorted.at[dest_by_choice].get(mode="promise_in_bounds")
    return _combine_call(y4, gate_pad, x, mod, n_p, ls, tm=256, split=split, name=name + "_combine")


def _ada_call(cond, w_ada, b_ada, layer, *, name):
    g = cond.shape[0]
    a = jnp.pad(jax.nn.silu(cond), ((0, 16 - g), (0, 0))).astype(BF16)
    m = _matmul(a, w_ada, n_out=6 * D_MODEL, tm=16, tn=1024, out_dtype=F32, name=name, layer=layer,
                bias=b_ada.reshape(b_ada.shape[0], 1, 6 * D_MODEL))
    return m[:g].reshape(g * 6, 1, D_MODEL)


def kernel(x_prompt, x_sample, c, c_ctx, cache_k, cache_v, state_ssm, norm1_w, norm2_w, w_ada, b_ada, w_qkv, q_norm_w, k_norm_w, lambda_q1, lambda_k1, lambda_q2, lambda_k2, subln_w, w_o, w_in_ssd, conv_w, conv_b, dt_bias, a_log, d_skip, ssd_norm_w, w_out_ssd, w_router, b_router, w_gate_up, b_gate_up, w_down, b_down):
    bp, lp, d = x_prompt.shape
    bs, ls, _ = x_sample.shape
    past = cache_k.shape[2]
    n_p, n_s = bp * lp, bs * ls
    xp0, xs0 = x_prompt.reshape(n_p, d), x_sample.reshape(n_s, d)
    cond = jnp.concatenate([c_ctx[None], c], axis=0)
    resid_of = lambda xx, mod, which: (xx, mod, which, n_p, ls)

    n = n_p + n_s
    mod = _ada_call(cond, w_ada, b_ada, 0, name="ada0")
    wq = w_qkv[0].astype(BF16)
    qk_w = jnp.stack([q_norm_w[0], k_norm_w[0]], axis=0)
    lam_init = 0.8 - 0.6 * math.exp(-0.3 * 0)
    lam_params = jnp.stack([lambda_q1[0], lambda_k1[0], lambda_q2[0], lambda_k2[0]], axis=0)
    qp, kp, vp = _qkv_call(xp0, norm1_w[0], mod, wq, qk_w, None, row_off=0, n_p=n_p, kv_dtype=F32, tm=512,
                           tn=512, ls=ls, name="qkv_prompt")
    qs, ks, vs = _qkv_call(xs0, norm1_w[0], mod, wq, qk_w, _rope_tables(ls), row_off=n_p, n_p=n_p,
                           kv_dtype=BF16, tm=512, tn=512, ls=ls, name="qkv_sample")
    o = _attn_call(qp, [(kp, vp, lp)], lam_params, subln_w[0], None, n_rows=n, row_off=0, nb=bp, lq=lp,
                   tq=lp, lam_init=lam_init, name="attn_prompt")
    ck = cache_k[:, 0].reshape(bs * past, d)
    cv = cache_v[:, 0].reshape(bs * past, d)
    o = _attn_call(qs, [(ck, cv, past), (ks, vs, ls)], lam_params, subln_w[0], o, n_rows=n, row_off=n_p,
                   nb=bs, lq=ls, tq=256, lam_init=lam_init, name="attn_sample")
    wo = w_o[0].astype(BF16)
    x = _resid_rows_matmul(o, wo, xp0, mod, 2, None, row_off=0, n_p=n_p, ls=ls, tm=1024, tn=512,
                           name="attn_out_prompt")
    x = _resid_rows_matmul(o, wo, xs0, mod, 2, x, row_off=n_p, n_p=n_p, ls=ls, tm=1024, tn=512,
                           name="attn_out_sample")
    x = _moe_layer(x, mod, n_p, ls, 0, norm2_w[0], w_router[0], b_router[0], w_gate_up, b_gate_up,
                   w_down, b_down, split=False, name="moe0")
    new_k = kp.reshape(bp, 1, lp, ATTN_HEADS, 2, QK_DIM)
    new_v = vp.reshape(bp, 1, lp, ATTN_HEADS, V_DIM)

    mod = _ada_call(cond, w_ada, b_ada, 1, name="ada1")
    h = _modnorm_call(x, norm1_w[1].reshape(1, d), mod, 0, n_p, ls, tm=256, name="norm1_1")
    w_in = w_in_ssd[0].astype(BF16)
    z = _matmul(h, w_in, n_out=SSD_D_INNER, tm=1024, tn=512, out_dtype=BF16, name="ssd_in_z")
    xbc = _matmul(h, w_in, n_out=SSD_CONV_CH, col_off=SSD_D_INNER, tm=1024, tn=512, out_dtype=BF16,
                  name="ssd_in_xbc")
    dt_raw = _matmul(h, w_in, n_out=2 * SSD_HEADS, col_off=SSD_D_INNER + SSD_CONV_CH, tm=1024, tn=LANES,
                     out_dtype=F32, name="ssd_in_dt")
    xbc_act = _conv_call(xbc, conv_w[0], conv_b[0], n_p=n_p, lp=lp, ls=ls, tc=256, tcn=2048, name="ssd_conv")
    ys, fin = [], None
    for direction in (0, 1):
        y, fin = _ssd_scan_call(xbc_act, dt_raw, dt_bias[0], a_log[0], None, None, fin, nb=bp, seq=lp,
                                row_off=0, direction=direction, name="ssd_scan_prompt%d" % direction)
        y, _ = _ssd_scan_call(xbc_act, dt_raw, dt_bias[0], a_log[0], state_ssm[:, 0], y, None, nb=bs, seq=ls,
                              row_off=n_p, direction=direction, name="ssd_scan_sample%d" % direction)
        ys.append(y)
    d_tot = jnp.repeat(d_skip[0, 0] + d_skip[0, 1], SSD_HEAD_DIM).reshape(1, SSD_D_INNER)
    yn = _ssd_post_call(ys[0], ys[1], xbc_act, z, d_tot, ssd_norm_w[0], tm=256, name="ssd_post")
    x = _matmul(yn, w_out_ssd[0].astype(BF16), n_out=d, tm=1024, tn=512, out_dtype=F32, name="ssd_out",
                resid=resid_of(x, mod, 2))
    xp, xs = _moe_layer(x, mod, n_p, ls, 1, norm2_w[1], w_router[1], b_router[1], w_gate_up, b_gate_up,
                        w_down, b_down, split=True, name="moe1")
    new_s = fin[:, None]
    return (xp.reshape(bp, lp, d), xs.reshape(bs, ls, d), new_k, new_v, new_s)
```

```python
import functools
import math

import jax
import jax.numpy as jnp
from jax import lax
from jax.experimental import pallas as pl
from jax.experimental.pallas import tpu as pltpu

F32 = jnp.float32
BF16 = jnp.bfloat16

D_MODEL = 2048
NORM_EPS = 1e-6
GRID_W = 64
ROPE_THETA = 10000.0
ATTN_HEADS = 8
QK_DIM = 128
V_DIM = 256
SSD_D_INNER = 4096
SSD_HEAD_DIM = 64
SSD_HEADS = 64
SSD_GROUPS = 8
SSD_STATE = 128
SSD_CONV = 5
SSD_CHUNK = 128
SSD_BC_DIM = SSD_GROUPS * SSD_STATE
SSD_CONV_CH = SSD_D_INNER + 2 * SSD_BC_DIM
N_EXPERTS = 32
TOP_K = 4
D_FF = 2048
SWIGLU_LIMIT = 7.0
SWIGLU_ALPHA = 1.702
MOE_BLOCK = 256
MOE_RANGES = 4
ATTN_KEY_CHUNK = 512
CAST_ROWS = 256
LOG2_E = math.log2(math.e)
LANES = 128
GROUP_W = SSD_D_INNER // SSD_GROUPS
assert SSD_CHUNK == SSD_STATE == LANES and 2 * SSD_HEAD_DIM == LANES
VMEM_LIMIT = 56 * 1024 * 1024


def _params(*sem):
    return pltpu.CompilerParams(dimension_semantics=sem, vmem_limit_bytes=VMEM_LIMIT)


def _group_of_tile(i, tm, n_p, ls):
    return jnp.where(i * tm < n_p, 0, 1 + (i * tm - n_p) // ls)


def _mod_spec(which, tm, tn, n_p, ls):
    return pl.BlockSpec((1, 1, tn), lambda i, j: (_group_of_tile(i, tm, n_p, ls) * 6 + which, 0, j))


def _mm_kernel(a_ref, w_ref, *rest, epilogue):
    acc = jnp.dot(a_ref[...].astype(BF16), w_ref[...].astype(BF16), preferred_element_type=F32)
    if epilogue == "bias":
        b_ref, o_ref = rest
        o_ref[...] = (acc + b_ref[...]).astype(o_ref.dtype)
    elif epilogue == "resid":
        x_ref, g_ref, o_ref = rest[0], rest[1], rest[-1]
        o_ref[...] = x_ref[...] + g_ref[0] * acc
    else:
        (o_ref,) = rest
        o_ref[...] = acc.astype(o_ref.dtype)


def _matmul(a, w, *, n_out, col_off=0, tm, tn, out_dtype, name, layer=None, bias=None, resid=None):
    m, k = a.shape
    joff = col_off // tn
    if layer is None:
        w_spec = pl.BlockSpec((k, tn), lambda i, j: (0, j + joff))
        b_spec = pl.BlockSpec((1, tn), lambda i, j: (0, j))
    else:
        w_spec = pl.BlockSpec((None, k, tn), lambda i, j: (layer, 0, j + joff))
        b_spec = pl.BlockSpec((None, 1, tn), lambda i, j: (layer, 0, j))
    in_specs = [pl.BlockSpec((tm, k), lambda i, j: (i, 0)), w_spec]
    args = [a, w]
    if bias is not None:
        epilogue = "bias"
        in_specs.append(b_spec)
        args.append(bias)
    elif resid is not None:
        epilogue = "resid"
        x, mod, which, n_p, ls = resid
        in_specs += [pl.BlockSpec((tm, tn), lambda i, j: (i, j)), _mod_spec(which, tm, tn, n_p, ls)]
        args += [x, mod]
    else:
        epilogue = "plain"
    return pl.pallas_call(
        functools.partial(_mm_kernel, epilogue=epilogue),
        grid=(m // tm, n_out // tn),
        in_specs=in_specs,
        out_specs=pl.BlockSpec((tm, tn), lambda i, j: (i, j)),
        out_shape=jax.ShapeDtypeStruct((m, n_out), out_dtype),
        compiler_params=_params("parallel", "arbitrary"),
        name=name,
    )(*args)


def _resid_rows_matmul(a_table, w, x_rows, mod, which, out_into, *, row_off, n_p, ls, tm, tn, name):
    n_rows, k = a_table.shape
    m, n_out = x_rows.shape
    ioff = row_off // tm
    in_specs = [pl.BlockSpec((tm, k), lambda i, j: (i + ioff, 0)),
                pl.BlockSpec((k, tn), lambda i, j: (0, j)),
                pl.BlockSpec((tm, tn), lambda i, j: (i, j)),
                pl.BlockSpec((1, 1, tn), lambda i, j: (_group_of_tile(i + ioff, tm, n_p, ls) * 6 + which, 0, j))]
    args = [a_table, w, x_rows, mod]
    aliases = {}
    if out_into is not None:
        in_specs.append(pl.BlockSpec(memory_space=pl.ANY))
        aliases = {len(args): 0}
        args.append(out_into)
    return pl.pallas_call(
        functools.partial(_mm_kernel, epilogue="resid"),
        grid=(m // tm, n_out // tn),
        in_specs=in_specs,
        out_specs=pl.BlockSpec((tm, tn), lambda i, j: (i + ioff, j)),
        out_shape=jax.ShapeDtypeStruct((n_rows, n_out), F32),
        input_output_aliases=aliases,
        compiler_params=_params("parallel", "arbitrary"),
        name=name,
    )(*args)


def _modnorm(x, w, shift, scale):
    ms = jnp.mean(x * x, axis=-1, keepdims=True)
    return (x * lax.rsqrt(ms + NORM_EPS) * w) * (1.0 + scale) + shift


def _modnorm_kernel(x_ref, w_ref, sh_ref, sc_ref, o_ref):
    o_ref[...] = _modnorm(x_ref[...], w_ref[...], sh_ref[0], sc_ref[0]).astype(o_ref.dtype)


def _modnorm_call(x, w, mod, which_shift, n_p, ls, *, tm, name):
    m, d = x.shape
    return pl.pallas_call(
        _modnorm_kernel,
        grid=(m // tm, 1),
        in_specs=[pl.BlockSpec((tm, d), lambda i, j: (i, 0)),
                  pl.BlockSpec((1, d), lambda i, j: (0, 0)),
                  _mod_spec(which_shift, tm, d, n_p, ls),
                  _mod_spec(which_shift + 1, tm, d, n_p, ls)],
        out_specs=pl.BlockSpec((tm, d), lambda i, j: (i, 0)),
        out_shape=jax.ShapeDtypeStruct((m, d), BF16),
        compiler_params=_params("parallel", "arbitrary"),
        name=name,
    )(x, w, mod, mod)


def _split2(v):
    hi = v.astype(BF16)
    lo = (v - hi.astype(F32)).astype(BF16)
    return hi, lo


def _router_kernel(x_ref, w_ref, sh_ref, sc_ref, wr_ref, br_ref, h_ref, idx_ref, gate_ref, rank_ref, cnt_ref,
                   run_ref):
    @pl.when(pl.program_id(0) == 0)
    def _():
        run_ref[...] = jnp.zeros_like(run_ref)

    h = _modnorm(x_ref[...], w_ref[...], sh_ref[0], sc_ref[0])
    h_ref[...] = h.astype(BF16)
    h_hi, h_lo = _split2(h)
    w_hi, w_lo = _split2(wr_ref[...])
    logits = (jnp.dot(h_hi, w_hi, preferred_element_type=F32)
              + jnp.dot(h_hi, w_lo, preferred_element_type=F32)
              + jnp.dot(h_lo, w_hi, preferred_element_type=F32)) + br_ref[...]
    lane = lax.broadcasted_iota(jnp.int32, logits.shape, 1).astype(F32)
    cur = jnp.where(lane < N_EXPERTS, logits, -jnp.inf)
    vals, idxs = [], []
    for _ in range(TOP_K):
        m = jnp.max(cur, axis=-1, keepdims=True)
        am = jnp.min(jnp.where(cur == m, lane, float(LANES)), axis=-1, keepdims=True)
        vals.append(m)
        idxs.append(am)
        cur = jnp.where(lane == am, -jnp.inf, cur)
    exps = [jnp.exp(v - vals[0]) for v in vals]
    denom = exps[0] + exps[1] + exps[2] + exps[3]
    idx_out = jnp.zeros(logits.shape, F32)
    gate_out = jnp.zeros(logits.shape, F32)
    for k in range(TOP_K):
        idx_out = jnp.where(lane == k, idxs[k], idx_out)
        gate_out = jnp.where(lane == k, exps[k] / denom, gate_out)
    idx_ref[...] = idx_out.astype(jnp.int32)
    gate_ref[...] = gate_out
    tm = logits.shape[0]
    chosen = functools.reduce(jnp.logical_or, [lane == am for am in idxs])
    onehot = jnp.where(chosen, 1.0, 0.0)
    ti = lax.broadcasted_iota(jnp.int32, (tm, tm), 0)
    tj = lax.broadcasted_iota(jnp.int32, (tm, tm), 1)
    earlier = jnp.where(ti > tj, 1.0, 0.0).astype(BF16)
    base = run_ref[...] + jnp.dot(earlier, onehot.astype(BF16), preferred_element_type=F32)
    rank_out = jnp.zeros(logits.shape, F32)
    for k in range(TOP_K):
        rk = jnp.sum(jnp.where(lane == idxs[k], base, 0.0), axis=-1, keepdims=True)
        rank_out = jnp.where(lane == k, rk, rank_out)
    rank_ref[...] = rank_out.astype(jnp.int32)
    run_ref[...] = run_ref[...] + jnp.sum(onehot, axis=0, keepdims=True)
    cnt_ref[...] = run_ref[...].astype(jnp.int32)


def _router_call(x, w, mod, n_p, ls, w_router, b_router, *, tm, name):
    m, d = x.shape
    wr = jnp.pad(w_router, ((0, 0), (0, LANES - N_EXPERTS)))
    br = jnp.pad(b_router, (0, LANES - N_EXPERTS)).reshape(1, LANES)
    return pl.pallas_call(
        _router_kernel,
        grid=(m // tm, 1),
        in_specs=[pl.BlockSpec((tm, d), lambda i, j: (i, 0)),
                  pl.BlockSpec((1, d), lambda i, j: (0, 0)),
                  _mod_spec(3, tm, d, n_p, ls),
                  _mod_spec(4, tm, d, n_p, ls),
                  pl.BlockSpec((d, LANES), lambda i, j: (0, 0)),
                  pl.BlockSpec((1, LANES), lambda i, j: (0, 0))],
        out_specs=[pl.BlockSpec((tm, d), lambda i, j: (i, 0)),
                   pl.BlockSpec((tm, LANES), lambda i, j: (i, 0)),
                   pl.BlockSpec((tm, LANES), lambda i, j: (i, 0)),
                   pl.BlockSpec((tm, LANES), lambda i, j: (i, 0)),
                   pl.BlockSpec((1, LANES), lambda i, j: (0, 0))],
        out_shape=[jax.ShapeDtypeStruct((m, d), BF16),
                   jax.ShapeDtypeStruct((m, LANES), jnp.int32),
                   jax.ShapeDtypeStruct((m, LANES), F32),
                   jax.ShapeDtypeStruct((m, LANES), jnp.int32),
                   jax.ShapeDtypeStruct((1, LANES), jnp.int32)],
        scratch_shapes=[pltpu.VMEM((1, LANES), F32)],
        compiler_params=_params("arbitrary", "arbitrary"),
        name=name,
    )(x, w, mod, mod, wr, br)


def _qkv_kernel(x_ref, n1_ref, sh_ref, sc_ref, w_ref, nw_ref, *rest, rope, tn):
    if rope:
        cos_ref, sin_ref, q_ref, k_ref, v_ref, h_s = rest
    else:
        q_ref, k_ref, v_ref, h_s = rest
    j = pl.program_id(1)
    nq = D_MODEL // tn

    @pl.when(j == 0)
    def _():
        h_s[...] = _modnorm(x_ref[...], n1_ref[...], sh_ref[0], sc_ref[0]).astype(BF16)

    acc = jnp.dot(h_s[...], w_ref[...], preferred_element_type=F32)

    def normed(widx, out_scale):
        w = nw_ref[widx:widx + 1, :]
        outs = []
        for c in range(tn // QK_DIM):
            xc = acc[:, c * QK_DIM:(c + 1) * QK_DIM]
            ms = jnp.mean(xc * xc, axis=-1, keepdims=True)
            y = xc * lax.rsqrt(ms + NORM_EPS) * w
            if rope:
                lane = lax.broadcasted_iota(jnp.int32, y.shape, 1)
                partner = jnp.where(lane % 64 < 32, pltpu.roll(y, QK_DIM - 32, 1), pltpu.roll(y, 32, 1))
                y = y * cos_ref[...] + partner * sin_ref[...]
            outs.append(y * out_scale if out_scale != 1.0 else y)
        return jnp.concatenate(outs, axis=1)

    @pl.when(j < nq)
    def _():
        q_ref[...] = normed(0, QK_DIM ** -0.5 * LOG2_E).astype(q_ref.dtype)

    @pl.when((j >= nq) & (j < 2 * nq))
    def _():
        k_ref[...] = normed(1, 1.0).astype(k_ref.dtype)

    @pl.when(j >= 2 * nq)
    def _():
        v_ref[...] = acc.astype(v_ref.dtype)


def _qkv_call(x_rows, norm1_w, mod, w_qkv, qk_norm_w, rope_tabs, *, row_off, n_p, kv_dtype, tm, tn, ls, name):
    m, d = x_rows.shape
    nq = d // tn
    ioff = row_off // tm
    rope = rope_tabs is not None

    def mod_spec(which):
        return pl.BlockSpec((1, 1, d), lambda i, j: (_group_of_tile(i + ioff, tm, n_p, ls) * 6 + which, 0, 0))

    in_specs = [pl.BlockSpec((tm, d), lambda i, j: (i, 0)),
                pl.BlockSpec((1, d), lambda i, j: (0, 0)),
                mod_spec(0), mod_spec(1),
                pl.BlockSpec((d, tn), lambda i, j: (0, j)),
                pl.BlockSpec((2, QK_DIM), lambda i, j: (0, 0))]
    args = [x_rows, norm1_w.reshape(1, d), mod, mod, w_qkv, qk_norm_w]
    if rope:
        nt = ls // tm
        in_specs += [pl.BlockSpec((tm, QK_DIM), lambda i, j: (i % nt, 0))] * 2
        args += list(rope_tabs)
    return pl.pallas_call(
        functools.partial(_qkv_kernel, rope=rope, tn=tn),
        grid=(m // tm, 3 * nq),
        in_specs=in_specs,
        out_specs=[pl.BlockSpec((tm, tn), lambda i, j: (i, jnp.minimum(j, nq - 1))),
                   pl.BlockSpec((tm, tn), lambda i, j: (i, jnp.clip(j - nq, 0, nq - 1))),
                   pl.BlockSpec((tm, tn), lambda i, j: (i, jnp.clip(j - 2 * nq, 0, nq - 1)))],
        out_shape=[jax.ShapeDtypeStruct((m, d), BF16),
                   jax.ShapeDtypeStruct((m, d), kv_dtype),
                   jax.ShapeDtypeStruct((m, d), kv_dtype)],
        scratch_shapes=[pltpu.VMEM((tm, d), BF16)],
        compiler_params=_params("parallel", "arbitrary"),
        name=name,
    )(*args)


def _rope_tables(ls):
    rows = ls // GRID_W
    row = jnp.repeat(jnp.arange(rows, dtype=F32), GRID_W)
    col = jnp.tile(jnp.arange(GRID_W, dtype=F32), rows)
    axis_dim = QK_DIM // 2
    inv = ROPE_THETA ** (-jnp.arange(0, axis_dim, 2, dtype=F32) / axis_dim)
    ar, ac = row[:, None] * inv, col[:, None] * inv
    cos = jnp.concatenate([jnp.cos(ar), jnp.cos(ar), jnp.cos(ac), jnp.cos(ac)], axis=1)
    sin = jnp.concatenate([-jnp.sin(ar), jnp.sin(ar), -jnp.sin(ac), jnp.sin(ac)], axis=1)
    return cos, sin


def _attn_kernel(lam_ref, subw_ref, q_ref, *rest, nseg, lam_init):
    k_refs, v_refs, o_ref = rest[:nseg], rest[nseg:2 * nseg], rest[-1]
    lp = lam_ref[...]
    lam = (jnp.exp(jnp.sum(lp[0:1] * lp[1:2], axis=-1, keepdims=True))
           - jnp.exp(jnp.sum(lp[2:3] * lp[3:4], axis=-1, keepdims=True)) + lam_init)
    q = q_ref[...]
    chunks = [(k_ref, v_ref, c0) for k_ref, v_ref in zip(k_refs, v_refs)
              for c0 in range(0, k_ref.shape[0], min(ATTN_KEY_CHUNK, k_ref.shape[0]))]
    outs = []
    for mi in range(2):
        qm = q[:, mi * QK_DIM:(mi + 1) * QK_DIM]
        mx = den = pv = None
        for k_ref, v_ref, c0 in chunks:
            kc = min(ATTN_KEY_CHUNK, k_ref.shape[0])
            k = k_ref[c0:c0 + kc, mi * QK_DIM:(mi + 1) * QK_DIM].astype(BF16)
            v = v_ref[c0:c0 + kc, :].astype(BF16)
            s = lax.dot_general(qm, k, (((1,), (1,)), ((), ())), preferred_element_type=F32)
            cmax = jnp.max(s, axis=-1, keepdims=True)
            if mx is None:
                mx = cmax
                p = jnp.exp2(s - mx)
                den = jnp.sum(p, axis=-1, keepdims=True)
                pv = jnp.dot(p.astype(BF16), v, preferred_element_type=F32)
            else:
                new_mx = jnp.maximum(mx, cmax)
                alpha = jnp.exp2(mx - new_mx)
                p = jnp.exp2(s - new_mx)
                den = den * alpha + jnp.sum(p, axis=-1, keepdims=True)
                pv = pv * alpha + jnp.dot(p.astype(BF16), v, preferred_element_type=F32)
                mx = new_mx
        outs.append(pv * (1.0 / den))
    acc = outs[0] - lam * outs[1]
    ms = jnp.mean(acc * acc, axis=-1, keepdims=True)
    o = acc * lax.rsqrt(ms + NORM_EPS) * subw_ref[...] * (1.0 - lam_init)
    o_ref[...] = o.astype(o_ref.dtype)


def _attn_call(q, kv_segs, lam_params, subln_w, o_into, *, n_rows, row_off, nb, lq, tq, lam_init, name):
    nqb = lq // tq
    nseg = len(kv_segs)
    ooff = row_off // tq
    in_specs = [pl.BlockSpec((4, QK_DIM), lambda b, h, i: (0, 0)),
                pl.BlockSpec((1, V_DIM), lambda b, h, i: (0, 0)),
                pl.BlockSpec((tq, V_DIM), lambda b, h, i: (b * nqb + i, h))]
    in_specs += [pl.BlockSpec((lk, V_DIM), lambda b, h, i: (b, h)) for (_, _, lk) in kv_segs] * 2
    args = [lam_params, subln_w.reshape(1, V_DIM), q]
    args += [k for (k, _, _) in kv_segs] + [v for (_, v, _) in kv_segs]
    aliases = {}
    if o_into is not None:
        in_specs.append(pl.BlockSpec(memory_space=pl.ANY))
        aliases = {len(args): 0}
        args.append(o_into)
    return pl.pallas_call(
        functools.partial(_attn_kernel, nseg=nseg, lam_init=lam_init),
        grid=(nb, ATTN_HEADS, nqb),
        in_specs=in_specs,
        out_specs=pl.BlockSpec((tq, V_DIM), lambda b, h, i: (ooff + b * nqb + i, h)),
        out_shape=jax.ShapeDtypeStruct((n_rows, q.shape[1]), BF16),
        input_output_aliases=aliases,
        compiler_params=_params("parallel", "parallel", "arbitrary"),
        name=name,
    )(*args)


def _conv_kernel(xp_ref, x_ref, xn_ref, w_ref, b_ref, o_ref, ext_ref, *, tc, n_p, lp, ls, halo):
    i = pl.program_id(0)
    row0 = i * tc
    in_prompt = row0 < n_p
    first = jnp.where(in_prompt, row0 % lp == 0, (row0 - n_p) % ls == 0)
    last = jnp.where(in_prompt, (row0 + tc) % lp == 0, (row0 + tc - n_p) % ls == 0)
    prev = xp_ref[...].astype(F32)[halo - 8:halo, :]
    nxt = xn_ref[...].astype(F32)[0:8, :]
    ext_ref[0:8, :] = jnp.where(first, 0.0, prev)
    ext_ref[8:8 + tc, :] = x_ref[...].astype(F32)
    ext_ref[8 + tc:16 + tc, :] = jnp.where(last, 0.0, nxt)
    acc = b_ref[...] + w_ref[0:1, :] * ext_ref[6:6 + tc, :]
    for k in range(1, SSD_CONV):
        acc = acc + w_ref[k:k + 1, :] * ext_ref[6 + k:6 + k + tc, :]
    o_ref[...] = (acc / (1.0 + jnp.exp(-acc))).astype(o_ref.dtype)


def _conv_call(xbc, conv_w, conv_b, *, n_p, lp, ls, tc, tcn, name):
    m, ch = xbc.shape
    halo = 16
    hb = tc // halo
    nhb = m // halo
    return pl.pallas_call(
        functools.partial(_conv_kernel, tc=tc, n_p=n_p, lp=lp, ls=ls, halo=halo),
        grid=(m // tc, ch // tcn),
        in_specs=[pl.BlockSpec((halo, tcn), lambda i, j: (jnp.maximum(i * hb - 1, 0), j)),
                  pl.BlockSpec((tc, tcn), lambda i, j: (i, j)),
                  pl.BlockSpec((halo, tcn), lambda i, j: (jnp.minimum((i + 1) * hb, nhb - 1), j)),
                  pl.BlockSpec((SSD_CONV, tcn), lambda i, j: (0, j)),
                  pl.BlockSpec((1, tcn), lambda i, j: (0, j))],
        out_specs=pl.BlockSpec((tc, tcn), lambda i, j: (i, j)),
        out_shape=jax.ShapeDtypeStruct((m, ch), BF16),
        scratch_shapes=[pltpu.VMEM((tc + 16, tcn), F32)],
        compiler_params=_params("parallel", "arbitrary"),
        name=name,
    )(xbc, xbc, xbc, conv_w, conv_b.reshape(1, ch))


def _ssd_kernel(xbc_ref, dtraw_ref, dtb_ref, alog_ref, tri_ref, *rest, direction, zero_init, nt):
    y_ref, fin_ref, st_ref = rest[-3:]
    if not zero_init:
        h0_ref = rest[0]
    t = pl.program_id(1)
    q = SSD_CHUNK
    hpg = SSD_HEADS // SSD_GROUPS

    @pl.when(t == 0)
    def _():
        for g in range(SSD_GROUPS):
            if zero_init:
                st_ref[g] = jnp.zeros((SSD_STATE, GROUP_W), F32)
            else:
                hg = h0_ref[0, 0, g * hpg:(g + 1) * hpg].reshape(GROUP_W, SSD_STATE)
                st_ref[g] = hg.T

    x = dtraw_ref[...] + dtb_ref[...]
    dt = jnp.maximum(x, 0.0) + jnp.log1p(jnp.exp(-jnp.abs(x)))
    a = dt * (-jnp.exp(alog_ref[...]) * LOG2_E)
    tri = tri_ref[...]
    a1 = a.astype(BF16)
    r1 = a - a1.astype(F32)
    a2 = r1.astype(BF16)
    a3 = (r1 - a2.astype(F32)).astype(BF16)
    cs = (jnp.dot(tri, a1, preferred_element_type=F32) + jnp.dot(tri, a2, preferred_element_type=F32)
          + jnp.dot(tri, a3, preferred_element_type=F32))
    tot = cs[q - 1:q, :] if direction == 0 else cs[0:1, :]
    cdec = jnp.exp2(tot)
    cs_t = cs.T
    dt_t = dt.T
    wend_t = (dt * jnp.exp2(tot - cs)).T
    li = lax.broadcasted_iota(jnp.int32, (q, q), 0)
    si = lax.broadcasted_iota(jnp.int32, (q, q), 1)
    mask = (li >= si) if direction == 0 else (li <= si)
    low = lax.broadcasted_iota(jnp.int32, (q, LANES), 1) < SSD_HEAD_DIM
    low_row = low[0:1, :]

    for g in range(SSD_GROUPS):
        bg = xbc_ref[:, SSD_D_INNER + g * SSD_STATE:SSD_D_INNER + (g + 1) * SSD_STATE]
        cg = xbc_ref[:, SSD_D_INNER + SSD_BC_DIM + g * SSD_STATE:SSD_D_INNER + SSD_BC_DIM + (g + 1) * SSD_STATE]
        cg32 = cg.astype(F32)
        bg_t = bg.astype(F32).T
        cb = lax.dot_general(cg, bg, (((1,), (1,)), ((), ())), preferred_element_type=F32)
        for p in range(hpg // 2):
            col0 = g * GROUP_W + p * LANES
            xp = xbc_ref[:, col0:col0 + LANES]
            zero = jnp.zeros_like(xp)
            x_lo = jnp.where(low, xp, zero)
            x_hi = jnp.where(low, zero, xp)
            s_in = st_ref[g, :, p * LANES:(p + 1) * LANES]
            s_bf = s_in.astype(BF16)
            s_lo = jnp.where(low, s_bf, zero)
            s_hi = jnp.where(low, zero, s_bf)
            c0 = direction * SSD_HEADS + g * hpg + 2 * p
            diag, offd, bw = [], [], []
            for c in (c0, c0 + 1):
                col = jnp.broadcast_to(cs[:, c:c + 1], (q, q))
                seg = col - cs_t[c:c + 1, :]
                diag.append((cb * jnp.exp2(jnp.where(mask, seg, -jnp.inf)) * dt_t[c:c + 1, :]).astype(BF16))
                offd.append((cg32 * jnp.exp2(col)).astype(BF16))
                bw.append((bg_t * wend_t[c:c + 1, :]).astype(BF16))
            lhs = jnp.concatenate(diag + offd, axis=1)
            rhs = jnp.concatenate([x_lo, x_hi, s_lo, s_hi], axis=0)
            y_ref[:, col0:col0 + LANES] = jnp.dot(lhs, rhs, preferred_element_type=F32).astype(y_ref.dtype)
            new = jnp.dot(jnp.concatenate(bw, axis=1), jnp.concatenate([x_lo, x_hi], axis=0),
                          preferred_element_type=F32)
            keep = jnp.where(low_row, cdec[:, c0:c0 + 1], cdec[:, c0 + 1:c0 + 2])
            st_ref[g, :, p * LANES:(p + 1) * LANES] = s_in * keep + new

    @pl.when(t == nt - 1)
    def _():
        for g in range(SSD_GROUPS):
            fin_ref[0, g * hpg:(g + 1) * hpg] = st_ref[g].T.reshape(hpg, SSD_HEAD_DIM, SSD_STATE)


def _ssd_scan_call(xbc_act, dt_raw, dt_bias, a_log, h0, y_into, fin_into, *, nb, seq, row_off, direction,
                   name):
    q = SSD_CHUNK
    nt = seq // q
    boff = row_off // q
    li = jnp.arange(q)[:, None]
    si = jnp.arange(q)[None, :]
    tri = ((li >= si) if direction == 0 else (li <= si)).astype(BF16)

    def chunk(b, t):
        return boff + b * nt + (t if direction == 0 else nt - 1 - t)

    zero_init = h0 is None
    in_specs = [pl.BlockSpec((q, SSD_CONV_CH), lambda b, t: (chunk(b, t), 0)),
                pl.BlockSpec((q, LANES), lambda b, t: (chunk(b, t), 0)),
                pl.BlockSpec((1, LANES), lambda b, t: (0, 0)),
                pl.BlockSpec((1, LANES), lambda b, t: (0, 0)),
                pl.BlockSpec((q, q), lambda b, t: (0, 0))]
    args = [xbc_act, dt_raw, dt_bias.reshape(1, LANES), a_log.reshape(1, LANES), tri]
    if not zero_init:
        in_specs.append(pl.BlockSpec((1, 1, SSD_HEADS, SSD_HEAD_DIM, SSD_STATE),
                                     lambda b, t: (b, direction, 0, 0, 0)))
        args.append(h0)
    aliases = {}
    if y_into is not None:
        in_specs.append(pl.BlockSpec(memory_space=pl.ANY))
        aliases = {len(args): 0}
        args.append(y_into)
    if fin_into is not None:
        in_specs.append(pl.BlockSpec(memory_space=pl.ANY))
        aliases[len(args)] = 1
        args.append(fin_into)
    return pl.pallas_call(
        functools.partial(_ssd_kernel, direction=direction, zero_init=zero_init, nt=nt),
        grid=(nb, nt),
        in_specs=in_specs,
        out_specs=[pl.BlockSpec((q, SSD_D_INNER), lambda b, t: (chunk(b, t), 0)),
                   pl.BlockSpec((1, None, SSD_HEADS, SSD_HEAD_DIM, SSD_STATE),
                                lambda b, t: (b, direction, 0, 0, 0))],
        out_shape=[jax.ShapeDtypeStruct((xbc_act.shape[0], SSD_D_INNER), BF16),
                   jax.ShapeDtypeStruct((nb, 2, SSD_HEADS, SSD_HEAD_DIM, SSD_STATE), F32)],
        scratch_shapes=[pltpu.VMEM((SSD_GROUPS, SSD_STATE, GROUP_W), F32)],
        input_output_aliases=aliases,
        compiler_params=_params("parallel", "arbitrary"),
        name=name,
    )(*args)


def _ssd_post_kernel(yf_ref, yb_ref, xs_ref, z_ref, d_ref, nw_ref, o_ref):
    z = z_ref[...].astype(F32)
    y = yf_ref[...].astype(F32) + yb_ref[...].astype(F32) + d_ref[...] * xs_ref[...].astype(F32)
    y = y * (z / (1.0 + jnp.exp(-z)))
    outs = []
    for g in range(SSD_GROUPS):
        yg = y[:, g * GROUP_W:(g + 1) * GROUP_W]
        ms = jnp.mean(yg * yg, axis=-1, keepdims=True)
        outs.append(yg * lax.rsqrt(ms + NORM_EPS) * nw_ref[:, g * GROUP_W:(g + 1) * GROUP_W])
    o_ref[...] = jnp.concatenate(outs, axis=1).astype(o_ref.dtype)


def _ssd_post_call(y_f, y_b, xbc_act, z, d_tot, norm_w, *, tm, name):
    m = y_f.shape[0]
    di = SSD_D_INNER
    row = pl.BlockSpec((tm, di), lambda i: (i, 0))
    vec = pl.BlockSpec((1, di), lambda i: (0, 0))
    return pl.pallas_call(
        _ssd_post_kernel,
        grid=(m // tm,),
        in_specs=[row, row, row, row, vec, vec],
        out_specs=row,
        out_shape=jax.ShapeDtypeStruct((m, di), BF16),
        compiler_params=_params("parallel"),
        name=name,
    )(y_f, y_b, xbc_act, z, d_tot, norm_w.reshape(1, di))


def _run_schedule(block_e, n_used):
    nblk = block_e.shape[0]
    idx = jnp.arange(nblk, dtype=jnp.int32)
    prev = jnp.concatenate([block_e[:1] - 1, block_e[:-1]])
    first = ((idx < n_used[0]) & (block_e != prev)).astype(jnp.int32)
    ridx = jnp.cumsum(first) - 1
    is_next = (first[None, :] == 1) & (ridx[None, :] == ridx[:, None] + 1)
    next_e = jnp.where(jnp.any(is_next, axis=1), jnp.sum(jnp.where(is_next, block_e[None, :], 0), axis=1), -1)
    counts = jnp.stack([n_used[0], jnp.sum(first)])
    return (block_e, counts.astype(jnp.int32), first, ridx.astype(jnp.int32), next_e.astype(jnp.int32))


def _weight_copies(w_hbm, stage, sems, layer, e, jj, slot, col_tiles, tn):
    return [pltpu.make_async_copy(w_hbm.at[layer, e, :, pl.ds(pl.multiple_of((off + jj) * tn, tn), tn)],
                                  stage.at[slot, wi], sems.at[slot, wi])
            for wi, off in enumerate(col_tiles)]


def _stage_run_weights(sched, w_hbm, stage, sems, w_bf, *, layer, col_tiles, tn, nj):
    be_ref, cnt_ref, first_ref, ridx_ref, next_ref = sched
    j = pl.program_id(0)
    i = pl.program_id(1)
    copies = functools.partial(_weight_copies, w_hbm, stage, sems, layer, col_tiles=col_tiles, tn=tn)

    @pl.when(first_ref[i] == 1)
    def _():
        slot = (j * cnt_ref[1] + ridx_ref[i]) & 1
        e = be_ref[i]

        @pl.when((i == 0) & (j == 0))
        def _():
            for c in copies(e, j, slot):
                c.start()

        for c in copies(e, j, slot):
            c.wait()
        def cast_rows(r, carry):
            rows = pl.ds(pl.multiple_of(r * CAST_ROWS, CAST_ROWS), CAST_ROWS)
            for wi in range(len(col_tiles)):
                w_bf[wi, rows, :] = stage[slot, wi, rows, :].astype(BF16)
            return carry

        lax.fori_loop(0, w_bf.shape[1] // CAST_ROWS, cast_rows, 0)
        nxt = next_ref[i]

        @pl.when(nxt >= 0)
        def _():
            for c in copies(nxt, j, 1 - slot):
                c.start()

        @pl.when((nxt < 0) & (j + 1 < nj))
        def _():
            for c in copies(be_ref[0], j + 1, 1 - slot):
                c.start()


def _moe_up_kernel(*refs, layer, tn, nj):
    sched, (x_ref, w_hbm, bg_ref, bu_ref, o_ref, stage, w_bf, sems) = refs[:5], refs[5:]
    i = pl.program_id(1)
    _stage_run_weights(sched, w_hbm, stage, sems, w_bf, layer=layer, col_tiles=(0, nj), tn=tn, nj=nj)
    nu_ref = sched[1]

    @pl.when(i < nu_ref[0])
    def _():
        x = x_ref[...]
        g = jnp.dot(x, w_bf[0], preferred_element_type=F32) + bg_ref[...]
        u = jnp.dot(x, w_bf[1], preferred_element_type=F32) + bu_ref[...]
        g = jnp.minimum(g, SWIGLU_LIMIT)
        u = jnp.clip(u, -SWIGLU_LIMIT, SWIGLU_LIMIT)
        act = (u + 1.0) * (g / (1.0 + jnp.exp2((-SWIGLU_ALPHA * LOG2_E) * g)))
        o_ref[...] = act.astype(o_ref.dtype)

    @pl.when(i >= nu_ref[0])
    def _():
        o_ref[...] = jnp.zeros_like(o_ref)


def _moe_down_kernel(*refs, layer, tn, nj):
    sched, (a_ref, w_hbm, b_ref), (o_ref, stage, w_bf, sems) = refs[:5], refs[5:8], refs[-4:]
    i = pl.program_id(1)
    _stage_run_weights(sched, w_hbm, stage, sems, w_bf, layer=layer, col_tiles=(0,), tn=tn, nj=nj)
    nu_ref = sched[1]

    @pl.when(i < nu_ref[0])
    def _():
        y = jnp.dot(a_ref[...], w_bf[0], preferred_element_type=F32) + b_ref[...]
        o_ref[...] = y.astype(o_ref.dtype)

    @pl.when(i >= nu_ref[0])
    def _():
        o_ref[...] = jnp.zeros_like(o_ref)


def _moe_experts(x_sorted, block_e, n_used, y_into, layer, w_gate_up, b_gate_up, w_down, b_down, *, blk_off,
                 total_slots, tn_up, tn_down, name):
    slots, d = x_sorted.shape
    nblk = slots // MOE_BLOCK
    nj = D_FF // tn_up
    nl = w_gate_up.shape[0]
    b_gu = b_gate_up.reshape(nl, N_EXPERTS, 1, 2 * D_FF)
    sched = _run_schedule(block_e, n_used)
    act = pl.pallas_call(
        functools.partial(_moe_up_kernel, layer=layer, tn=tn_up, nj=nj),
        grid_spec=pltpu.PrefetchScalarGridSpec(
            num_scalar_prefetch=len(sched),
            grid=(nj, nblk),
            in_specs=[pl.BlockSpec((MOE_BLOCK, d), lambda j, i, *s: (i, 0)),
                      pl.BlockSpec(memory_space=pl.ANY),
                      pl.BlockSpec((None, None, 1, tn_up), lambda j, i, be, *s: (layer, be[i], 0, j)),
                      pl.BlockSpec((None, None, 1, tn_up), lambda j, i, be, *s: (layer, be[i], 0, nj + j))],
            out_specs=pl.BlockSpec((MOE_BLOCK, tn_up), lambda j, i, *s: (i, j)),
            scratch_shapes=[pltpu.VMEM((2, 2, d, tn_up), F32), pltpu.VMEM((2, d, tn_up), BF16),
                            pltpu.SemaphoreType.DMA((2, 2))]),
        out_shape=jax.ShapeDtypeStruct((slots, D_FF), BF16),
        compiler_params=_params("arbitrary", "arbitrary"),
        name=name + "_up",
    )(*sched, x_sorted, w_gate_up, b_gu, b_gu)
    nj2 = d // tn_down
    in_specs = [pl.BlockSpec((MOE_BLOCK, D_FF), lambda j, i, *s: (i, 0)),
                pl.BlockSpec(memory_space=pl.ANY),
                pl.BlockSpec((None, None, 1, tn_down), lambda j, i, be, *s: (layer, be[i], 0, j))]
    args = [*sched, act, w_down, b_down.reshape(nl, N_EXPERTS, 1, d)]
    aliases = {}
    if y_into is not None:
        in_specs.append(pl.BlockSpec(memory_space=pl.ANY))
        aliases = {len(args): 0}
        args.append(y_into)
    return pl.pallas_call(
        functools.partial(_moe_down_kernel, layer=layer, tn=tn_down, nj=nj2),
        grid_spec=pltpu.PrefetchScalarGridSpec(
            num_scalar_prefetch=len(sched),
            grid=(nj2, nblk),
            in_specs=in_specs,
            out_specs=pl.BlockSpec((MOE_BLOCK, tn_down), lambda j, i, *s: (i + blk_off, j)),
            scratch_shapes=[pltpu.VMEM((2, 1, D_FF, tn_down), F32), pltpu.VMEM((1, D_FF, tn_down), BF16),
                            pltpu.SemaphoreType.DMA((2, 1))]),
        out_shape=jax.ShapeDtypeStruct((total_slots, d), BF16),
        input_output_aliases=aliases,
        compiler_params=_params("arbitrary", "arbitrary"),
        name=name + "_down",
    )(*args)


def _combine_kernel(*refs, split_tile):
    y_refs = refs[:TOP_K]
    gate_ref, x_ref, g2_ref = refs[TOP_K:TOP_K + 3]
    o_refs = refs[TOP_K + 3:]
    gates = gate_ref[...]
    f = None
    for k in range(TOP_K):
        term = gates[:, k:k + 1] * y_refs[k][...].astype(F32)
        f = term if f is None else f + term
    out = x_ref[...] + g2_ref[0] * f
    if split_tile is None:
        o_refs[0][...] = out
    else:
        i = pl.program_id(0)

        @pl.when(i < split_tile)
        def _():
            o_refs[0][...] = out

        @pl.when(i >= split_tile)
        def _():
            o_refs[1][...] = out


def _combine_call(y4, gate_pad, x, mod, n_p, ls, *, tm, split, name):
    n, d = x.shape
    st = n_p // tm
    if split:
        out_specs = [pl.BlockSpec((tm, d), lambda i, j: (jnp.minimum(i, st - 1), 0)),
                     pl.BlockSpec((tm, d), lambda i, j: (jnp.maximum(i - st, 0), 0))]
        out_shape = [jax.ShapeDtypeStruct((n_p, d), F32), jax.ShapeDtypeStruct((n - n_p, d), F32)]
    else:
        out_specs = pl.BlockSpec((tm, d), lambda i, j: (i, 0))
        out_shape = jax.ShapeDtypeStruct((n, d), F32)
    nt = n // tm
    y_specs = [pl.BlockSpec((tm, d), functools.partial(lambda i, j, k: (k * nt + i, 0), k=k))
               for k in range(TOP_K)]
    return pl.pallas_call(
        functools.partial(_combine_kernel, split_tile=st if split else None),
        grid=(nt, 1),
        in_specs=y_specs + [pl.BlockSpec((tm, LANES), lambda i, j: (i, 0)),
                            pl.BlockSpec((tm, d), lambda i, j: (i, 0)),
                            _mod_spec(5, tm, d, n_p, ls)],
        out_specs=out_specs,
        out_shape=out_shape,
        compiler_params=_params("arbitrary", "arbitrary"),
        name=name,
    )(*([y4] * TOP_K), gate_pad, x, mod)


def _moe_layer(x, mod, n_p, ls, layer, norm_w, w_router, b_router, w_gate_up, b_gate_up, w_down, b_down, *,
               split, name):
    n, d = x.shape
    h, idx_pad, gate_pad, rank_pad, cnt = _router_call(x, norm_w.reshape(1, d), mod, n_p, ls, w_router,
                                                       b_router, tm=256, name=name + "_router")
    n_assign = n * TOP_K
    flat_e = idx_pad[:, :TOP_K].reshape(-1)
    counts = cnt[0, :N_EXPERTS]
    padded = (counts + MOE_BLOCK - 1) // MOE_BLOCK * MOE_BLOCK
    pad_end = jnp.cumsum(padded)
    pad_start = pad_end - padded
    dest = pad_start[flat_e] + rank_pad[:, :TOP_K].reshape(-1)
    n_blocks = -(-n_assign // MOE_BLOCK) + N_EXPERTS
    slots = n_blocks * MOE_BLOCK
    slot_tok = (jnp.arange(slots, dtype=jnp.int32) % n).at[dest].set(
        jnp.arange(n_assign, dtype=jnp.int32) // TOP_K, unique_indices=True, mode="promise_in_bounds")
    block_start = jnp.arange(n_blocks, dtype=jnp.int32) * MOE_BLOCK
    block_e = jnp.minimum(jnp.sum((pad_end[None, :] <= block_start[:, None]).astype(jnp.int32), axis=1),
                          N_EXPERTS - 1)
    n_used = (pad_end[-1] // MOE_BLOCK).astype(jnp.int32).reshape(1)
    cb = n_blocks // MOE_RANGES
    y_sorted = None
    for c in range(MOE_RANGES):
        rows = slice(c * cb * MOE_BLOCK, (c + 1) * cb * MOE_BLOCK)
        x_sorted = h.at[slot_tok[rows]].get(mode="promise_in_bounds")
        used = jnp.clip(n_used - c * cb, 0, cb)
        y_sorted = _moe_experts(x_sorted, block_e[c * cb:(c + 1) * cb], used, y_sorted, layer, w_gate_up,
                                b_gate_up, w_down, b_down, blk_off=c * cb, total_slots=slots, tn_up=1024,
                                tn_down=2048, name="%s_r%d" % (name, c))
    dest_by_choice = dest.reshape(n, TOP_K).T.reshape(-1)
    y4 = y_sorted.at[dest_by_choice].get(mode="promise_in_bounds")
    return _combine_call(y4, gate_pad, x, mod, n_p, ls, tm=256, split=split, name=name + "_combine")


def _ada_call(cond, w_ada, b_ada, layer, *, name):
    g = cond.shape[0]
    a = jnp.pad(jax.nn.silu(cond), ((0, 16 - g), (0, 0))).astype(BF16)
    m = _matmul(a, w_ada, n_out=6 * D_MODEL, tm=16, tn=1024, out_dtype=F32, name=name, layer=layer,
                bias=b_ada.reshape(b_ada.shape[0], 1, 6 * D_MODEL))
    return m[:g].reshape(g * 6, 1, D_MODEL)


def kernel(x_prompt, x_sample, c, c_ctx, cache_k, cache_v, state_ssm, norm1_w, norm2_w, w_ada, b_ada, w_qkv, q_norm_w, k_norm_w, lambda_q1, lambda_k1, lambda_q2, lambda_k2, subln_w, w_o, w_in_ssd, conv_w, conv_b, dt_bias, a_log, d_skip, ssd_norm_w, w_out_ssd, w_router, b_router, w_gate_up, b_gate_up, w_down, b_down):
    bp, lp, d = x_prompt.shape
    bs, ls, _ = x_sample.shape
    past = cache_k.shape[2]
    n_p, n_s = bp * lp, bs * ls
    xp0, xs0 = x_prompt.reshape(n_p, d), x_sample.reshape(n_s, d)
    cond = jnp.concatenate([c_ctx[None], c], axis=0)
    resid_of = lambda xx, mod, which: (xx, mod, which, n_p, ls)

    n = n_p + n_s
    mod = _ada_call(cond, w_ada, b_ada, 0, name="ada0")
    wq = w_qkv[0].astype(BF16)
    qk_w = jnp.stack([q_norm_w[0], k_norm_w[0]], axis=0)
    lam_init = 0.8 - 0.6 * math.exp(-0.3 * 0)
    lam_params = jnp.stack([lambda_q1[0], lambda_k1[0], lambda_q2[0], lambda_k2[0]], axis=0)
    qp, kp, vp = _qkv_call(xp0, norm1_w[0], mod, wq, qk_w, None, row_off=0, n_p=n_p, kv_dtype=F32, tm=512,
                           tn=512, ls=ls, name="qkv_prompt")
    qs, ks, vs = _qkv_call(xs0, norm1_w[0], mod, wq, qk_w, _rope_tables(ls), row_off=n_p, n_p=n_p,
                           kv_dtype=BF16, tm=512, tn=512, ls=ls, name="qkv_sample")
    o = _attn_call(qp, [(kp, vp, lp)], lam_params, subln_w[0], None, n_rows=n, row_off=0, nb=bp, lq=lp,
                   tq=lp, lam_init=lam_init, name="attn_prompt")
    ck = cache_k[:, 0].reshape(bs * past, d)
    cv = cache_v[:, 0].reshape(bs * past, d)
    o = _attn_call(qs, [(ck, cv, past), (ks, vs, ls)], lam_params, subln_w[0], o, n_rows=n, row_off=n_p,
                   nb=bs, lq=ls, tq=256, lam_init=lam_init, name="attn_sample")
    wo = w_o[0].astype(BF16)
    x = _resid_rows_matmul(o, wo, xp0, mod, 2, None, row_off=0, n_p=n_p, ls=ls, tm=1024, tn=512,
                           name="attn_out_prompt")
    x = _resid_rows_matmul(o, wo, xs0, mod, 2, x, row_off=n_p, n_p=n_p, ls=ls, tm=1024, tn=512,
                           name="attn_out_sample")
    x = _moe_layer(x, mod, n_p, ls, 0, norm2_w[0], w_router[0], b_router[0], w_gate_up, b_gate_up,
                   w_down, b_down, split=False, name="moe0")
    new_k = kp.reshape(bp, 1, lp, ATTN_HEADS, 2, QK_DIM)
    new_v = vp.reshape(bp, 1, lp, ATTN_HEADS, V_DIM)

    mod = _ada_call(cond, w_ada, b_ada, 1, name="ada1")
    h = _modnorm_call(x, norm1_w[1].reshape(1, d), mod, 0, n_p, ls, tm=256, name="norm1_1")
    w_in = w_in_ssd[0].astype(BF16)
    z = _matmul(h, w_in, n_out=SSD_D_INNER, tm=1024, tn=512, out_dtype=BF16, name="ssd_in_z")
    xbc = _matmul(h, w_in, n_out=SSD_CONV_CH, col_off=SSD_D_INNER, tm=1024, tn=512, out_dtype=BF16,
                  name="ssd_in_xbc")
    dt_raw = _matmul(h, w_in, n_out=2 * SSD_HEADS, col_off=SSD_D_INNER + SSD_CONV_CH, tm=1024, tn=LANES,
                     out_dtype=F32, name="ssd_in_dt")
    xbc_act = _conv_call(xbc, conv_w[0], conv_b[0], n_p=n_p, lp=lp, ls=ls, tc=256, tcn=2048, name="ssd_conv")
    ys, fin = [], None
    for direction in (0, 1):
        y, fin = _ssd_scan_call(xbc_act, dt_raw, dt_bias[0], a_log[0], None, None, fin, nb=bp, seq=lp,
                                row_off=0, direction=direction, name="ssd_scan_prompt%d" % direction)
        y, _ = _ssd_scan_call(xbc_act, dt_raw, dt_bias[0], a_log[0], state_ssm[:, 0], y, None, nb=bs, seq=ls,
                              row_off=n_p, direction=direction, name="ssd_scan_sample%d" % direction)
        ys.append(y)
    d_tot = jnp.repeat(d_skip[0, 0] + d_skip[0, 1], SSD_HEAD_DIM).reshape(1, SSD_D_INNER)
    yn = _ssd_post_call(ys[0], ys[1], xbc_act, z, d_tot, ssd_norm_w[0], tm=256, name="ssd_post")
    x = _matmul(yn, w_out_ssd[0].astype(BF16), n_out=d, tm=1024, tn=512, out_dtype=F32, name="ssd_out",
                resid=resid_of(x, mod, 2))
    xp, xs = _moe_layer(x, mod, n_p, ls, 1, norm2_w[1], w_router[1], b_router[1], w_gate_up, b_gate_up,
                        w_down, b_down, split=True, name="moe1")
    new_s = fin[:, None]
    return (xp.reshape(bp, lp, d), xs.reshape(bs, ls, d), new_k, new_v, new_s)
```

```python
import functools
import math

import jax
import jax.numpy as jnp
from jax import lax
from jax.experimental import pallas as pl
from jax.experimental.pallas import tpu as pltpu

F32 = jnp.float32
BF16 = jnp.bfloat16

D_MODEL = 2048
NORM_EPS = 1e-6
GRID_W = 64
ROPE_THETA = 10000.0
ATTN_HEADS = 8
QK_DIM = 128
V_DIM = 256
SSD_D_INNER = 4096
SSD_HEAD_DIM = 64
SSD_HEADS = 64
SSD_GROUPS = 8
SSD_STATE = 128
SSD_CONV = 5
SSD_CHUNK = 128
SSD_BC_DIM = SSD_GROUPS * SSD_STATE
SSD_CONV_CH = SSD_D_INNER + 2 * SSD_BC_DIM
N_EXPERTS = 32
TOP_K = 4
D_FF = 2048
SWIGLU_LIMIT = 7.0
SWIGLU_ALPHA = 1.702
MOE_BLOCK = 256
MOE_RANGES = 4
ATTN_KEY_CHUNK = 512
CAST_ROWS = 256
LOG2_E = math.log2(math.e)
LANES = 128
GROUP_W = SSD_D_INNER // SSD_GROUPS
assert SSD_CHUNK == SSD_STATE == LANES and 2 * SSD_HEAD_DIM == LANES
VMEM_LIMIT = 56 * 1024 * 1024


def _params(*sem):
    return pltpu.CompilerParams(dimension_semantics=sem, vmem_limit_bytes=VMEM_LIMIT)


def _group_of_tile(i, tm, n_p, ls):
    return jnp.where(i * tm < n_p, 0, 1 + (i * tm - n_p) // ls)


def _mod_spec(which, tm, tn, n_p, ls):
    return pl.BlockSpec((1, 1, tn), lambda i, j: (_group_of_tile(i, tm, n_p, ls) * 6 + which, 0, j))


def _mm_kernel(a_ref, w_ref, *rest, epilogue):
    acc = jnp.dot(a_ref[...].astype(BF16), w_ref[...].astype(BF16), preferred_element_type=F32)
    if epilogue == "bias":
        b_ref, o_ref = rest
        o_ref[...] = (acc + b_ref[...]).astype(o_ref.dtype)
    elif epilogue == "resid":
        x_ref, g_ref, o_ref = rest[0], rest[1], rest[-1]
        o_ref[...] = x_ref[...] + g_ref[0] * acc
    else:
        (o_ref,) = rest
        o_ref[...] = acc.astype(o_ref.dtype)


def _matmul(a, w, *, n_out, col_off=0, tm, tn, out_dtype, name, layer=None, bias=None, resid=None):
    m, k = a.shape
    joff = col_off // tn
    if layer is None:
        w_spec = pl.BlockSpec((k, tn), lambda i, j: (0, j + joff))
        b_spec = pl.BlockSpec((1, tn), lambda i, j: (0, j))
    else:
        w_spec = pl.BlockSpec((None, k, tn), lambda i, j: (layer, 0, j + joff))
        b_spec = pl.BlockSpec((None, 1, tn), lambda i, j: (layer, 0, j))
    in_specs = [pl.BlockSpec((tm, k), lambda i, j: (i, 0)), w_spec]
    args = [a, w]
    if bias is not None:
        epilogue = "bias"
        in_specs.append(b_spec)
        args.append(bias)
    elif resid is not None:
        epilogue = "resid"
        x, mod, which, n_p, ls = resid
        in_specs += [pl.BlockSpec((tm, tn), lambda i, j: (i, j)), _mod_spec(which, tm, tn, n_p, ls)]
        args += [x, mod]
    else:
        epilogue = "plain"
    return pl.pallas_call(
        functools.partial(_mm_kernel, epilogue=epilogue),
        grid=(m // tm, n_out // tn),
        in_specs=in_specs,
        out_specs=pl.BlockSpec((tm, tn), lambda i, j: (i, j)),
        out_shape=jax.ShapeDtypeStruct((m, n_out), out_dtype),
        compiler_params=_params("parallel", "arbitrary"),
        name=name,
    )(*args)


def _resid_rows_matmul(a_table, w, x_rows, mod, which, out_into, *, row_off, n_p, ls, tm, tn, name):
    n_rows, k = a_table.shape
    m, n_out = x_rows.shape
    ioff = row_off // tm
    in_specs = [pl.BlockSpec((tm, k), lambda i, j: (i + ioff, 0)),
                pl.BlockSpec((k, tn), lambda i, j: (0, j)),
                pl.BlockSpec((tm, tn), lambda i, j: (i, j)),
                pl.BlockSpec((1, 1, tn), lambda i, j: (_group_of_tile(i + ioff, tm, n_p, ls) * 6 + which, 0, j))]
    args = [a_table, w, x_rows, mod]
    aliases = {}
    if out_into is not None:
        in_specs.append(pl.BlockSpec(memory_space=pl.ANY))
        aliases = {len(args): 0}
        args.append(out_into)
    return pl.pallas_call(
        functools.partial(_mm_kernel, epilogue="resid"),
        grid=(m // tm, n_out // tn),
        in_specs=in_specs,
        out_specs=pl.BlockSpec((tm, tn), lambda i, j: (i + ioff, j)),
        out_shape=jax.ShapeDtypeStruct((n_rows, n_out), F32),
        input_output_aliases=aliases,
        compiler_params=_params("parallel", "arbitrary"),
        name=name,
    )(*args)


def _modnorm(x, w, shift, scale):
    ms = jnp.mean(x * x, axis=-1, keepdims=True)
    return (x * lax.rsqrt(ms + NORM_EPS) * w) * (1.0 + scale) + shift


def _modnorm_kernel(x_ref, w_ref, sh_ref, sc_ref, o_ref):
    o_ref[...] = _modnorm(x_ref[...], w_ref[...], sh_ref[0], sc_ref[0]).astype(o_ref.dtype)


def _modnorm_call(x, w, mod, which_shift, n_p, ls, *, tm, name):
    m, d = x.shape
    return pl.pallas_call(
        _modnorm_kernel,
        grid=(m // tm, 1),
        in_specs=[pl.BlockSpec((tm, d), lambda i, j: (i, 0)),
                  pl.BlockSpec((1, d), lambda i, j: (0, 0)),
                  _mod_spec(which_shift, tm, d, n_p, ls),
                  _mod_spec(which_shift + 1, tm, d, n_p, ls)],
        out_specs=pl.BlockSpec((tm, d), lambda i, j: (i, 0)),
        out_shape=jax.ShapeDtypeStruct((m, d), BF16),
        compiler_params=_params("parallel", "arbitrary"),
        name=name,
    )(x, w, mod, mod)


def _split2(v):
    hi = v.astype(BF16)
    lo = (v - hi.astype(F32)).astype(BF16)
    return hi, lo


def _router_kernel(x_ref, w_ref, sh_ref, sc_ref, wr_ref, br_ref, h_ref, idx_ref, gate_ref, rank_ref, cnt_ref,
                   run_ref):
    @pl.when(pl.program_id(0) == 0)
    def _():
        run_ref[...] = jnp.zeros_like(run_ref)

    h = _modnorm(x_ref[...], w_ref[...], sh_ref[0], sc_ref[0])
    h_ref[...] = h.astype(BF16)
    h_hi, h_lo = _split2(h)
    w_hi, w_lo = _split2(wr_ref[...])
    logits = (jnp.dot(h_hi, w_hi, preferred_element_type=F32)
              + jnp.dot(h_hi, w_lo, preferred_element_type=F32)
              + jnp.dot(h_lo, w_hi, preferred_element_type=F32)) + br_ref[...]
    lane = lax.broadcasted_iota(jnp.int32, logits.shape, 1).astype(F32)
    cur = jnp.where(lane < N_EXPERTS, logits, -jnp.inf)
    vals, idxs = [], []
    for _ in range(TOP_K):
        m = jnp.max(cur, axis=-1, keepdims=True)
        am = jnp.min(jnp.where(cur == m, lane, float(LANES)), axis=-1, keepdims=True)
        vals.append(m)
        idxs.append(am)
        cur = jnp.where(lane == am, -jnp.inf, cur)
    exps = [jnp.exp(v - vals[0]) for v in vals]
    denom = exps[0] + exps[1] + exps[2] + exps[3]
    idx_out = jnp.zeros(logits.shape, F32)
    gate_out = jnp.zeros(logits.shape, F32)
    for k in range(TOP_K):
        idx_out = jnp.where(lane == k, idxs[k], idx_out)
        gate_out = jnp.where(lane == k, exps[k] / denom, gate_out)
    idx_ref[...] = idx_out.astype(jnp.int32)
    gate_ref[...] = gate_out
    tm = logits.shape[0]
    chosen = functools.reduce(jnp.logical_or, [lane == am for am in idxs])
    onehot = jnp.where(chosen, 1.0, 0.0)
    ti = lax.broadcasted_iota(jnp.int32, (tm, tm), 0)
    tj = lax.broadcasted_iota(jnp.int32, (tm, tm), 1)
    earlier = jnp.where(ti > tj, 1.0, 0.0).astype(BF16)
    base = run_ref[...] + jnp.dot(earlier, onehot.astype(BF16), preferred_element_type=F32)
    rank_out = jnp.zeros(logits.shape, F32)
    for k in range(TOP_K):
        rk = jnp.sum(jnp.where(lane == idxs[k], base, 0.0), axis=-1, keepdims=True)
        rank_out = jnp.where(lane == k, rk, rank_out)
    rank_ref[...] = rank_out.astype(jnp.int32)
    run_ref[...] = run_ref[...] + jnp.sum(onehot, axis=0, keepdims=True)
    cnt_ref[...] = run_ref[...].astype(jnp.int32)


def _router_call(x, w, mod, n_p, ls, w_router, b_router, *, tm, name):
    m, d = x.shape
    wr = jnp.pad(w_router, ((0, 0), (0, LANES - N_EXPERTS)))
    br = jnp.pad(b_router, (0, LANES - N_EXPERTS)).reshape(1, LANES)
    return pl.pallas_call(
        _router_kernel,
        grid=(m // tm, 1),
        in_specs=[pl.BlockSpec((tm, d), lambda i, j: (i, 0)),
                  pl.BlockSpec((1, d), lambda i, j: (0, 0)),
                  _mod_spec(3, tm, d, n_p, ls),
                  _mod_spec(4, tm, d, n_p, ls),
                  pl.BlockSpec((d, LANES), lambda i, j: (0, 0)),
                  pl.BlockSpec((1, LANES), lambda i, j: (0, 0))],
        out_specs=[pl.BlockSpec((tm, d), lambda i, j: (i, 0)),
                   pl.BlockSpec((tm, LANES), lambda i, j: (i, 0)),
                   pl.BlockSpec((tm, LANES), lambda i, j: (i, 0)),
                   pl.BlockSpec((tm, LANES), lambda i, j: (i, 0)),
                   pl.BlockSpec((1, LANES), lambda i, j: (0, 0))],
        out_shape=[jax.ShapeDtypeStruct((m, d), BF16),
                   jax.ShapeDtypeStruct((m, LANES), jnp.int32),
                   jax.ShapeDtypeStruct((m, LANES), F32),
                   jax.ShapeDtypeStruct((m, LANES), jnp.int32),
                   jax.ShapeDtypeStruct((1, LANES), jnp.int32)],
        scratch_shapes=[pltpu.VMEM((1, LANES), F32)],
        compiler_params=_params("arbitrary", "arbitrary"),
        name=name,
    )(x, w, mod, mod, wr, br)


def _qkv_kernel(x_ref, n1_ref, sh_ref, sc_ref, w_ref, nw_ref, *rest, rope, tn):
    if rope:
        cos_ref, sin_ref, q_ref, k_ref, v_ref, h_s = rest
    else:
        q_ref, k_ref, v_ref, h_s = rest
    j = pl.program_id(1)
    nq = D_MODEL // tn

    @pl.when(j == 0)
    def _():
        h_s[...] = _modnorm(x_ref[...], n1_ref[...], sh_ref[0], sc_ref[0]).astype(BF16)

    acc = jnp.dot(h_s[...], w_ref[...], preferred_element_type=F32)

    def normed(widx, out_scale):
        w = nw_ref[widx:widx + 1, :]
        outs = []
        for c in range(tn // QK_DIM):
            xc = acc[:, c * QK_DIM:(c + 1) * QK_DIM]
            ms = jnp.mean(xc * xc, axis=-1, keepdims=True)
            y = xc * lax.rsqrt(ms + NORM_EPS) * w
            if rope:
                lane = lax.broadcasted_iota(jnp.int32, y.shape, 1)
                partner = jnp.where(lane % 64 < 32, pltpu.roll(y, QK_DIM - 32, 1), pltpu.roll(y, 32, 1))
                y = y * cos_ref[...] + partner * sin_ref[...]
            outs.append(y * out_scale if out_scale != 1.0 else y)
        return jnp.concatenate(outs, axis=1)

    @pl.when(j < nq)
    def _():
        q_ref[...] = normed(0, QK_DIM ** -0.5 * LOG2_E).astype(q_ref.dtype)

    @pl.when((j >= nq) & (j < 2 * nq))
    def _():
        k_ref[...] = normed(1, 1.0).astype(k_ref.dtype)

    @pl.when(j >= 2 * nq)
    def _():
        v_ref[...] = acc.astype(v_ref.dtype)


def _qkv_call(x_rows, norm1_w, mod, w_qkv, qk_norm_w, rope_tabs, *, row_off, n_p, kv_dtype, tm, tn, ls, name):
    m, d = x_rows.shape
    nq = d // tn
    ioff = row_off // tm
    rope = rope_tabs is not None

    def mod_spec(which):
        return pl.BlockSpec((1, 1, d), lambda i, j: (_group_of_tile(i + ioff, tm, n_p, ls) * 6 + which, 0, 0))

    in_specs = [pl.BlockSpec((tm, d), lambda i, j: (i, 0)),
                pl.BlockSpec((1, d), lambda i, j: (0, 0)),
                mod_spec(0), mod_spec(1),
                pl.BlockSpec((d, tn), lambda i, j: (0, j)),
                pl.BlockSpec((2, QK_DIM), lambda i, j: (0, 0))]
    args = [x_rows, norm1_w.reshape(1, d), mod, mod, w_qkv, qk_norm_w]
    if rope:
        nt = ls // tm
        in_specs += [pl.BlockSpec((tm, QK_DIM), lambda i, j: (i % nt, 0))] * 2
        args += list(rope_tabs)
    return pl.pallas_call(
        functools.partial(_qkv_kernel, rope=rope, tn=tn),
        grid=(m // tm, 3 * nq),
        in_specs=in_specs,
        out_specs=[pl.BlockSpec((tm, tn), lambda i, j: (i, jnp.minimum(j, nq - 1))),
                   pl.BlockSpec((tm, tn), lambda i, j: (i, jnp.clip(j - nq, 0, nq - 1))),
                   pl.BlockSpec((tm, tn), lambda i, j: (i, jnp.clip(j - 2 * nq, 0, nq - 1)))],
        out_shape=[jax.ShapeDtypeStruct((m, d), BF16),
                   jax.ShapeDtypeStruct((m, d), kv_dtype),
                   jax.ShapeDtypeStruct((m, d), kv_dtype)],
        scratch_shapes=[pltpu.VMEM((tm, d), BF16)],
        compiler_params=_params("parallel", "arbitrary"),
        name=name,
    )(*args)


def _rope_tables(ls):
    rows = ls // GRID_W
    row = jnp.repeat(jnp.arange(rows, dtype=F32), GRID_W)
    col = jnp.tile(jnp.arange(GRID_W, dtype=F32), rows)
    axis_dim = QK_DIM // 2
    inv = ROPE_THETA ** (-jnp.arange(0, axis_dim, 2, dtype=F32) / axis_dim)
    ar, ac = row[:, None] * inv, col[:, None] * inv
    cos = jnp.concatenate([jnp.cos(ar), jnp.cos(ar), jnp.cos(ac), jnp.cos(ac)], axis=1)
    sin = jnp.concatenate([-jnp.sin(ar), jnp.sin(ar), -jnp.sin(ac), jnp.sin(ac)], axis=1)
    return cos, sin


def _attn_kernel(lam_ref, subw_ref, q_ref, *rest, nseg, lam_init):
    k_refs, v_refs, o_ref = rest[:nseg], rest[nseg:2 * nseg], rest[-1]
    lp = lam_ref[...]
    lam = (jnp.exp(jnp.sum(lp[0:1] * lp[1:2], axis=-1, keepdims=True))
           - jnp.exp(jnp.sum(lp[2:3] * lp[3:4], axis=-1, keepdims=True)) + lam_init)
    q = q_ref[...]
    chunks = [(k_ref, v_ref, c0) for k_ref, v_ref in zip(k_refs, v_refs)
              for c0 in range(0, k_ref.shape[0], min(ATTN_KEY_CHUNK, k_ref.shape[0]))]
    outs = []
    for mi in range(2):
        qm = q[:, mi * QK_DIM:(mi + 1) * QK_DIM]
        mx = den = pv = None
        for k_ref, v_ref, c0 in chunks:
            kc = min(ATTN_KEY_CHUNK, k_ref.shape[0])
            k = k_ref[c0:c0 + kc, mi * QK_DIM:(mi + 1) * QK_DIM].astype(BF16)
            v = v_ref[c0:c0 + kc, :].astype(BF16)
            s = lax.dot_general(qm, k, (((1,), (1,)), ((), ())), preferred_element_type=F32)
            cmax = jnp.max(s, axis=-1, keepdims=True)
            if mx is None:
                mx = cmax
                p = jnp.exp2(s - mx)
                den = jnp.sum(p, axis=-1, keepdims=True)
                pv = jnp.dot(p.astype(BF16), v, preferred_element_type=F32)
            else:
                new_mx = jnp.maximum(mx, cmax)
                alpha = jnp.exp2(mx - new_mx)
                p = jnp.exp2(s - new_mx)
                den = den * alpha + jnp.sum(p, axis=-1, keepdims=True)
                pv = pv * alpha + jnp.dot(p.astype(BF16), v, preferred_element_type=F32)
                mx = new_mx
        outs.append(pv * (1.0 / den))
    acc = outs[0] - lam * outs[1]
    ms = jnp.mean(acc * acc, axis=-1, keepdims=True)
    o = acc * lax.rsqrt(ms + NORM_EPS) * subw_ref[...] * (1.0 - lam_init)
    o_ref[...] = o.astype(o_ref.dtype)


def _attn_call(q, kv_segs, lam_params, subln_w, o_into, *, n_rows, row_off, nb, lq, tq, lam_init, name):
    nqb = lq // tq
    nseg = len(kv_segs)
    ooff = row_off // tq
    in_specs = [pl.BlockSpec((4, QK_DIM), lambda b, h, i: (0, 0)),
                pl.BlockSpec((1, V_DIM), lambda b, h, i: (0, 0)),
                pl.BlockSpec((tq, V_DIM), lambda b, h, i: (b * nqb + i, h))]
    in_specs += [pl.BlockSpec((lk, V_DIM), lambda b, h, i: (b, h)) for (_, _, lk) in kv_segs] * 2
    args = [lam_params, subln_w.reshape(1, V_DIM), q]
    args += [k for (k, _, _) in kv_segs] + [v for (_, v, _) in kv_segs]
    aliases = {}
    if o_into is not None:
        in_specs.append(pl.BlockSpec(memory_space=pl.ANY))
        aliases = {len(args): 0}
        args.append(o_into)
    return pl.pallas_call(
        functools.partial(_attn_kernel, nseg=nseg, lam_init=lam_init),
        grid=(nb, ATTN_HEADS, nqb),
        in_specs=in_specs,
        out_specs=pl.BlockSpec((tq, V_DIM), lambda b, h, i: (ooff + b * nqb + i, h)),
        out_shape=jax.ShapeDtypeStruct((n_rows, q.shape[1]), BF16),
        input_output_aliases=aliases,
        compiler_params=_params("parallel", "parallel", "arbitrary"),
        name=name,
    )(*args)


def _conv_kernel(xp_ref, x_ref, xn_ref, w_ref, b_ref, shift_ref, o_ref, *, tc, n_p, lp, ls, halo):
    i = pl.program_id(0)
    row0 = i * tc
    in_prompt = row0 < n_p
    first = jnp.where(in_prompt, row0 % lp == 0, (row0 - n_p) % ls == 0)
    last = jnp.where(in_prompt, (row0 + tc) % lp == 0, (row0 + tc - n_p) % ls == 0)
    x = x_ref[...]
    shifted = jnp.dot(shift_ref[...], x, preferred_element_type=F32)
    acc = b_ref[...] + w_ref[2:3, :] * x.astype(F32)
    for si, k in enumerate((0, 1, 3, 4)):
        acc = acc + w_ref[k:k + 1, :] * shifted[si * tc:(si + 1) * tc, :]
    prev = jnp.where(first, 0.0, xp_ref[...].astype(F32)[halo - 8:halo, :])
    nxt = jnp.where(last, 0.0, xn_ref[...].astype(F32)[0:8, :])
    r8 = lax.broadcasted_iota(jnp.int32, prev.shape, 0)
    top = (w_ref[0:1, :] * jnp.where(r8 < 2, pltpu.roll(prev, 2, 0), 0.0)
           + w_ref[1:2, :] * jnp.where(r8 < 1, pltpu.roll(prev, 1, 0), 0.0))
    bot = (w_ref[3:4, :] * jnp.where(r8 >= 7, pltpu.roll(nxt, 7, 0), 0.0)
           + w_ref[4:5, :] * jnp.where(r8 >= 6, pltpu.roll(nxt, 6, 0), 0.0))
    acc = acc + jnp.concatenate([top, jnp.zeros((tc - 16, acc.shape[1]), F32), bot], axis=0)
    o_ref[...] = (acc / (1.0 + jnp.exp2(-LOG2_E * acc))).astype(o_ref.dtype)


def _conv_call(xbc, conv_w, conv_b, *, n_p, lp, ls, tc, tcn, name):
    m, ch = xbc.shape
    halo = 16
    hb = tc // halo
    nhb = m // halo
    out_row = jnp.arange(tc)[None, :, None]
    in_row = jnp.arange(tc)[None, None, :]
    tap = jnp.array([0, 1, 3, 4])[:, None, None]
    shift = (in_row == out_row + tap - SSD_CONV // 2).astype(BF16).reshape(4 * tc, tc)
    return pl.pallas_call(
        functools.partial(_conv_kernel, tc=tc, n_p=n_p, lp=lp, ls=ls, halo=halo),
        grid=(m // tc, ch // tcn),
        in_specs=[pl.BlockSpec((halo, tcn), lambda i, j: (jnp.maximum(i * hb - 1, 0), j)),
                  pl.BlockSpec((tc, tcn), lambda i, j: (i, j)),
                  pl.BlockSpec((halo, tcn), lambda i, j: (jnp.minimum((i + 1) * hb, nhb - 1), j)),
                  pl.BlockSpec((SSD_CONV, tcn), lambda i, j: (0, j)),
                  pl.BlockSpec((1, tcn), lambda i, j: (0, j)),
                  pl.BlockSpec((4 * tc, tc), lambda i, j: (0, 0))],
        out_specs=pl.BlockSpec((tc, tcn), lambda i, j: (i, j)),
        out_shape=jax.ShapeDtypeStruct((m, ch), BF16),
        compiler_params=_params("parallel", "arbitrary"),
        name=name,
    )(xbc, xbc, xbc, conv_w, conv_b.reshape(1, ch), shift)


def _ssd_kernel(xbc_ref, dtraw_ref, dtb_ref, alog_ref, tri_ref, *rest, direction, zero_init, nt):
    y_ref, fin_ref, st_ref = rest[-3:]
    if not zero_init:
        h0_ref = rest[0]
    t = pl.program_id(1)
    q = SSD_CHUNK
    hpg = SSD_HEADS // SSD_GROUPS

    @pl.when(t == 0)
    def _():
        for g in range(SSD_GROUPS):
            if zero_init:
                st_ref[g] = jnp.zeros((SSD_STATE, GROUP_W), F32)
            else:
                hg = h0_ref[0, 0, g * hpg:(g + 1) * hpg].reshape(GROUP_W, SSD_STATE)
                st_ref[g] = hg.T

    x = dtraw_ref[...] + dtb_ref[...]
    dt = jnp.maximum(x, 0.0) + jnp.log1p(jnp.exp(-jnp.abs(x)))
    a = dt * (-jnp.exp(alog_ref[...]) * LOG2_E)
    tri = tri_ref[...]
    a1 = a.astype(BF16)
    r1 = a - a1.astype(F32)
    a2 = r1.astype(BF16)
    a3 = (r1 - a2.astype(F32)).astype(BF16)
    cs = (jnp.dot(tri, a1, preferred_element_type=F32) + jnp.dot(tri, a2, preferred_element_type=F32)
          + jnp.dot(tri, a3, preferred_element_type=F32))
    tot = cs[q - 1:q, :] if direction == 0 else cs[0:1, :]
    cdec = jnp.exp2(tot)
    cs_t = cs.T
    dt_t = dt.T
    wend_t = (dt * jnp.exp2(tot - cs)).T
    li = lax.broadcasted_iota(jnp.int32, (q, q), 0)
    si = lax.broadcasted_iota(jnp.int32, (q, q), 1)
    mask = (li >= si) if direction == 0 else (li <= si)
    low = lax.broadcasted_iota(jnp.int32, (q, LANES), 1) < SSD_HEAD_DIM
    low_row = low[0:1, :]

    for g in range(SSD_GROUPS):
        bg = xbc_ref[:, SSD_D_INNER + g * SSD_STATE:SSD_D_INNER + (g + 1) * SSD_STATE]
        cg = xbc_ref[:, SSD_D_INNER + SSD_BC_DIM + g * SSD_STATE:SSD_D_INNER + SSD_BC_DIM + (g + 1) * SSD_STATE]
        cg32 = cg.astype(F32)
        bg_t = bg.astype(F32).T
        cb = lax.dot_general(cg, bg, (((1,), (1,)), ((), ())), preferred_element_type=F32)
        for p in range(hpg // 2):
            col0 = g * GROUP_W + p * LANES
            xp = xbc_ref[:, col0:col0 + LANES]
            zero = jnp.zeros_like(xp)
            x_lo = jnp.where(low, xp, zero)
            x_hi = jnp.where(low, zero, xp)
            s_in = st_ref[g, :, p * LANES:(p + 1) * LANES]
            s_bf = s_in.astype(BF16)
            s_lo = jnp.where(low, s_bf, zero)
            s_hi = jnp.where(low, zero, s_bf)
            c0 = direction * SSD_HEADS + g * hpg + 2 * p
            diag, offd, bw = [], [], []
            for c in (c0, c0 + 1):
                col = jnp.broadcast_to(cs[:, c:c + 1], (q, q))
                seg = col - cs_t[c:c + 1, :]
                diag.append((cb * jnp.exp2(jnp.where(mask, seg, -jnp.inf)) * dt_t[c:c + 1, :]).astype(BF16))
                offd.append((cg32 * jnp.exp2(col)).astype(BF16))
                bw.append((bg_t * wend_t[c:c + 1, :]).astype(BF16))
            lhs = jnp.concatenate(diag + offd, axis=1)
            rhs = jnp.concatenate([x_lo, x_hi, s_lo, s_hi], axis=0)
            y_ref[:, col0:col0 + LANES] = jnp.dot(lhs, rhs, preferred_element_type=F32).astype(y_ref.dtype)
            new = jnp.dot(jnp.concatenate(bw, axis=1), jnp.concatenate([x_lo, x_hi], axis=0),
                          preferred_element_type=F32)
            keep = jnp.where(low_row, cdec[:, c0:c0 + 1], cdec[:, c0 + 1:c0 + 2])
            st_ref[g, :, p * LANES:(p + 1) * LANES] = s_in * keep + new

    @pl.when(t == nt - 1)
    def _():
        for g in range(SSD_GROUPS):
            fin_ref[0, g * hpg:(g + 1) * hpg] = st_ref[g].T.reshape(hpg, SSD_HEAD_DIM, SSD_STATE)


def _ssd_scan_call(xbc_act, dt_raw, dt_bias, a_log, h0, y_into, fin_into, *, nb, seq, row_off, direction,
                   name):
    q = SSD_CHUNK
    nt = seq // q
    boff = row_off // q
    li = jnp.arange(q)[:, None]
    si = jnp.arange(q)[None, :]
    tri = ((li >= si) if direction == 0 else (li <= si)).astype(BF16)

    def chunk(b, t):
        return boff + b * nt + (t if direction == 0 else nt - 1 - t)

    zero_init = h0 is None
    in_specs = [pl.BlockSpec((q, SSD_CONV_CH), lambda b, t: (chunk(b, t), 0)),
                pl.BlockSpec((q, LANES), lambda b, t: (chunk(b, t), 0)),
                pl.BlockSpec((1, LANES), lambda b, t: (0, 0)),
                pl.BlockSpec((1, LANES), lambda b, t: (0, 0)),
                pl.BlockSpec((q, q), lambda b, t: (0, 0))]
    args = [xbc_act, dt_raw, dt_bias.reshape(1, LANES), a_log.reshape(1, LANES), tri]
    if not zero_init:
        in_specs.append(pl.BlockSpec((1, 1, SSD_HEADS, SSD_HEAD_DIM, SSD_STATE),
                                     lambda b, t: (b, direction, 0, 0, 0)))
        args.append(h0)
    aliases = {}
    if y_into is not None:
        in_specs.append(pl.BlockSpec(memory_space=pl.ANY))
        aliases = {len(args): 0}
        args.append(y_into)
    if fin_into is not None:
        in_specs.append(pl.BlockSpec(memory_space=pl.ANY))
        aliases[len(args)] = 1
        args.append(fin_into)
    return pl.pallas_call(
        functools.partial(_ssd_kernel, direction=direction, zero_init=zero_init, nt=nt),
        grid=(nb, nt),
        in_specs=in_specs,
        out_specs=[pl.BlockSpec((q, SSD_D_INNER), lambda b, t: (chunk(b, t), 0)),
                   pl.BlockSpec((1, None, SSD_HEADS, SSD_HEAD_DIM, SSD_STATE),
                                lambda b, t: (b, direction, 0, 0, 0))],
        out_shape=[jax.ShapeDtypeStruct((xbc_act.shape[0], SSD_D_INNER), BF16),
                   jax.ShapeDtypeStruct((nb, 2, SSD_HEADS, SSD_HEAD_DIM, SSD_STATE), F32)],
        scratch_shapes=[pltpu.VMEM((SSD_GROUPS, SSD_STATE, GROUP_W), F32)],
        input_output_aliases=aliases,
        compiler_params=_params("parallel", "arbitrary"),
        name=name,
    )(*args)


def _ssd_post_kernel(yf_ref, yb_ref, xs_ref, z_ref, d_ref, nw_ref, o_ref):
    z = z_ref[...].astype(F32)
    y = yf_ref[...].astype(F32) + yb_ref[...].astype(F32) + d_ref[...] * xs_ref[...].astype(F32)
    y = y * (z / (1.0 + jnp.exp(-z)))
    outs = []
    for g in range(SSD_GROUPS):
        yg = y[:, g * GROUP_W:(g + 1) * GROUP_W]
        ms = jnp.mean(yg * yg, axis=-1, keepdims=True)
        outs.append(yg * lax.rsqrt(ms + NORM_EPS) * nw_ref[:, g * GROUP_W:(g + 1) * GROUP_W])
    o_ref[...] = jnp.concatenate(outs, axis=1).astype(o_ref.dtype)


def _ssd_post_call(y_f, y_b, xbc_act, z, d_tot, norm_w, *, tm, name):
    m = y_f.shape[0]
    di = SSD_D_INNER
    row = pl.BlockSpec((tm, di), lambda i: (i, 0))
    vec = pl.BlockSpec((1, di), lambda i: (0, 0))
    return pl.pallas_call(
        _ssd_post_kernel,
        grid=(m // tm,),
        in_specs=[row, row, row, row, vec, vec],
        out_specs=row,
        out_shape=jax.ShapeDtypeStruct((m, di), BF16),
        compiler_params=_params("parallel"),
        name=name,
    )(y_f, y_b, xbc_act, z, d_tot, norm_w.reshape(1, di))


def _run_schedule(block_e, n_used):
    nblk = block_e.shape[0]
    idx = jnp.arange(nblk, dtype=jnp.int32)
    prev = jnp.concatenate([block_e[:1] - 1, block_e[:-1]])
    first = ((idx < n_used[0]) & (block_e != prev)).astype(jnp.int32)
    ridx = jnp.cumsum(first) - 1
    is_next = (first[None, :] == 1) & (ridx[None, :] == ridx[:, None] + 1)
    next_e = jnp.where(jnp.any(is_next, axis=1), jnp.sum(jnp.where(is_next, block_e[None, :], 0), axis=1), -1)
    counts = jnp.stack([n_used[0], jnp.sum(first)])
    return (block_e, counts.astype(jnp.int32), first, ridx.astype(jnp.int32), next_e.astype(jnp.int32))


def _weight_copies(w_hbm, stage, sems, layer, e, jj, slot, col_tiles, tn):
    return [pltpu.make_async_copy(w_hbm.at[layer, e, :, pl.ds(pl.multiple_of((off + jj) * tn, tn), tn)],
                                  stage.at[slot, wi], sems.at[slot, wi])
            for wi, off in enumerate(col_tiles)]


def _stage_run_weights(sched, w_hbm, stage, sems, w_bf, *, layer, col_tiles, tn, nj):
    be_ref, cnt_ref, first_ref, ridx_ref, next_ref = sched
    j = pl.program_id(0)
    i = pl.program_id(1)
    copies = functools.partial(_weight_copies, w_hbm, stage, sems, layer, col_tiles=col_tiles, tn=tn)

    @pl.when(first_ref[i] == 1)
    def _():
        slot = (j * cnt_ref[1] + ridx_ref[i]) & 1
        e = be_ref[i]

        @pl.when((i == 0) & (j == 0))
        def _():
            for c in copies(e, j, slot):
                c.start()

        for c in copies(e, j, slot):
            c.wait()
        def cast_rows(r, carry):
            rows = pl.ds(pl.multiple_of(r * CAST_ROWS, CAST_ROWS), CAST_ROWS)
            for wi in range(len(col_tiles)):
                w_bf[wi, rows, :] = stage[slot, wi, rows, :].astype(BF16)
            return carry

        lax.fori_loop(0, w_bf.shape[1] // CAST_ROWS, cast_rows, 0)
        nxt = next_ref[i]

        @pl.when(nxt >= 0)
        def _():
            for c in copies(nxt, j, 1 - slot):
                c.start()

        @pl.when((nxt < 0) & (j + 1 < nj))
        def _():
            for c in copies(be_ref[0], j + 1, 1 - slot):
                c.start()


def _moe_up_kernel(*refs, layer, tn, nj):
    sched, (x_ref, w_hbm, bg_ref, bu_ref, o_ref, stage, w_bf, sems) = refs[:5], refs[5:]
    i = pl.program_id(1)
    _stage_run_weights(sched, w_hbm, stage, sems, w_bf, layer=layer, col_tiles=(0, nj), tn=tn, nj=nj)
    nu_ref = sched[1]

    @pl.when(i < nu_ref[0])
    def _():
        x = x_ref[...]
        g = jnp.dot(x, w_bf[0], preferred_element_type=F32) + bg_ref[...]
        u = jnp.dot(x, w_bf[1], preferred_element_type=F32) + bu_ref[...]
        g = jnp.minimum(g, SWIGLU_LIMIT)
        u = jnp.clip(u, -SWIGLU_LIMIT, SWIGLU_LIMIT)
        act = (u + 1.0) * (g / (1.0 + jnp.exp2((-SWIGLU_ALPHA * LOG2_E) * g)))
        o_ref[...] = act.astype(o_ref.dtype)

    @pl.when(i >= nu_ref[0])
    def _():
        o_ref[...] = jnp.zeros_like(o_ref)


def _moe_down_kernel(*refs, layer, tn, nj):
    sched, (a_ref, w_hbm, b_ref), (o_ref, stage, w_bf, sems) = refs[:5], refs[5:8], refs[-4:]
    i = pl.program_id(1)
    _stage_run_weights(sched, w_hbm, stage, sems, w_bf, layer=layer, col_tiles=(0,), tn=tn, nj=nj)
    nu_ref = sched[1]

    @pl.when(i < nu_ref[0])
    def _():
        y = jnp.dot(a_ref[...], w_bf[0], preferred_element_type=F32) + b_ref[...]
        o_ref[...] = y.astype(o_ref.dtype)

    @pl.when(i >= nu_ref[0])
    def _():
        o_ref[...] = jnp.zeros_like(o_ref)


def _moe_experts(x_sorted, block_e, n_used, y_into, layer, w_gate_up, b_gate_up, w_down, b_down, *, blk_off,
                 total_slots, tn_up, tn_down, name):
    slots, d = x_sorted.shape
    nblk = slots // MOE_BLOCK
    nj = D_FF // tn_up
    nl = w_gate_up.shape[0]
    b_gu = b_gate_up.reshape(nl, N_EXPERTS, 1, 2 * D_FF)
    sched = _run_schedule(block_e, n_used)
    act = pl.pallas_call(
        functools.partial(_moe_up_kernel, layer=layer, tn=tn_up, nj=nj),
        grid_spec=pltpu.PrefetchScalarGridSpec(
            num_scalar_prefetch=len(sched),
            grid=(nj, nblk),
            in_specs=[pl.BlockSpec((MOE_BLOCK, d), lambda j, i, *s: (i, 0)),
                      pl.BlockSpec(memory_space=pl.ANY),
                      pl.BlockSpec((None, None, 1, tn_up), lambda j, i, be, *s: (layer, be[i], 0, j)),
                      pl.BlockSpec((None, None, 1, tn_up), lambda j, i, be, *s: (layer, be[i], 0, nj + j))],
            out_specs=pl.BlockSpec((MOE_BLOCK, tn_up), lambda j, i, *s: (i, j)),
            scratch_shapes=[pltpu.VMEM((2, 2, d, tn_up), F32), pltpu.VMEM((2, d, tn_up), BF16),
                            pltpu.SemaphoreType.DMA((2, 2))]),
        out_shape=jax.ShapeDtypeStruct((slots, D_FF), BF16),
        compiler_params=_params("arbitrary", "arbitrary"),
        name=name + "_up",
    )(*sched, x_sorted, w_gate_up, b_gu, b_gu)
    nj2 = d // tn_down
    in_specs = [pl.BlockSpec((MOE_BLOCK, D_FF), lambda j, i, *s: (i, 0)),
                pl.BlockSpec(memory_space=pl.ANY),
                pl.BlockSpec((None, None, 1, tn_down), lambda j, i, be, *s: (layer, be[i], 0, j))]
    args = [*sched, act, w_down, b_down.reshape(nl, N_EXPERTS, 1, d)]
    aliases = {}
    if y_into is not None:
        in_specs.append(pl.BlockSpec(memory_space=pl.ANY))
        aliases = {len(args): 0}
        args.append(y_into)
    return pl.pallas_call(
        functools.partial(_moe_down_kernel, layer=layer, tn=tn_down, nj=nj2),
        grid_spec=pltpu.PrefetchScalarGridSpec(
            num_scalar_prefetch=len(sched),
            grid=(nj2, nblk),
            in_specs=in_specs,
            out_specs=pl.BlockSpec((MOE_BLOCK, tn_down), lambda j, i, *s: (i + blk_off, j)),
            scratch_shapes=[pltpu.VMEM((2, 1, D_FF, tn_down), F32), pltpu.VMEM((1, D_FF, tn_down), BF16),
                            pltpu.SemaphoreType.DMA((2, 1))]),
        out_shape=jax.ShapeDtypeStruct((total_slots, d), BF16),
        input_output_aliases=aliases,
        compiler_params=_params("arbitrary", "arbitrary"),
        name=name + "_down",
    )(*args)


def _combine_kernel(*refs):
    y_refs = refs[:TOP_K]
    gate_ref, x_ref, g2_ref = refs[TOP_K:TOP_K + 3]
    o_ref = refs[-1]
    gates = gate_ref[...]
    f = None
    for k in range(TOP_K):
        term = gates[:, k:k + 1] * y_refs[k][...].astype(F32)
        f = term if f is None else f + term
    o_ref[...] = x_ref[...] + g2_ref[0] * f


def _combine_rows_call(y4, gate_pad, x, mod, out_into, *, row_off, m, out_rows, out_off, n_p, ls, tm, name):
    d = x.shape[1]
    nt = m // tm
    xoff = row_off // tm
    ooff = out_off // tm
    y_specs = [pl.BlockSpec((tm, d), functools.partial(lambda i, j, k: (k * nt + i, 0), k=k))
               for k in range(TOP_K)]
    in_specs = y_specs + [
        pl.BlockSpec((tm, LANES), lambda i, j: (i + xoff, 0)),
        pl.BlockSpec((tm, d), lambda i, j: (i + xoff, 0)),
        pl.BlockSpec((1, 1, d), lambda i, j: (_group_of_tile(i + xoff, tm, n_p, ls) * 6 + 5, 0, 0))]
    args = [y4] * TOP_K + [gate_pad, x, mod]
    aliases = {}
    if out_into is not None:
        in_specs.append(pl.BlockSpec(memory_space=pl.ANY))
        aliases = {len(args): 0}
        args.append(out_into)
    return pl.pallas_call(
        _combine_kernel,
        grid=(nt, 1),
        in_specs=in_specs,
        out_specs=pl.BlockSpec((tm, d), lambda i, j: (i + ooff, 0)),
        out_shape=jax.ShapeDtypeStruct((out_rows, d), F32),
        input_output_aliases=aliases,
        compiler_params=_params("parallel", "arbitrary"),
        name=name,
    )(*args)


def _moe_layer(x, mod, n_p, ls, layer, norm_w, w_router, b_router, w_gate_up, b_gate_up, w_down, b_down, *,
               split, name):
    n, d = x.shape
    h, idx_pad, gate_pad, rank_pad, cnt = _router_call(x, norm_w.reshape(1, d), mod, n_p, ls, w_router,
                                                       b_router, tm=256, name=name + "_router")
    n_assign = n * TOP_K
    flat_e = idx_pad[:, :TOP_K].reshape(-1)
    counts = cnt[0, :N_EXPERTS]
    padded = (counts + MOE_BLOCK - 1) // MOE_BLOCK * MOE_BLOCK
    pad_end = jnp.cumsum(padded)
    pad_start = pad_end - padded
    dest = pad_start[flat_e] + rank_pad[:, :TOP_K].reshape(-1)
    n_blocks = -(-n_assign // MOE_BLOCK) + N_EXPERTS
    slots = n_blocks * MOE_BLOCK
    slot_tok = (jnp.arange(slots, dtype=jnp.int32) % n).at[dest].set(
        jnp.arange(n_assign, dtype=jnp.int32) // TOP_K, unique_indices=True, mode="promise_in_bounds")
    block_start = jnp.arange(n_blocks, dtype=jnp.int32) * MOE_BLOCK
    block_e = jnp.minimum(jnp.sum((pad_end[None, :] <= block_start[:, None]).astype(jnp.int32), axis=1),
                          N_EXPERTS - 1)
    n_used = (pad_end[-1] // MOE_BLOCK).astype(jnp.int32).reshape(1)
    cb = n_blocks // MOE_RANGES
    y_sorted = None
    for c in range(MOE_RANGES):
        rows = slice(c * cb * MOE_BLOCK, (c + 1) * cb * MOE_BLOCK)
        x_sorted = h.at[slot_tok[rows]].get(mode="promise_in_bounds")
        used = jnp.clip(n_used - c * cb, 0, cb)
        y_sorted = _moe_experts(x_sorted, block_e[c * cb:(c + 1) * cb], used, y_sorted, layer, w_gate_up,
                                b_gate_up, w_down, b_down, blk_off=c * cb, total_slots=slots, tn_up=1024,
                                tn_down=2048, name="%s_r%d" % (name, c))
    n_s = n - n_p
    ranges = [(0, n_p), (n_p, n_s // 2), (n_p + n_s // 2, n_s - n_s // 2)]
    dest_tk = dest.reshape(n, TOP_K)
    outs = [None, None]
    for ri, (t0, m) in enumerate(ranges):
        idx = dest_tk[t0:t0 + m].T.reshape(-1)
        y4 = y_sorted.at[idx].get(mode="promise_in_bounds")
        if split:
            which, out_rows, out_off = (0, n_p, 0) if ri == 0 else (1, n_s, t0 - n_p)
        else:
            which, out_rows, out_off = 0, n, t0
        outs[which] = _combine_rows_call(y4, gate_pad, x, mod, outs[which], row_off=t0, m=m,
                                         out_rows=out_rows, out_off=out_off, n_p=n_p, ls=ls, tm=256,
                                         name="%s_combine%d" % (name, ri))
    return (outs[0], outs[1]) if split else outs[0]


def _ada_call(cond, w_ada, b_ada, layer, *, name):
    g = cond.shape[0]
    a = jnp.pad(jax.nn.silu(cond), ((0, 16 - g), (0, 0))).astype(BF16)
    m = _matmul(a, w_ada, n_out=6 * D_MODEL, tm=16, tn=1024, out_dtype=F32, name=name, layer=layer,
                bias=b_ada.reshape(b_ada.shape[0], 1, 6 * D_MODEL))
    return m[:g].reshape(g * 6, 1, D_MODEL)


def kernel(x_prompt, x_sample, c, c_ctx, cache_k, cache_v, state_ssm, norm1_w, norm2_w, w_ada, b_ada, w_qkv, q_norm_w, k_norm_w, lambda_q1, lambda_k1, lambda_q2, lambda_k2, subln_w, w_o, w_in_ssd, conv_w, conv_b, dt_bias, a_log, d_skip, ssd_norm_w, w_out_ssd, w_router, b_router, w_gate_up, b_gate_up, w_down, b_down):
    bp, lp, d = x_prompt.shape
    bs, ls, _ = x_sample.shape
    past = cache_k.shape[2]
    n_p, n_s = bp * lp, bs * ls
    xp0, xs0 = x_prompt.reshape(n_p, d), x_sample.reshape(n_s, d)
    cond = jnp.concatenate([c_ctx[None], c], axis=0)
    resid_of = lambda xx, mod, which: (xx, mod, which, n_p, ls)

    n = n_p + n_s
    mod = _ada_call(cond, w_ada, b_ada, 0, name="ada0")
    wq = w_qkv[0].astype(BF16)
    qk_w = jnp.stack([q_norm_w[0], k_norm_w[0]], axis=0)
    lam_init = 0.8 - 0.6 * math.exp(-0.3 * 0)
    lam_params = jnp.stack([lambda_q1[0], lambda_k1[0], lambda_q2[0], lambda_k2[0]], axis=0)
    qp, kp, vp = _qkv_call(xp0, norm1_w[0], mod, wq, qk_w, None, row_off=0, n_p=n_p, kv_dtype=F32, tm=512,
                           tn=512, ls=ls, name="qkv_prompt")
    qs, ks, vs = _qkv_call(xs0, norm1_w[0], mod, wq, qk_w, _rope_tables(ls), row_off=n_p, n_p=n_p,
                           kv_dtype=BF16, tm=512, tn=512, ls=ls, name="qkv_sample")
    o = _attn_call(qp, [(kp, vp, lp)], lam_params, subln_w[0], None, n_rows=n, row_off=0, nb=bp, lq=lp,
                   tq=lp, lam_init=lam_init, name="attn_prompt")
    ck = cache_k[:, 0].reshape(bs * past, d)
    cv = cache_v[:, 0].reshape(bs * past, d)
    o = _attn_call(qs, [(ck, cv, past), (ks, vs, ls)], lam_params, subln_w[0], o, n_rows=n, row_off=n_p,
                   nb=bs, lq=ls, tq=256, lam_init=lam_init, name="attn_sample")
    wo = w_o[0].astype(BF16)
    x = _resid_rows_matmul(o, wo, xp0, mod, 2, None, row_off=0, n_p=n_p, ls=ls, tm=1024, tn=512,
                           name="attn_out_prompt")
    x = _resid_rows_matmul(o, wo, xs0, mod, 2, x, row_off=n_p, n_p=n_p, ls=ls, tm=1024, tn=512,
                           name="attn_out_sample")
    x = _moe_layer(x, mod, n_p, ls, 0, norm2_w[0], w_router[0], b_router[0], w_gate_up, b_gate_up,
                   w_down, b_down, split=False, name="moe0")
    new_k = kp.reshape(bp, 1, lp, ATTN_HEADS, 2, QK_DIM)
    new_v = vp.reshape(bp, 1, lp, ATTN_HEADS, V_DIM)

    mod = _ada_call(cond, w_ada, b_ada, 1, name="ada1")
    h = _modnorm_call(x, norm1_w[1].reshape(1, d), mod, 0, n_p, ls, tm=256, name="norm1_1")
    w_in = w_in_ssd[0].astype(BF16)
    z = _matmul(h, w_in, n_out=SSD_D_INNER, tm=1024, tn=512, out_dtype=BF16, name="ssd_in_z")
    xbc = _matmul(h, w_in, n_out=SSD_CONV_CH, col_off=SSD_D_INNER, tm=1024, tn=512, out_dtype=BF16,
                  name="ssd_in_xbc")
    dt_raw = _matmul(h, w_in, n_out=2 * SSD_HEADS, col_off=SSD_D_INNER + SSD_CONV_CH, tm=1024, tn=LANES,
                     out_dtype=F32, name="ssd_in_dt")
    xbc_act = _conv_call(xbc, conv_w[0], conv_b[0], n_p=n_p, lp=lp, ls=ls, tc=256, tcn=2048, name="ssd_conv")
    ys, fin = [], None
    for direction in (0, 1):
        y, fin = _ssd_scan_call(xbc_act, dt_raw, dt_bias[0], a_log[0], None, None, fin, nb=bp, seq=lp,
                                row_off=0, direction=direction, name="ssd_scan_prompt%d" % direction)
        y, _ = _ssd_scan_call(xbc_act, dt_raw, dt_bias[0], a_log[0], state_ssm[:, 0], y, None, nb=bs, seq=ls,
                              row_off=n_p, direction=direction, name="ssd_scan_sample%d" % direction)
        ys.append(y)
    d_tot = jnp.repeat(d_skip[0, 0] + d_skip[0, 1], SSD_HEAD_DIM).reshape(1, SSD_D_INNER)
    yn = _ssd_post_call(ys[0], ys[1], xbc_act, z, d_tot, ssd_norm_w[0], tm=256, name="ssd_post")
    x = _matmul(yn, w_out_ssd[0].astype(BF16), n_out=d, tm=1024, tn=512, out_dtype=F32, name="ssd_out",
                resid=resid_of(x, mod, 2))
    xp, xs = _moe_layer(x, mod, n_p, ls, 1, norm2_w[1], w_router[1], b_router[1], w_gate_up, b_gate_up,
                        w_down, b_down, split=True, name="moe1")
    new_s = fin[:, None]
    return (xp.reshape(bp, lp, d), xs.reshape(bs, ls, d), new_k, new_v, new_s)
```

```python
import functools
import math

import jax
import jax.numpy as jnp
from jax import lax
from jax.experimental import pallas as pl
from jax.experimental.pallas import tpu as pltpu

F32 = jnp.float32
BF16 = jnp.bfloat16

D_MODEL = 2048
NORM_EPS = 1e-6
GRID_W = 64
ROPE_THETA = 10000.0
ATTN_HEADS = 8
QK_DIM = 128
V_DIM = 256
SSD_D_INNER = 4096
SSD_HEAD_DIM = 64
SSD_HEADS = 64
SSD_GROUPS = 8
SSD_STATE = 128
SSD_CONV = 5
SSD_CHUNK = 128
SSD_BC_DIM = SSD_GROUPS * SSD_STATE
SSD_CONV_CH = SSD_D_INNER + 2 * SSD_BC_DIM
N_EXPERTS = 32
TOP_K = 4
D_FF = 2048
SWIGLU_LIMIT = 7.0
SWIGLU_ALPHA = 1.702
MOE_BLOCK = 512
MOE_RANGES = 4
ATTN_KEY_CHUNK = 512
CAST_ROWS = 256
LOG2_E = math.log2(math.e)
LANES = 128
GROUP_W = SSD_D_INNER // SSD_GROUPS
assert SSD_CHUNK == SSD_STATE == LANES and 2 * SSD_HEAD_DIM == LANES
VMEM_LIMIT = 56 * 1024 * 1024


def _params(*sem):
    return pltpu.CompilerParams(dimension_semantics=sem, vmem_limit_bytes=VMEM_LIMIT)


def _group_of_tile(i, tm, n_p, ls):
    return jnp.where(i * tm < n_p, 0, 1 + (i * tm - n_p) // ls)


def _mod_spec(which, tm, tn, n_p, ls):
    return pl.BlockSpec((1, 1, tn), lambda i, j: (_group_of_tile(i, tm, n_p, ls) * 6 + which, 0, j))


def _mm_kernel(a_ref, w_ref, *rest, epilogue):
    acc = jnp.dot(a_ref[...].astype(BF16), w_ref[...].astype(BF16), preferred_element_type=F32)
    if epilogue == "bias":
        b_ref, o_ref = rest
        o_ref[...] = (acc + b_ref[...]).astype(o_ref.dtype)
    elif epilogue == "resid":
        x_ref, g_ref, o_ref = rest[0], rest[1], rest[-1]
        o_ref[...] = x_ref[...] + g_ref[0] * acc
    else:
        (o_ref,) = rest
        o_ref[...] = acc.astype(o_ref.dtype)


def _matmul(a, w, *, n_out, col_off=0, tm, tn, out_dtype, name, layer=None, bias=None, resid=None):
    m, k = a.shape
    joff = col_off // tn
    if layer is None:
        w_spec = pl.BlockSpec((k, tn), lambda i, j: (0, j + joff))
        b_spec = pl.BlockSpec((1, tn), lambda i, j: (0, j))
    else:
        w_spec = pl.BlockSpec((None, k, tn), lambda i, j: (layer, 0, j + joff))
        b_spec = pl.BlockSpec((None, 1, tn), lambda i, j: (layer, 0, j))
    in_specs = [pl.BlockSpec((tm, k), lambda i, j: (i, 0)), w_spec]
    args = [a, w]
    if bias is not None:
        epilogue = "bias"
        in_specs.append(b_spec)
        args.append(bias)
    elif resid is not None:
        epilogue = "resid"
        x, mod, which, n_p, ls = resid
        in_specs += [pl.BlockSpec((tm, tn), lambda i, j: (i, j)), _mod_spec(which, tm, tn, n_p, ls)]
        args += [x, mod]
    else:
        epilogue = "plain"
    return pl.pallas_call(
        functools.partial(_mm_kernel, epilogue=epilogue),
        grid=(m // tm, n_out // tn),
        in_specs=in_specs,
        out_specs=pl.BlockSpec((tm, tn), lambda i, j: (i, j)),
        out_shape=jax.ShapeDtypeStruct((m, n_out), out_dtype),
        compiler_params=_params("parallel", "arbitrary"),
        name=name,
    )(*args)


def _resid_rows_matmul(a_table, w, x_rows, mod, which, out_into, *, row_off, n_p, ls, tm, tn, name):
    n_rows, k = a_table.shape
    m, n_out = x_rows.shape
    ioff = row_off // tm
    in_specs = [pl.BlockSpec((tm, k), lambda i, j: (i + ioff, 0)),
                pl.BlockSpec((k, tn), lambda i, j: (0, j)),
                pl.BlockSpec((tm, tn), lambda i, j: (i, j)),
                pl.BlockSpec((1, 1, tn), lambda i, j: (_group_of_tile(i + ioff, tm, n_p, ls) * 6 + which, 0, j))]
    args = [a_table, w, x_rows, mod]
    aliases = {}
    if out_into is not None:
        in_specs.append(pl.BlockSpec(memory_space=pl.ANY))
        aliases = {len(args): 0}
        args.append(out_into)
    return pl.pallas_call(
        functools.partial(_mm_kernel, epilogue="resid"),
        grid=(m // tm, n_out // tn),
        in_specs=in_specs,
        out_specs=pl.BlockSpec((tm, tn), lambda i, j: (i + ioff, j)),
        out_shape=jax.ShapeDtypeStruct((n_rows, n_out), F32),
        input_output_aliases=aliases,
        compiler_params=_params("parallel", "arbitrary"),
        name=name,
    )(*args)


def _modnorm(x, w, shift, scale):
    ms = jnp.mean(x * x, axis=-1, keepdims=True)
    return (x * lax.rsqrt(ms + NORM_EPS) * w) * (1.0 + scale) + shift


def _modnorm_kernel(x_ref, w_ref, sh_ref, sc_ref, o_ref):
    o_ref[...] = _modnorm(x_ref[...], w_ref[...], sh_ref[0], sc_ref[0]).astype(o_ref.dtype)


def _modnorm_call(x, w, mod, which_shift, n_p, ls, *, tm, name):
    m, d = x.shape
    return pl.pallas_call(
        _modnorm_kernel,
        grid=(m // tm, 1),
        in_specs=[pl.BlockSpec((tm, d), lambda i, j: (i, 0)),
                  pl.BlockSpec((1, d), lambda i, j: (0, 0)),
                  _mod_spec(which_shift, tm, d, n_p, ls),
                  _mod_spec(which_shift + 1, tm, d, n_p, ls)],
        out_specs=pl.BlockSpec((tm, d), lambda i, j: (i, 0)),
        out_shape=jax.ShapeDtypeStruct((m, d), BF16),
        compiler_params=_params("parallel", "arbitrary"),
        name=name,
    )(x, w, mod, mod)


def _split2(v):
    hi = v.astype(BF16)
    lo = (v - hi.astype(F32)).astype(BF16)
    return hi, lo


def _router_kernel(x_ref, w_ref, sh_ref, sc_ref, wr_ref, br_ref, h_ref, idx_ref, gate_ref, rank_ref, cnt_ref,
                   run_ref):
    @pl.when(pl.program_id(0) == 0)
    def _():
        run_ref[...] = jnp.zeros_like(run_ref)

    h = _modnorm(x_ref[...], w_ref[...], sh_ref[0], sc_ref[0])
    h_ref[...] = h.astype(BF16)
    h_hi, h_lo = _split2(h)
    w_hi, w_lo = _split2(wr_ref[...])
    logits = (jnp.dot(h_hi, w_hi, preferred_element_type=F32)
              + jnp.dot(h_hi, w_lo, preferred_element_type=F32)
              + jnp.dot(h_lo, w_hi, preferred_element_type=F32)) + br_ref[...]
    lane = lax.broadcasted_iota(jnp.int32, logits.shape, 1).astype(F32)
    cur = jnp.where(lane < N_EXPERTS, logits, -jnp.inf)
    vals, idxs = [], []
    for _ in range(TOP_K):
        m = jnp.max(cur, axis=-1, keepdims=True)
        am = jnp.min(jnp.where(cur == m, lane, float(LANES)), axis=-1, keepdims=True)
        vals.append(m)
        idxs.append(am)
        cur = jnp.where(lane == am, -jnp.inf, cur)
    exps = [jnp.exp(v - vals[0]) for v in vals]
    denom = exps[0] + exps[1] + exps[2] + exps[3]
    idx_out = jnp.zeros(logits.shape, F32)
    gate_out = jnp.zeros(logits.shape, F32)
    for k in range(TOP_K):
        idx_out = jnp.where(lane == k, idxs[k], idx_out)
        gate_out = jnp.where(lane == k, exps[k] / denom, gate_out)
    idx_ref[...] = idx_out.astype(jnp.int32)
    gate_ref[...] = gate_out
    tm = logits.shape[0]
    chosen = functools.reduce(jnp.logical_or, [lane == am for am in idxs])
    onehot = jnp.where(chosen, 1.0, 0.0)
    ti = lax.broadcasted_iota(jnp.int32, (tm, tm), 0)
    tj = lax.broadcasted_iota(jnp.int32, (tm, tm), 1)
    earlier = jnp.where(ti > tj, 1.0, 0.0).astype(BF16)
    base = run_ref[...] + jnp.dot(earlier, onehot.astype(BF16), preferred_element_type=F32)
    rank_out = jnp.zeros(logits.shape, F32)
    for k in range(TOP_K):
        rk = jnp.sum(jnp.where(lane == idxs[k], base, 0.0), axis=-1, keepdims=True)
        rank_out = jnp.where(lane == k, rk, rank_out)
    rank_ref[...] = rank_out.astype(jnp.int32)
    run_ref[...] = run_ref[...] + jnp.sum(onehot, axis=0, keepdims=True)
    cnt_ref[...] = run_ref[...].astype(jnp.int32)


def _router_call(x, w, mod, n_p, ls, w_router, b_router, *, tm, name):
    m, d = x.shape
    wr = jnp.pad(w_router, ((0, 0), (0, LANES - N_EXPERTS)))
    br = jnp.pad(b_router, (0, LANES - N_EXPERTS)).reshape(1, LANES)
    return pl.pallas_call(
        _router_kernel,
        grid=(m // tm, 1),
        in_specs=[pl.BlockSpec((tm, d), lambda i, j: (i, 0)),
                  pl.BlockSpec((1, d), lambda i, j: (0, 0)),
                  _mod_spec(3, tm, d, n_p, ls),
                  _mod_spec(4, tm, d, n_p, ls),
                  pl.BlockSpec((d, LANES), lambda i, j: (0, 0)),
                  pl.BlockSpec((1, LANES), lambda i, j: (0, 0))],
        out_specs=[pl.BlockSpec((tm, d), lambda i, j: (i, 0)),
                   pl.BlockSpec((tm, LANES), lambda i, j: (i, 0)),
                   pl.BlockSpec((tm, LANES), lambda i, j: (i, 0)),
                   pl.BlockSpec((tm, LANES), lambda i, j: (i, 0)),
                   pl.BlockSpec((1, LANES), lambda i, j: (0, 0))],
        out_shape=[jax.ShapeDtypeStruct((m, d), BF16),
                   jax.ShapeDtypeStruct((m, LANES), jnp.int32),
                   jax.ShapeDtypeStruct((m, LANES), F32),
                   jax.ShapeDtypeStruct((m, LANES), jnp.int32),
                   jax.ShapeDtypeStruct((1, LANES), jnp.int32)],
        scratch_shapes=[pltpu.VMEM((1, LANES), F32)],
        compiler_params=_params("arbitrary", "arbitrary"),
        name=name,
    )(x, w, mod, mod, wr, br)


def _qkv_kernel(x_ref, n1_ref, sh_ref, sc_ref, w_ref, nw_ref, *rest, rope, tn):
    if rope:
        cos_ref, sin_ref, q_ref, k_ref, v_ref, h_s = rest
    else:
        q_ref, k_ref, v_ref, h_s = rest
    j = pl.program_id(1)
    nq = D_MODEL // tn

    @pl.when(j == 0)
    def _():
        h_s[...] = _modnorm(x_ref[...], n1_ref[...], sh_ref[0], sc_ref[0]).astype(BF16)

    acc = jnp.dot(h_s[...], w_ref[...], preferred_element_type=F32)

    def normed(widx, out_scale):
        w = nw_ref[widx:widx + 1, :]
        outs = []
        for c in range(tn // QK_DIM):
            xc = acc[:, c * QK_DIM:(c + 1) * QK_DIM]
            ms = jnp.mean(xc * xc, axis=-1, keepdims=True)
            y = xc * lax.rsqrt(ms + NORM_EPS) * w
            if rope:
                lane = lax.broadcasted_iota(jnp.int32, y.shape, 1)
                partner = jnp.where(lane % 64 < 32, pltpu.roll(y, QK_DIM - 32, 1), pltpu.roll(y, 32, 1))
                y = y * cos_ref[...] + partner * sin_ref[...]
            outs.append(y * out_scale if out_scale != 1.0 else y)
        return jnp.concatenate(outs, axis=1)

    @pl.when(j < nq)
    def _():
        q_ref[...] = normed(0, QK_DIM ** -0.5 * LOG2_E).astype(q_ref.dtype)

    @pl.when((j >= nq) & (j < 2 * nq))
    def _():
        k_ref[...] = normed(1, 1.0).astype(k_ref.dtype)

    @pl.when(j >= 2 * nq)
    def _():
        v_ref[...] = acc.astype(v_ref.dtype)


def _qkv_call(x_rows, norm1_w, mod, w_qkv, qk_norm_w, rope_tabs, *, row_off, n_p, kv_dtype, tm, tn, ls, name):
    m, d = x_rows.shape
    nq = d // tn
    ioff = row_off // tm
    rope = rope_tabs is not None

    def mod_spec(which):
        return pl.BlockSpec((1, 1, d), lambda i, j: (_group_of_tile(i + ioff, tm, n_p, ls) * 6 + which, 0, 0))

    in_specs = [pl.BlockSpec((tm, d), lambda i, j: (i, 0)),
                pl.BlockSpec((1, d), lambda i, j: (0, 0)),
                mod_spec(0), mod_spec(1),
                pl.BlockSpec((d, tn), lambda i, j: (0, j)),
                pl.BlockSpec((2, QK_DIM), lambda i, j: (0, 0))]
    args = [x_rows, norm1_w.reshape(1, d), mod, mod, w_qkv, qk_norm_w]
    if rope:
        nt = ls // tm
        in_specs += [pl.BlockSpec((tm, QK_DIM), lambda i, j: (i % nt, 0))] * 2
        args += list(rope_tabs)
    return pl.pallas_call(
        functools.partial(_qkv_kernel, rope=rope, tn=tn),
        grid=(m // tm, 3 * nq),
        in_specs=in_specs,
        out_specs=[pl.BlockSpec((tm, tn), lambda i, j: (i, jnp.minimum(j, nq - 1))),
                   pl.BlockSpec((tm, tn), lambda i, j: (i, jnp.clip(j - nq, 0, nq - 1))),
                   pl.BlockSpec((tm, tn), lambda i, j: (i, jnp.clip(j - 2 * nq, 0, nq - 1)))],
        out_shape=[jax.ShapeDtypeStruct((m, d), BF16),
                   jax.ShapeDtypeStruct((m, d), kv_dtype),
                   jax.ShapeDtypeStruct((m, d), kv_dtype)],
        scratch_shapes=[pltpu.VMEM((tm, d), BF16)],
        compiler_params=_params("parallel", "arbitrary"),
        name=name,
    )(*args)


def _rope_tables(ls):
    rows = ls // GRID_W
    row = jnp.repeat(jnp.arange(rows, dtype=F32), GRID_W)
    col = jnp.tile(jnp.arange(GRID_W, dtype=F32), rows)
    axis_dim = QK_DIM // 2
    inv = ROPE_THETA ** (-jnp.arange(0, axis_dim, 2, dtype=F32) / axis_dim)
    ar, ac = row[:, None] * inv, col[:, None] * inv
    cos = jnp.concatenate([jnp.cos(ar), jnp.cos(ar), jnp.cos(ac), jnp.cos(ac)], axis=1)
    sin = jnp.concatenate([-jnp.sin(ar), jnp.sin(ar), -jnp.sin(ac), jnp.sin(ac)], axis=1)
    return cos, sin


def _attn_kernel(lam_ref, subw_ref, q_ref, *rest, nseg, lam_init):
    k_refs, v_refs, o_ref = rest[:nseg], rest[nseg:2 * nseg], rest[-1]
    lp = lam_ref[...]
    lam = (jnp.exp(jnp.sum(lp[0:1] * lp[1:2], axis=-1, keepdims=True))
           - jnp.exp(jnp.sum(lp[2:3] * lp[3:4], axis=-1, keepdims=True)) + lam_init)
    q = q_ref[...]
    chunks = [(k_ref, v_ref, c0) for k_ref, v_ref in zip(k_refs, v_refs)
              for c0 in range(0, k_ref.shape[0], min(ATTN_KEY_CHUNK, k_ref.shape[0]))]
    outs = []
    for mi in range(2):
        qm = q[:, mi * QK_DIM:(mi + 1) * QK_DIM]
        mx = den = pv = None
        for k_ref, v_ref, c0 in chunks:
            kc = min(ATTN_KEY_CHUNK, k_ref.shape[0])
            k = k_ref[c0:c0 + kc, mi * QK_DIM:(mi + 1) * QK_DIM].astype(BF16)
            v = v_ref[c0:c0 + kc, :].astype(BF16)
            s = lax.dot_general(qm, k, (((1,), (1,)), ((), ())), preferred_element_type=F32)
            cmax = jnp.max(s, axis=-1, keepdims=True)
            if mx is None:
                mx = cmax
                p = jnp.exp2(s - mx)
                den = jnp.sum(p, axis=-1, keepdims=True)
                pv = jnp.dot(p.astype(BF16), v, preferred_element_type=F32)
            else:
                new_mx = jnp.maximum(mx, cmax)
                alpha = jnp.exp2(mx - new_mx)
                p = jnp.exp2(s - new_mx)
                den = den * alpha + jnp.sum(p, axis=-1, keepdims=True)
                pv = pv * alpha + jnp.dot(p.astype(BF16), v, preferred_element_type=F32)
                mx = new_mx
        outs.append(pv * (1.0 / den))
    acc = outs[0] - lam * outs[1]
    ms = jnp.mean(acc * acc, axis=-1, keepdims=True)
    o = acc * lax.rsqrt(ms + NORM_EPS) * subw_ref[...] * (1.0 - lam_init)
    o_ref[...] = o.astype(o_ref.dtype)


def _attn_call(q, kv_segs, lam_params, subln_w, o_into, *, n_rows, row_off, nb, lq, tq, lam_init, name):
    nqb = lq // tq
    nseg = len(kv_segs)
    ooff = row_off // tq
    in_specs = [pl.BlockSpec((4, QK_DIM), lambda b, h, i: (0, 0)),
                pl.BlockSpec((1, V_DIM), lambda b, h, i: (0, 0)),
                pl.BlockSpec((tq, V_DIM), lambda b, h, i: (b * nqb + i, h))]
    in_specs += [pl.BlockSpec((lk, V_DIM), lambda b, h, i: (b, h)) for (_, _, lk) in kv_segs] * 2
    args = [lam_params, subln_w.reshape(1, V_DIM), q]
    args += [k for (k, _, _) in kv_segs] + [v for (_, v, _) in kv_segs]
    aliases = {}
    if o_into is not None:
        in_specs.append(pl.BlockSpec(memory_space=pl.ANY))
        aliases = {len(args): 0}
        args.append(o_into)
    return pl.pallas_call(
        functools.partial(_attn_kernel, nseg=nseg, lam_init=lam_init),
        grid=(nb, ATTN_HEADS, nqb),
        in_specs=in_specs,
        out_specs=pl.BlockSpec((tq, V_DIM), lambda b, h, i: (ooff + b * nqb + i, h)),
        out_shape=jax.ShapeDtypeStruct((n_rows, q.shape[1]), BF16),
        input_output_aliases=aliases,
        compiler_params=_params("parallel", "parallel", "arbitrary"),
        name=name,
    )(*args)


def _conv_kernel(xp_ref, x_ref, xn_ref, w_ref, b_ref, shift_ref, o_ref, *, tc, n_p, lp, ls, halo):
    i = pl.program_id(0)
    row0 = i * tc
    in_prompt = row0 < n_p
    first = jnp.where(in_prompt, row0 % lp == 0, (row0 - n_p) % ls == 0)
    last = jnp.where(in_prompt, (row0 + tc) % lp == 0, (row0 + tc - n_p) % ls == 0)
    x = x_ref[...]
    shifted = jnp.dot(shift_ref[...], x, preferred_element_type=F32)
    acc = b_ref[...] + w_ref[2:3, :] * x.astype(F32)
    for si, k in enumerate((0, 1, 3, 4)):
        acc = acc + w_ref[k:k + 1, :] * shifted[si * tc:(si + 1) * tc, :]
    prev = jnp.where(first, 0.0, xp_ref[...].astype(F32)[halo - 8:halo, :])
    nxt = jnp.where(last, 0.0, xn_ref[...].astype(F32)[0:8, :])
    r8 = lax.broadcasted_iota(jnp.int32, prev.shape, 0)
    top = (w_ref[0:1, :] * jnp.where(r8 < 2, pltpu.roll(prev, 2, 0), 0.0)
           + w_ref[1:2, :] * jnp.where(r8 < 1, pltpu.roll(prev, 1, 0), 0.0))
    bot = (w_ref[3:4, :] * jnp.where(r8 >= 7, pltpu.roll(nxt, 7, 0), 0.0)
           + w_ref[4:5, :] * jnp.where(r8 >= 6, pltpu.roll(nxt, 6, 0), 0.0))
    acc = acc + jnp.concatenate([top, jnp.zeros((tc - 16, acc.shape[1]), F32), bot], axis=0)
    o_ref[...] = (acc / (1.0 + jnp.exp2(-LOG2_E * acc))).astype(o_ref.dtype)


def _conv_call(xbc, conv_w, conv_b, *, n_p, lp, ls, tc, tcn, name):
    m, ch = xbc.shape
    halo = 16
    hb = tc // halo
    nhb = m // halo
    out_row = jnp.arange(tc)[None, :, None]
    in_row = jnp.arange(tc)[None, None, :]
    tap = jnp.array([0, 1, 3, 4])[:, None, None]
    shift = (in_row == out_row + tap - SSD_CONV // 2).astype(BF16).reshape(4 * tc, tc)
    return pl.pallas_call(
        functools.partial(_conv_kernel, tc=tc, n_p=n_p, lp=lp, ls=ls, halo=halo),
        grid=(m // tc, ch // tcn),
        in_specs=[pl.BlockSpec((halo, tcn), lambda i, j: (jnp.maximum(i * hb - 1, 0), j)),
                  pl.BlockSpec((tc, tcn), lambda i, j: (i, j)),
                  pl.BlockSpec((halo, tcn), lambda i, j: (jnp.minimum((i + 1) * hb, nhb - 1), j)),
                  pl.BlockSpec((SSD_CONV, tcn), lambda i, j: (0, j)),
                  pl.BlockSpec((1, tcn), lambda i, j: (0, j)),
                  pl.BlockSpec((4 * tc, tc), lambda i, j: (0, 0))],
        out_specs=pl.BlockSpec((tc, tcn), lambda i, j: (i, j)),
        out_shape=jax.ShapeDtypeStruct((m, ch), BF16),
        compiler_params=_params("parallel", "arbitrary"),
        name=name,
    )(xbc, xbc, xbc, conv_w, conv_b.reshape(1, ch), shift)


def _ssd_kernel(xbc_ref, dtraw_ref, dtb_ref, alog_ref, tri_ref, *rest, direction, zero_init, nt):
    y_ref, fin_ref, st_ref = rest[-3:]
    if not zero_init:
        h0_ref = rest[0]
    t = pl.program_id(1)
    q = SSD_CHUNK
    hpg = SSD_HEADS // SSD_GROUPS

    @pl.when(t == 0)
    def _():
        for g in range(SSD_GROUPS):
            if zero_init:
                st_ref[g] = jnp.zeros((SSD_STATE, GROUP_W), F32)
            else:
                hg = h0_ref[0, 0, g * hpg:(g + 1) * hpg].reshape(GROUP_W, SSD_STATE)
                st_ref[g] = hg.T

    x = dtraw_ref[...] + dtb_ref[...]
    dt = jnp.maximum(x, 0.0) + jnp.log1p(jnp.exp(-jnp.abs(x)))
    a = dt * (-jnp.exp(alog_ref[...]) * LOG2_E)
    tri = tri_ref[...]
    a1 = a.astype(BF16)
    r1 = a - a1.astype(F32)
    a2 = r1.astype(BF16)
    a3 = (r1 - a2.astype(F32)).astype(BF16)
    cs = (jnp.dot(tri, a1, preferred_element_type=F32) + jnp.dot(tri, a2, preferred_element_type=F32)
          + jnp.dot(tri, a3, preferred_element_type=F32))
    tot = cs[q - 1:q, :] if direction == 0 else cs[0:1, :]
    cdec = jnp.exp2(tot)
    cs_t = cs.T
    dt_t = dt.T
    wend_t = (dt * jnp.exp2(tot - cs)).T
    li = lax.broadcasted_iota(jnp.int32, (q, q), 0)
    si = lax.broadcasted_iota(jnp.int32, (q, q), 1)
    mask = (li >= si) if direction == 0 else (li <= si)
    low = lax.broadcasted_iota(jnp.int32, (q, LANES), 1) < SSD_HEAD_DIM
    low_row = low[0:1, :]

    for g in range(SSD_GROUPS):
        bg = xbc_ref[:, SSD_D_INNER + g * SSD_STATE:SSD_D_INNER + (g + 1) * SSD_STATE]
        cg = xbc_ref[:, SSD_D_INNER + SSD_BC_DIM + g * SSD_STATE:SSD_D_INNER + SSD_BC_DIM + (g + 1) * SSD_STATE]
        cg32 = cg.astype(F32)
        bg_t = bg.astype(F32).T
        cb = lax.dot_general(cg, bg, (((1,), (1,)), ((), ())), preferred_element_type=F32)
        for p in range(hpg // 2):
            col0 = g * GROUP_W + p * LANES
            xp = xbc_ref[:, col0:col0 + LANES]
            zero = jnp.zeros_like(xp)
            x_lo = jnp.where(low, xp, zero)
            x_hi = jnp.where(low, zero, xp)
            s_in = st_ref[g, :, p * LANES:(p + 1) * LANES]
            s_bf = s_in.astype(BF16)
            s_lo = jnp.where(low, s_bf, zero)
            s_hi = jnp.where(low, zero, s_bf)
            c0 = direction * SSD_HEADS + g * hpg + 2 * p
            diag, offd, bw = [], [], []
            for c in (c0, c0 + 1):
                col = jnp.broadcast_to(cs[:, c:c + 1], (q, q))
                seg = col - cs_t[c:c + 1, :]
                diag.append((cb * jnp.exp2(jnp.where(mask, seg, -jnp.inf)) * dt_t[c:c + 1, :]).astype(BF16))
                offd.append((cg32 * jnp.exp2(col)).astype(BF16))
                bw.append((bg_t * wend_t[c:c + 1, :]).astype(BF16))
            lhs = jnp.concatenate(diag + offd, axis=1)
            rhs = jnp.concatenate([x_lo, x_hi, s_lo, s_hi], axis=0)
            y_ref[:, col0:col0 + LANES] = jnp.dot(lhs, rhs, preferred_element_type=F32).astype(y_ref.dtype)
            new = jnp.dot(jnp.concatenate(bw, axis=1), jnp.concatenate([x_lo, x_hi], axis=0),
                          preferred_element_type=F32)
            keep = jnp.where(low_row, cdec[:, c0:c0 + 1], cdec[:, c0 + 1:c0 + 2])
            st_ref[g, :, p * LANES:(p + 1) * LANES] = s_in * keep + new

    @pl.when(t == nt - 1)
    def _():
        for g in range(SSD_GROUPS):
            fin_ref[0, g * hpg:(g + 1) * hpg] = st_ref[g].T.reshape(hpg, SSD_HEAD_DIM, SSD_STATE)


def _ssd_scan_call(xbc_act, dt_raw, dt_bias, a_log, h0, y_into, fin_into, *, nb, seq, row_off, direction,
                   name):
    q = SSD_CHUNK
    nt = seq // q
    boff = row_off // q
    li = jnp.arange(q)[:, None]
    si = jnp.arange(q)[None, :]
    tri = ((li >= si) if direction == 0 else (li <= si)).astype(BF16)

    def chunk(b, t):
        return boff + b * nt + (t if direction == 0 else nt - 1 - t)

    zero_init = h0 is None
    in_specs = [pl.BlockSpec((q, SSD_CONV_CH), lambda b, t: (chunk(b, t), 0)),
                pl.BlockSpec((q, LANES), lambda b, t: (chunk(b, t), 0)),
                pl.BlockSpec((1, LANES), lambda b, t: (0, 0)),
                pl.BlockSpec((1, LANES), lambda b, t: (0, 0)),
                pl.BlockSpec((q, q), lambda b, t: (0, 0))]
    args = [xbc_act, dt_raw, dt_bias.reshape(1, LANES), a_log.reshape(1, LANES), tri]
    if not zero_init:
        in_specs.append(pl.BlockSpec((1, 1, SSD_HEADS, SSD_HEAD_DIM, SSD_STATE),
                                     lambda b, t: (b, direction, 0, 0, 0)))
        args.append(h0)
    aliases = {}
    if y_into is not None:
        in_specs.append(pl.BlockSpec(memory_space=pl.ANY))
        aliases = {len(args): 0}
        args.append(y_into)
    if fin_into is not None:
        in_specs.append(pl.BlockSpec(memory_space=pl.ANY))
        aliases[len(args)] = 1
        args.append(fin_into)
    return pl.pallas_call(
        functools.partial(_ssd_kernel, direction=direction, zero_init=zero_init, nt=nt),
        grid=(nb, nt),
        in_specs=in_specs,
        out_specs=[pl.BlockSpec((q, SSD_D_INNER), lambda b, t: (chunk(b, t), 0)),
                   pl.BlockSpec((1, None, SSD_HEADS, SSD_HEAD_DIM, SSD_STATE),
                                lambda b, t: (b, direction, 0, 0, 0))],
        out_shape=[jax.ShapeDtypeStruct((xbc_act.shape[0], SSD_D_INNER), BF16),
                   jax.ShapeDtypeStruct((nb, 2, SSD_HEADS, SSD_HEAD_DIM, SSD_STATE), F32)],
        scratch_shapes=[pltpu.VMEM((SSD_GROUPS, SSD_STATE, GROUP_W), F32)],
        input_output_aliases=aliases,
        compiler_params=_params("parallel", "arbitrary"),
        name=name,
    )(*args)


def _ssd_post_kernel(yf_ref, yb_ref, xs_ref, z_ref, d_ref, nw_ref, o_ref):
    z = z_ref[...].astype(F32)
    y = yf_ref[...].astype(F32) + yb_ref[...].astype(F32) + d_ref[...] * xs_ref[...].astype(F32)
    y = y * (z / (1.0 + jnp.exp(-z)))
    outs = []
    for g in range(SSD_GROUPS):
        yg = y[:, g * GROUP_W:(g + 1) * GROUP_W]
        ms = jnp.mean(yg * yg, axis=-1, keepdims=True)
        outs.append(yg * lax.rsqrt(ms + NORM_EPS) * nw_ref[:, g * GROUP_W:(g + 1) * GROUP_W])
    o_ref[...] = jnp.concatenate(outs, axis=1).astype(o_ref.dtype)


def _ssd_post_call(y_f, y_b, xbc_act, z, d_tot, norm_w, *, tm, name):
    m = y_f.shape[0]
    di = SSD_D_INNER
    row = pl.BlockSpec((tm, di), lambda i: (i, 0))
    vec = pl.BlockSpec((1, di), lambda i: (0, 0))
    return pl.pallas_call(
        _ssd_post_kernel,
        grid=(m // tm,),
        in_specs=[row, row, row, row, vec, vec],
        out_specs=row,
        out_shape=jax.ShapeDtypeStruct((m, di), BF16),
        compiler_params=_params("parallel"),
        name=name,
    )(y_f, y_b, xbc_act, z, d_tot, norm_w.reshape(1, di))


def _run_schedule(block_e, n_used):
    nblk = block_e.shape[0]
    idx = jnp.arange(nblk, dtype=jnp.int32)
    prev = jnp.concatenate([block_e[:1] - 1, block_e[:-1]])
    first = ((idx < n_used[0]) & (block_e != prev)).astype(jnp.int32)
    ridx = jnp.cumsum(first) - 1
    is_next = (first[None, :] == 1) & (ridx[None, :] == ridx[:, None] + 1)
    next_e = jnp.where(jnp.any(is_next, axis=1), jnp.sum(jnp.where(is_next, block_e[None, :], 0), axis=1), -1)
    counts = jnp.stack([n_used[0], jnp.sum(first)])
    return (block_e, counts.astype(jnp.int32), first, ridx.astype(jnp.int32), next_e.astype(jnp.int32))


def _weight_copies(w_hbm, stage, sems, layer, e, jj, slot, col_tiles, tn):
    return [pltpu.make_async_copy(w_hbm.at[layer, e, :, pl.ds(pl.multiple_of((off + jj) * tn, tn), tn)],
                                  stage.at[slot, wi], sems.at[slot, wi])
            for wi, off in enumerate(col_tiles)]


def _stage_run_weights(sched, w_hbm, stage, sems, w_bf, *, layer, col_tiles, tn, nj):
    be_ref, cnt_ref, first_ref, ridx_ref, next_ref = sched
    j = pl.program_id(0)
    i = pl.program_id(1)
    copies = functools.partial(_weight_copies, w_hbm, stage, sems, layer, col_tiles=col_tiles, tn=tn)

    @pl.when(first_ref[i] == 1)
    def _():
        slot = (j * cnt_ref[1] + ridx_ref[i]) & 1
        e = be_ref[i]

        @pl.when((i == 0) & (j == 0))
        def _():
            for c in copies(e, j, slot):
                c.start()

        for c in copies(e, j, slot):
            c.wait()
        def cast_rows(r, carry):
            rows = pl.ds(pl.multiple_of(r * CAST_ROWS, CAST_ROWS), CAST_ROWS)
            for wi in range(len(col_tiles)):
                w_bf[wi, rows, :] = stage[slot, wi, rows, :].astype(BF16)
            return carry

        lax.fori_loop(0, w_bf.shape[1] // CAST_ROWS, cast_rows, 0)
        nxt = next_ref[i]

        @pl.when(nxt >= 0)
        def _():
            for c in copies(nxt, j, 1 - slot):
                c.start()

        @pl.when((nxt < 0) & (j + 1 < nj))
        def _():
            for c in copies(be_ref[0], j + 1, 1 - slot):
                c.start()


def _moe_up_kernel(*refs, layer, tn, nj):
    sched, (x_ref, w_hbm, bg_ref, bu_ref, o_ref, stage, w_bf, sems) = refs[:5], refs[5:]
    i = pl.program_id(1)
    _stage_run_weights(sched, w_hbm, stage, sems, w_bf, layer=layer, col_tiles=(0, nj), tn=tn, nj=nj)
    nu_ref = sched[1]

    @pl.when(i < nu_ref[0])
    def _():
        x = x_ref[...]
        g = jnp.dot(x, w_bf[0], preferred_element_type=F32) + bg_ref[...]
        u = jnp.dot(x, w_bf[1], preferred_element_type=F32) + bu_ref[...]
        g = jnp.minimum(g, SWIGLU_LIMIT)
        u = jnp.clip(u, -SWIGLU_LIMIT, SWIGLU_LIMIT)
        act = (u + 1.0) * (g / (1.0 + jnp.exp2((-SWIGLU_ALPHA * LOG2_E) * g)))
        o_ref[...] = act.astype(o_ref.dtype)

    @pl.when(i >= nu_ref[0])
    def _():
        o_ref[...] = jnp.zeros_like(o_ref)


def _moe_down_kernel(*refs, layer, tn, nj):
    sched, (a_ref, w_hbm, b_ref), (o_ref, stage, w_bf, sems) = refs[:5], refs[5:8], refs[-4:]
    i = pl.program_id(1)
    _stage_run_weights(sched, w_hbm, stage, sems, w_bf, layer=layer, col_tiles=(0,), tn=tn, nj=nj)
    nu_ref = sched[1]

    @pl.when(i < nu_ref[0])
    def _():
        y = jnp.dot(a_ref[...], w_bf[0], preferred_element_type=F32) + b_ref[...]
        o_ref[...] = y.astype(o_ref.dtype)

    @pl.when(i >= nu_ref[0])
    def _():
        o_ref[...] = jnp.zeros_like(o_ref)


def _moe_experts(x_sorted, block_e, n_used, y_into, layer, w_gate_up, b_gate_up, w_down, b_down, *, blk_off,
                 total_slots, tn_up, tn_down, name):
    slots, d = x_sorted.shape
    nblk = slots // MOE_BLOCK
    nj = D_FF // tn_up
    nl = w_gate_up.shape[0]
    b_gu = b_gate_up.reshape(nl, N_EXPERTS, 1, 2 * D_FF)
    sched = _run_schedule(block_e, n_used)
    act = pl.pallas_call(
        functools.partial(_moe_up_kernel, layer=layer, tn=tn_up, nj=nj),
        grid_spec=pltpu.PrefetchScalarGridSpec(
            num_scalar_prefetch=len(sched),
            grid=(nj, nblk),
            in_specs=[pl.BlockSpec((MOE_BLOCK, d), lambda j, i, *s: (i, 0)),
                      pl.BlockSpec(memory_space=pl.ANY),
                      pl.BlockSpec((None, None, 1, tn_up), lambda j, i, be, *s: (layer, be[i], 0, j)),
                      pl.BlockSpec((None, None, 1, tn_up), lambda j, i, be, *s: (layer, be[i], 0, nj + j))],
            out_specs=pl.BlockSpec((MOE_BLOCK, tn_up), lambda j, i, *s: (i, j)),
            scratch_shapes=[pltpu.VMEM((2, 2, d, tn_up), F32), pltpu.VMEM((2, d, tn_up), BF16),
                            pltpu.SemaphoreType.DMA((2, 2))]),
        out_shape=jax.ShapeDtypeStruct((slots, D_FF), BF16),
        compiler_params=_params("arbitrary", "arbitrary"),
        name=name + "_up",
    )(*sched, x_sorted, w_gate_up, b_gu, b_gu)
    nj2 = d // tn_down
    in_specs = [pl.BlockSpec((MOE_BLOCK, D_FF), lambda j, i, *s: (i, 0)),
                pl.BlockSpec(memory_space=pl.ANY),
                pl.BlockSpec((None, None, 1, tn_down), lambda j, i, be, *s: (layer, be[i], 0, j))]
    args = [*sched, act, w_down, b_down.reshape(nl, N_EXPERTS, 1, d)]
    aliases = {}
    if y_into is not None:
        in_specs.append(pl.BlockSpec(memory_space=pl.ANY))
        aliases = {len(args): 0}
        args.append(y_into)
    return pl.pallas_call(
        functools.partial(_moe_down_kernel, layer=layer, tn=tn_down, nj=nj2),
        grid_spec=pltpu.PrefetchScalarGridSpec(
            num_scalar_prefetch=len(sched),
            grid=(nj2, nblk),
            in_specs=in_specs,
            out_specs=pl.BlockSpec((MOE_BLOCK, tn_down), lambda j, i, *s: (i + blk_off, j)),
            scratch_shapes=[pltpu.VMEM((2, 1, D_FF, tn_down), F32), pltpu.VMEM((1, D_FF, tn_down), BF16),
                            pltpu.SemaphoreType.DMA((2, 1))]),
        out_shape=jax.ShapeDtypeStruct((total_slots, d), BF16),
        input_output_aliases=aliases,
        compiler_params=_params("arbitrary", "arbitrary"),
        name=name + "_down",
    )(*args)


def _combine_kernel(*refs):
    y_refs = refs[:TOP_K]
    gate_ref, x_ref, g2_ref = refs[TOP_K:TOP_K + 3]
    o_ref = refs[-1]
    gates = gate_ref[...]
    f = None
    for k in range(TOP_K):
        term = gates[:, k:k + 1] * y_refs[k][...].astype(F32)
        f = term if f is None else f + term
    o_ref[...] = x_ref[...] + g2_ref[0] * f


def _combine_rows_call(y4, gate_pad, x, mod, out_into, *, row_off, m, out_rows, out_off, n_p, ls, tm, name):
    d = x.shape[1]
    nt = m // tm
    xoff = row_off // tm
    ooff = out_off // tm
    y_specs = [pl.BlockSpec((tm, d), functools.partial(lambda i, j, k: (k * nt + i, 0), k=k))
               for k in range(TOP_K)]
    in_specs = y_specs + [
        pl.BlockSpec((tm, LANES), lambda i, j: (i + xoff, 0)),
        pl.BlockSpec((tm, d), lambda i, j: (i + xoff, 0)),
        pl.BlockSpec((1, 1, d), lambda i, j: (_group_of_tile(i + xoff, tm, n_p, ls) * 6 + 5, 0, 0))]
    args = [y4] * TOP_K + [gate_pad, x, mod]
    aliases = {}
    if out_into is not None:
        in_specs.append(pl.BlockSpec(memory_space=pl.ANY))
        aliases = {len(args): 0}
        args.append(out_into)
    return pl.pallas_call(
        _combine_kernel,
        grid=(nt, 1),
        in_specs=in_specs,
        out_specs=pl.BlockSpec((tm, d), lambda i, j: (i + ooff, 0)),
        out_shape=jax.ShapeDtypeStruct((out_rows, d), F32),
        input_output_aliases=aliases,
        compiler_params=_params("parallel", "arbitrary"),
        name=name,
    )(*args)


def _moe_layer(x, mod, n_p, ls, layer, norm_w, w_router, b_router, w_gate_up, b_gate_up, w_down, b_down, *,
               split, name):
    n, d = x.shape
    h, idx_pad, gate_pad, rank_pad, cnt = _router_call(x, norm_w.reshape(1, d), mod, n_p, ls, w_router,
                                                       b_router, tm=256, name=name + "_router")
    n_assign = n * TOP_K
    flat_e = idx_pad[:, :TOP_K].reshape(-1)
    counts = cnt[0, :N_EXPERTS]
    padded = (counts + MOE_BLOCK - 1) // MOE_BLOCK * MOE_BLOCK
    pad_end = jnp.cumsum(padded)
    pad_start = pad_end - padded
    dest = pad_start[flat_e] + rank_pad[:, :TOP_K].reshape(-1)
    n_blocks = -(-n_assign // MOE_BLOCK) + N_EXPERTS
    slots = n_blocks * MOE_BLOCK
    slot_tok = (jnp.arange(slots, dtype=jnp.int32) % n).at[dest].set(
        jnp.arange(n_assign, dtype=jnp.int32) // TOP_K, unique_indices=True, mode="promise_in_bounds")
    block_start = jnp.arange(n_blocks, dtype=jnp.int32) * MOE_BLOCK
    block_e = jnp.minimum(jnp.sum((pad_end[None, :] <= block_start[:, None]).astype(jnp.int32), axis=1),
                          N_EXPERTS - 1)
    n_used = (pad_end[-1] // MOE_BLOCK).astype(jnp.int32).reshape(1)
    cb = n_blocks // MOE_RANGES
    y_sorted = None
    for c in range(MOE_RANGES):
        rows = slice(c * cb * MOE_BLOCK, (c + 1) * cb * MOE_BLOCK)
        x_sorted = h.at[slot_tok[rows]].get(mode="promise_in_bounds")
        used = jnp.clip(n_used - c * cb, 0, cb)
        y_sorted = _moe_experts(x_sorted, block_e[c * cb:(c + 1) * cb], used, y_sorted, layer, w_gate_up,
                                b_gate_up, w_down, b_down, blk_off=c * cb, total_slots=slots, tn_up=1024,
                                tn_down=2048, name="%s_r%d" % (name, c))
    n_s = n - n_p
    ranges = [(0, n_p), (n_p, n_s // 2), (n_p + n_s // 2, n_s - n_s // 2)]
    dest_tk = dest.reshape(n, TOP_K)
    outs = [None, None]
    for ri, (t0, m) in enumerate(ranges):
        idx = dest_tk[t0:t0 + m].T.reshape(-1)
        y4 = y_sorted.at[idx].get(mode="promise_in_bounds")
        if split:
            which, out_rows, out_off = (0, n_p, 0) if ri == 0 else (1, n_s, t0 - n_p)
        else:
            which, out_rows, out_off = 0, n, t0
        outs[which] = _combine_rows_call(y4, gate_pad, x, mod, outs[which], row_off=t0, m=m,
                                         out_rows=out_rows, out_off=out_off, n_p=n_p, ls=ls, tm=256,
                                         name="%s_combine%d" % (name, ri))
    return (outs[0], outs[1]) if split else outs[0]


def _ada_call(cond, w_ada, b_ada, layer, *, name):
    g = cond.shape[0]
    a = jnp.pad(jax.nn.silu(cond), ((0, 16 - g), (0, 0))).astype(BF16)
    m = _matmul(a, w_ada, n_out=6 * D_MODEL, tm=16, tn=1024, out_dtype=F32, name=name, layer=layer,
                bias=b_ada.reshape(b_ada.shape[0], 1, 6 * D_MODEL))
    return m[:g].reshape(g * 6, 1, D_MODEL)


def kernel(x_prompt, x_sample, c, c_ctx, cache_k, cache_v, state_ssm, norm1_w, norm2_w, w_ada, b_ada, w_qkv, q_norm_w, k_norm_w, lambda_q1, lambda_k1, lambda_q2, lambda_k2, subln_w, w_o, w_in_ssd, conv_w, conv_b, dt_bias, a_log, d_skip, ssd_norm_w, w_out_ssd, w_router, b_router, w_gate_up, b_gate_up, w_down, b_down):
    bp, lp, d = x_prompt.shape
    bs, ls, _ = x_sample.shape
    past = cache_k.shape[2]
    n_p, n_s = bp * lp, bs * ls
    xp0, xs0 = x_prompt.reshape(n_p, d), x_sample.reshape(n_s, d)
    cond = jnp.concatenate([c_ctx[None], c], axis=0)
    resid_of = lambda xx, mod, which: (xx, mod, which, n_p, ls)

    n = n_p + n_s
    mod = _ada_call(cond, w_ada, b_ada, 0, name="ada0")
    wq = w_qkv[0].astype(BF16)
    qk_w = jnp.stack([q_norm_w[0], k_norm_w[0]], axis=0)
    lam_init = 0.8 - 0.6 * math.exp(-0.3 * 0)
    lam_params = jnp.stack([lambda_q1[0], lambda_k1[0], lambda_q2[0], lambda_k2[0]], axis=0)
    qp, kp, vp = _qkv_call(xp0, norm1_w[0], mod, wq, qk_w, None, row_off=0, n_p=n_p, kv_dtype=F32, tm=512,
                           tn=512, ls=ls, name="qkv_prompt")
    qs, ks, vs = _qkv_call(xs0, norm1_w[0], mod, wq, qk_w, _rope_tables(ls), row_off=n_p, n_p=n_p,
                           kv_dtype=BF16, tm=512, tn=512, ls=ls, name="qkv_sample")
    o = _attn_call(qp, [(kp, vp, lp)], lam_params, subln_w[0], None, n_rows=n, row_off=0, nb=bp, lq=lp,
                   tq=lp, lam_init=lam_init, name="attn_prompt")
    ck = cache_k[:, 0].reshape(bs * past, d)
    cv = cache_v[:, 0].reshape(bs * past, d)
    o = _attn_call(qs, [(ck, cv, past), (ks, vs, ls)], lam_params, subln_w[0], o, n_rows=n, row_off=n_p,
                   nb=bs, lq=ls, tq=256, lam_init=lam_init, name="attn_sample")
    wo = w_o[0].astype(BF16)
    x = _resid_rows_matmul(o, wo, xp0, mod, 2, None, row_off=0, n_p=n_p, ls=ls, tm=1024, tn=512,
                           name="attn_out_prompt")
    x = _resid_rows_matmul(o, wo, xs0, mod, 2, x, row_off=n_p, n_p=n_p, ls=ls, tm=1024, tn=512,
                           name="attn_out_sample")
    x = _moe_layer(x, mod, n_p, ls, 0, norm2_w[0], w_router[0], b_router[0], w_gate_up, b_gate_up,
                   w_down, b_down, split=False, name="moe0")
    new_k = kp.reshape(bp, 1, lp, ATTN_HEADS, 2, QK_DIM)
    new_v = vp.reshape(bp, 1, lp, ATTN_HEADS, V_DIM)

    mod = _ada_call(cond, w_ada, b_ada, 1, name="ada1")
    h = _modnorm_call(x, norm1_w[1].reshape(1, d), mod, 0, n_p, ls, tm=256, name="norm1_1")
    w_in = w_in_ssd[0].astype(BF16)
    z = _matmul(h, w_in, n_out=SSD_D_INNER, tm=1024, tn=512, out_dtype=BF16, name="ssd_in_z")
    xbc = _matmul(h, w_in, n_out=SSD_CONV_CH, col_off=SSD_D_INNER, tm=1024, tn=512, out_dtype=BF16,
                  name="ssd_in_xbc")
    dt_raw = _matmul(h, w_in, n_out=2 * SSD_HEADS, col_off=SSD_D_INNER + SSD_CONV_CH, tm=1024, tn=LANES,
                     out_dtype=F32, name="ssd_in_dt")
    xbc_act = _conv_call(xbc, conv_w[0], conv_b[0], n_p=n_p, lp=lp, ls=ls, tc=256, tcn=2048, name="ssd_conv")
    ys, fin = [], None
    for direction in (0, 1):
        y, fin = _ssd_scan_call(xbc_act, dt_raw, dt_bias[0], a_log[0], None, None, fin, nb=bp, seq=lp,
                                row_off=0, direction=direction, name="ssd_scan_prompt%d" % direction)
        y, _ = _ssd_scan_call(xbc_act, dt_raw, dt_bias[0], a_log[0], state_ssm[:, 0], y, None, nb=bs, seq=ls,
                              row_off=n_p, direction=direction, name="ssd_scan_sample%d" % direction)
        ys.append(y)
    d_tot = jnp.repeat(d_skip[0, 0] + d_skip[0, 1], SSD_HEAD_DIM).reshape(1, SSD_D_INNER)
    yn = _ssd_post_call(ys[0], ys[1], xbc_act, z, d_tot, ssd_norm_w[0], tm=256, name="ssd_post")
    x = _matmul(yn, w_out_ssd[0].astype(BF16), n_out=d, tm=1024, tn=512, out_dtype=F32, name="ssd_out",
                resid=resid_of(x, mod, 2))
    xp, xs = _moe_layer(x, mod, n_p, ls, 1, norm2_w[1], w_router[1], b_router[1], w_gate_up, b_gate_up,
                        w_down, b_down, split=True, name="moe1")
    new_s = fin[:, None]
    return (xp.reshape(bp, lp, d), xs.reshape(bs, ls, d), new_k, new_v, new_s)
```

```python
import functools
import math

import jax
import jax.numpy as jnp
from jax import lax
from jax.experimental import pallas as pl
from jax.experimental.pallas import tpu as pltpu

F32 = jnp.float32
BF16 = jnp.bfloat16

D_MODEL = 2048
NORM_EPS = 1e-6
GRID_W = 64
ROPE_THETA = 10000.0
ATTN_HEADS = 8
QK_DIM = 128
V_DIM = 256
SSD_D_INNER = 4096
SSD_HEAD_DIM = 64
SSD_HEADS = 64
SSD_GROUPS = 8
SSD_STATE = 128
SSD_CONV = 5
SSD_CHUNK = 128
SSD_BC_DIM = SSD_GROUPS * SSD_STATE
SSD_CONV_CH = SSD_D_INNER + 2 * SSD_BC_DIM
N_EXPERTS = 32
TOP_K = 4
D_FF = 2048
SWIGLU_LIMIT = 7.0
SWIGLU_ALPHA = 1.702
MOE_BLOCK = 512
MOE_RANGES = 4
ATTN_KEY_CHUNK = 512
CAST_ROWS = 256
LOG2_E = math.log2(math.e)
LANES = 128
GROUP_W = SSD_D_INNER // SSD_GROUPS
assert SSD_CHUNK == SSD_STATE == LANES and 2 * SSD_HEAD_DIM == LANES
VMEM_LIMIT = 56 * 1024 * 1024


def _params(*sem):
    return pltpu.CompilerParams(dimension_semantics=sem, vmem_limit_bytes=VMEM_LIMIT)


def _group_of_tile(i, tm, n_p, ls):
    return jnp.where(i * tm < n_p, 0, 1 + (i * tm - n_p) // ls)


def _mod_spec(which, tm, tn, n_p, ls):
    return pl.BlockSpec((1, 1, tn), lambda i, j: (_group_of_tile(i, tm, n_p, ls) * 6 + which, 0, j))


def _mm_kernel(a_ref, w_ref, *rest, epilogue):
    acc = jnp.dot(a_ref[...].astype(BF16), w_ref[...].astype(BF16), preferred_element_type=F32)
    if epilogue == "bias":
        b_ref, o_ref = rest
        o_ref[...] = (acc + b_ref[...]).astype(o_ref.dtype)
    elif epilogue == "resid":
        x_ref, g_ref, o_ref = rest[0], rest[1], rest[-1]
        o_ref[...] = x_ref[...] + g_ref[0] * acc
    else:
        (o_ref,) = rest
        o_ref[...] = acc.astype(o_ref.dtype)


def _matmul(a, w, *, n_out, col_off=0, tm, tn, out_dtype, name, layer=None, bias=None, resid=None):
    m, k = a.shape
    joff = col_off // tn
    if layer is None:
        w_spec = pl.BlockSpec((k, tn), lambda i, j: (0, j + joff))
        b_spec = pl.BlockSpec((1, tn), lambda i, j: (0, j))
    else:
        w_spec = pl.BlockSpec((None, k, tn), lambda i, j: (layer, 0, j + joff))
        b_spec = pl.BlockSpec((None, 1, tn), lambda i, j: (layer, 0, j))
    in_specs = [pl.BlockSpec((tm, k), lambda i, j: (i, 0)), w_spec]
    args = [a, w]
    if bias is not None:
        epilogue = "bias"
        in_specs.append(b_spec)
        args.append(bias)
    elif resid is not None:
        epilogue = "resid"
        x, mod, which, n_p, ls = resid
        in_specs += [pl.BlockSpec((tm, tn), lambda i, j: (i, j)), _mod_spec(which, tm, tn, n_p, ls)]
        args += [x, mod]
    else:
        epilogue = "plain"
    return pl.pallas_call(
        functools.partial(_mm_kernel, epilogue=epilogue),
        grid=(m // tm, n_out // tn),
        in_specs=in_specs,
        out_specs=pl.BlockSpec((tm, tn), lambda i, j: (i, j)),
        out_shape=jax.ShapeDtypeStruct((m, n_out), out_dtype),
        compiler_params=_params("parallel", "arbitrary"),
        name=name,
    )(*args)


def _resid_rows_matmul(a_table, w, x_rows, mod, which, out_into, *, row_off, n_p, ls, tm, tn, name):
    n_rows, k = a_table.shape
    m, n_out = x_rows.shape
    ioff = row_off // tm
    in_specs = [pl.BlockSpec((tm, k), lambda i, j: (i + ioff, 0)),
                pl.BlockSpec((k, tn), lambda i, j: (0, j)),
                pl.BlockSpec((tm, tn), lambda i, j: (i, j)),
                pl.BlockSpec((1, 1, tn), lambda i, j: (_group_of_tile(i + ioff, tm, n_p, ls) * 6 + which, 0, j))]
    args = [a_table, w, x_rows, mod]
    aliases = {}
    if out_into is not None:
        in_specs.append(pl.BlockSpec(memory_space=pl.ANY))
        aliases = {len(args): 0}
        args.append(out_into)
    return pl.pallas_call(
        functools.partial(_mm_kernel, epilogue="resid"),
        grid=(m // tm, n_out // tn),
        in_specs=in_specs,
        out_specs=pl.BlockSpec((tm, tn), lambda i, j: (i + ioff, j)),
        out_shape=jax.ShapeDtypeStruct((n_rows, n_out), F32),
        input_output_aliases=aliases,
        compiler_params=_params("parallel", "arbitrary"),
        name=name,
    )(*args)


def _modnorm(x, w, shift, scale):
    ms = jnp.mean(x * x, axis=-1, keepdims=True)
    return (x * lax.rsqrt(ms + NORM_EPS) * w) * (1.0 + scale) + shift


def _split2(v):
    hi = v.astype(BF16)
    lo = (v - hi.astype(F32)).astype(BF16)
    return hi, lo


def _router_kernel(x_ref, w_ref, sh_ref, sc_ref, wr_ref, br_ref, h_ref, idx_ref, gate_ref, rank_ref, cnt_ref,
                   run_ref):
    @pl.when(pl.program_id(0) == 0)
    def _():
        run_ref[...] = jnp.zeros_like(run_ref)

    h = _modnorm(x_ref[...], w_ref[...], sh_ref[0], sc_ref[0])
    h_ref[...] = h.astype(BF16)
    h_hi, h_lo = _split2(h)
    w_hi, w_lo = _split2(wr_ref[...])
    logits = (jnp.dot(h_hi, w_hi, preferred_element_type=F32)
              + jnp.dot(h_hi, w_lo, preferred_element_type=F32)
              + jnp.dot(h_lo, w_hi, preferred_element_type=F32)) + br_ref[...]
    lane = lax.broadcasted_iota(jnp.int32, logits.shape, 1).astype(F32)
    cur = jnp.where(lane < N_EXPERTS, logits, -jnp.inf)
    vals, idxs = [], []
    for _ in range(TOP_K):
        m = jnp.max(cur, axis=-1, keepdims=True)
        am = jnp.min(jnp.where(cur == m, lane, float(LANES)), axis=-1, keepdims=True)
        vals.append(m)
        idxs.append(am)
        cur = jnp.where(lane == am, -jnp.inf, cur)
    exps = [jnp.exp(v - vals[0]) for v in vals]
    denom = exps[0] + exps[1] + exps[2] + exps[3]
    idx_out = jnp.zeros(logits.shape, F32)
    gate_out = jnp.zeros(logits.shape, F32)
    for k in range(TOP_K):
        idx_out = jnp.where(lane == k, idxs[k], idx_out)
        gate_out = jnp.where(lane == k, exps[k] / denom, gate_out)
    idx_ref[...] = idx_out.astype(jnp.int32)
    gate_ref[...] = gate_out
    tm = logits.shape[0]
    chosen = functools.reduce(jnp.logical_or, [lane == am for am in idxs])
    onehot = jnp.where(chosen, 1.0, 0.0)
    ti = lax.broadcasted_iota(jnp.int32, (tm, tm), 0)
    tj = lax.broadcasted_iota(jnp.int32, (tm, tm), 1)
    earlier = jnp.where(ti > tj, 1.0, 0.0).astype(BF16)
    base = run_ref[...] + jnp.dot(earlier, onehot.astype(BF16), preferred_element_type=F32)
    rank_out = jnp.zeros(logits.shape, F32)
    for k in range(TOP_K):
        rk = jnp.sum(jnp.where(lane == idxs[k], base, 0.0), axis=-1, keepdims=True)
        rank_out = jnp.where(lane == k, rk, rank_out)
    rank_ref[...] = rank_out.astype(jnp.int32)
    run_ref[...] = run_ref[...] + jnp.sum(onehot, axis=0, keepdims=True)
    cnt_ref[...] = run_ref[...].astype(jnp.int32)


def _router_call(x, w, mod, n_p, ls, w_router, b_router, *, tm, name):
    m, d = x.shape
    wr = jnp.pad(w_router, ((0, 0), (0, LANES - N_EXPERTS)))
    br = jnp.pad(b_router, (0, LANES - N_EXPERTS)).reshape(1, LANES)
    return pl.pallas_call(
        _router_kernel,
        grid=(m // tm, 1),
        in_specs=[pl.BlockSpec((tm, d), lambda i, j: (i, 0)),
                  pl.BlockSpec((1, d), lambda i, j: (0, 0)),
                  _mod_spec(3, tm, d, n_p, ls),
                  _mod_spec(4, tm, d, n_p, ls),
                  pl.BlockSpec((d, LANES), lambda i, j: (0, 0)),
                  pl.BlockSpec((1, LANES), lambda i, j: (0, 0))],
        out_specs=[pl.BlockSpec((tm, d), lambda i, j: (i, 0)),
                   pl.BlockSpec((tm, LANES), lambda i, j: (i, 0)),
                   pl.BlockSpec((tm, LANES), lambda i, j: (i, 0)),
                   pl.BlockSpec((tm, LANES), lambda i, j: (i, 0)),
                   pl.BlockSpec((1, LANES), lambda i, j: (0, 0))],
        out_shape=[jax.ShapeDtypeStruct((m, d), BF16),
                   jax.ShapeDtypeStruct((m, LANES), jnp.int32),
                   jax.ShapeDtypeStruct((m, LANES), F32),
                   jax.ShapeDtypeStruct((m, LANES), jnp.int32),
                   jax.ShapeDtypeStruct((1, LANES), jnp.int32)],
        scratch_shapes=[pltpu.VMEM((1, LANES), F32)],
        compiler_params=_params("arbitrary", "arbitrary"),
        name=name,
    )(x, w, mod, mod, wr, br)


def _qkv_kernel(x_ref, n1_ref, sh_ref, sc_ref, w_ref, nw_ref, *rest, rope, tn):
    if rope:
        cos_ref, sin_ref, q_ref, k_ref, v_ref, h_s = rest
    else:
        q_ref, k_ref, v_ref, h_s = rest
    j = pl.program_id(1)
    nq = D_MODEL // tn

    @pl.when(j == 0)
    def _():
        h_s[...] = _modnorm(x_ref[...], n1_ref[...], sh_ref[0], sc_ref[0]).astype(BF16)

    acc = jnp.dot(h_s[...], w_ref[...], preferred_element_type=F32)

    def normed(widx, out_scale):
        w = nw_ref[widx:widx + 1, :]
        outs = []
        for c in range(tn // QK_DIM):
            xc = acc[:, c * QK_DIM:(c + 1) * QK_DIM]
            ms = jnp.mean(xc * xc, axis=-1, keepdims=True)
            y = xc * lax.rsqrt(ms + NORM_EPS) * w
            if rope:
                lane = lax.broadcasted_iota(jnp.int32, y.shape, 1)
                partner = jnp.where(lane % 64 < 32, pltpu.roll(y, QK_DIM - 32, 1), pltpu.roll(y, 32, 1))
                y = y * cos_ref[...] + partner * sin_ref[...]
            outs.append(y * out_scale if out_scale != 1.0 else y)
        return jnp.concatenate(outs, axis=1)

    @pl.when(j < nq)
    def _():
        q_ref[...] = normed(0, QK_DIM ** -0.5 * LOG2_E).astype(q_ref.dtype)

    @pl.when((j >= nq) & (j < 2 * nq))
    def _():
        k_ref[...] = normed(1, 1.0).astype(k_ref.dtype)

    @pl.when(j >= 2 * nq)
    def _():
        v_ref[...] = acc.astype(v_ref.dtype)


def _qkv_call(x_rows, norm1_w, mod, w_qkv, qk_norm_w, rope_tabs, *, row_off, n_p, kv_dtype, tm, tn, ls, name):
    m, d = x_rows.shape
    nq = d // tn
    ioff = row_off // tm
    rope = rope_tabs is not None

    def mod_spec(which):
        return pl.BlockSpec((1, 1, d), lambda i, j: (_group_of_tile(i + ioff, tm, n_p, ls) * 6 + which, 0, 0))

    in_specs = [pl.BlockSpec((tm, d), lambda i, j: (i, 0)),
                pl.BlockSpec((1, d), lambda i, j: (0, 0)),
                mod_spec(0), mod_spec(1),
                pl.BlockSpec((d, tn), lambda i, j: (0, j)),
                pl.BlockSpec((2, QK_DIM), lambda i, j: (0, 0))]
    args = [x_rows, norm1_w.reshape(1, d), mod, mod, w_qkv, qk_norm_w]
    if rope:
        nt = ls // tm
        in_specs += [pl.BlockSpec((tm, QK_DIM), lambda i, j: (i % nt, 0))] * 2
        args += list(rope_tabs)
    return pl.pallas_call(
        functools.partial(_qkv_kernel, rope=rope, tn=tn),
        grid=(m // tm, 3 * nq),
        in_specs=in_specs,
        out_specs=[pl.BlockSpec((tm, tn), lambda i, j: (i, jnp.minimum(j, nq - 1))),
                   pl.BlockSpec((tm, tn), lambda i, j: (i, jnp.clip(j - nq, 0, nq - 1))),
                   pl.BlockSpec((tm, tn), lambda i, j: (i, jnp.clip(j - 2 * nq, 0, nq - 1)))],
        out_shape=[jax.ShapeDtypeStruct((m, d), BF16),
                   jax.ShapeDtypeStruct((m, d), kv_dtype),
                   jax.ShapeDtypeStruct((m, d), kv_dtype)],
        scratch_shapes=[pltpu.VMEM((tm, d), BF16)],
        compiler_params=_params("parallel", "arbitrary"),
        name=name,
    )(*args)


def _rope_tables(ls):
    rows = ls // GRID_W
    row = jnp.repeat(jnp.arange(rows, dtype=F32), GRID_W)
    col = jnp.tile(jnp.arange(GRID_W, dtype=F32), rows)
    axis_dim = QK_DIM // 2
    inv = ROPE_THETA ** (-jnp.arange(0, axis_dim, 2, dtype=F32) / axis_dim)
    ar, ac = row[:, None] * inv, col[:, None] * inv
    cos = jnp.concatenate([jnp.cos(ar), jnp.cos(ar), jnp.cos(ac), jnp.cos(ac)], axis=1)
    sin = jnp.concatenate([-jnp.sin(ar), jnp.sin(ar), -jnp.sin(ac), jnp.sin(ac)], axis=1)
    return cos, sin


def _attn_kernel(lam_ref, subw_ref, q_ref, *rest, nseg, lam_init):
    k_refs, v_refs, o_ref = rest[:nseg], rest[nseg:2 * nseg], rest[-1]
    lp = lam_ref[...]
    lam = (jnp.exp(jnp.sum(lp[0:1] * lp[1:2], axis=-1, keepdims=True))
           - jnp.exp(jnp.sum(lp[2:3] * lp[3:4], axis=-1, keepdims=True)) + lam_init)
    q = q_ref[...]
    chunks = [(k_ref, v_ref, c0) for k_ref, v_ref in zip(k_refs, v_refs)
              for c0 in range(0, k_ref.shape[0], min(ATTN_KEY_CHUNK, k_ref.shape[0]))]
    outs = []
    for mi in range(2):
        qm = q[:, mi * QK_DIM:(mi + 1) * QK_DIM]
        mx = den = pv = None
        for k_ref, v_ref, c0 in chunks:
            kc = min(ATTN_KEY_CHUNK, k_ref.shape[0])
            k = k_ref[c0:c0 + kc, mi * QK_DIM:(mi + 1) * QK_DIM].astype(BF16)
            v = v_ref[c0:c0 + kc, :].astype(BF16)
            s = lax.dot_general(qm, k, (((1,), (1,)), ((), ())), preferred_element_type=F32)
            cmax = jnp.max(s, axis=-1, keepdims=True)
            if mx is None:
                mx = cmax
                p = jnp.exp2(s - mx)
                den = jnp.sum(p, axis=-1, keepdims=True)
                pv = jnp.dot(p.astype(BF16), v, preferred_element_type=F32)
            else:
                new_mx = jnp.maximum(mx, cmax)
                alpha = jnp.exp2(mx - new_mx)
                p = jnp.exp2(s - new_mx)
                den = den * alpha + jnp.sum(p, axis=-1, keepdims=True)
                pv = pv * alpha + jnp.dot(p.astype(BF16), v, preferred_element_type=F32)
                mx = new_mx
        outs.append(pv * (1.0 / den))
    acc = outs[0] - lam * outs[1]
    ms = jnp.mean(acc * acc, axis=-1, keepdims=True)
    o = acc * lax.rsqrt(ms + NORM_EPS) * subw_ref[...] * (1.0 - lam_init)
    o_ref[...] = o.astype(o_ref.dtype)


def _attn_call(q, kv_segs, lam_params, subln_w, o_into, *, n_rows, row_off, nb, lq, tq, lam_init, name):
    nqb = lq // tq
    nseg = len(kv_segs)
    ooff = row_off // tq
    in_specs = [pl.BlockSpec((4, QK_DIM), lambda b, h, i: (0, 0)),
                pl.BlockSpec((1, V_DIM), lambda b, h, i: (0, 0)),
                pl.BlockSpec((tq, V_DIM), lambda b, h, i: (b * nqb + i, h))]
    in_specs += [pl.BlockSpec((lk, V_DIM), lambda b, h, i: (b, h)) for (_, _, lk) in kv_segs] * 2
    args = [lam_params, subln_w.reshape(1, V_DIM), q]
    args += [k for (k, _, _) in kv_segs] + [v for (_, v, _) in kv_segs]
    aliases = {}
    if o_into is not None:
        in_specs.append(pl.BlockSpec(memory_space=pl.ANY))
        aliases = {len(args): 0}
        args.append(o_into)
    return pl.pallas_call(
        functools.partial(_attn_kernel, nseg=nseg, lam_init=lam_init),
        grid=(nb, ATTN_HEADS, nqb),
        in_specs=in_specs,
        out_specs=pl.BlockSpec((tq, V_DIM), lambda b, h, i: (ooff + b * nqb + i, h)),
        out_shape=jax.ShapeDtypeStruct((n_rows, q.shape[1]), BF16),
        input_output_aliases=aliases,
        compiler_params=_params("parallel", "parallel", "arbitrary"),
        name=name,
    )(*args)


def _conv_kernel(xp_ref, x_ref, xn_ref, w_ref, b_ref, shift_ref, o_ref, *, tc, n_p, lp, ls, halo):
    i = pl.program_id(0)
    row0 = i * tc
    in_prompt = row0 < n_p
    first = jnp.where(in_prompt, row0 % lp == 0, (row0 - n_p) % ls == 0)
    last = jnp.where(in_prompt, (row0 + tc) % lp == 0, (row0 + tc - n_p) % ls == 0)
    x = x_ref[...]
    shifted = jnp.dot(shift_ref[...], x, preferred_element_type=F32)
    acc = b_ref[...] + w_ref[2:3, :] * x.astype(F32)
    for si, k in enumerate((0, 1, 3, 4)):
        acc = acc + w_ref[k:k + 1, :] * shifted[si * tc:(si + 1) * tc, :]
    prev = jnp.where(first, 0.0, xp_ref[...].astype(F32)[halo - 8:halo, :])
    nxt = jnp.where(last, 0.0, xn_ref[...].astype(F32)[0:8, :])
    r8 = lax.broadcasted_iota(jnp.int32, prev.shape, 0)
    top = (w_ref[0:1, :] * jnp.where(r8 < 2, pltpu.roll(prev, 2, 0), 0.0)
           + w_ref[1:2, :] * jnp.where(r8 < 1, pltpu.roll(prev, 1, 0), 0.0))
    bot = (w_ref[3:4, :] * jnp.where(r8 >= 7, pltpu.roll(nxt, 7, 0), 0.0)
           + w_ref[4:5, :] * jnp.where(r8 >= 6, pltpu.roll(nxt, 6, 0), 0.0))
    acc = acc + jnp.concatenate([top, jnp.zeros((tc - 16, acc.shape[1]), F32), bot], axis=0)
    o_ref[...] = (acc / (1.0 + jnp.exp2(-LOG2_E * acc))).astype(o_ref.dtype)


def _conv_call(xbc, conv_w, conv_b, *, n_p, lp, ls, tc, tcn, name):
    m, ch = xbc.shape
    halo = 16
    hb = tc // halo
    nhb = m // halo
    out_row = jnp.arange(tc)[None, :, None]
    in_row = jnp.arange(tc)[None, None, :]
    tap = jnp.array([0, 1, 3, 4])[:, None, None]
    shift = (in_row == out_row + tap - SSD_CONV // 2).astype(BF16).reshape(4 * tc, tc)
    return pl.pallas_call(
        functools.partial(_conv_kernel, tc=tc, n_p=n_p, lp=lp, ls=ls, halo=halo),
        grid=(m // tc, ch // tcn),
        in_specs=[pl.BlockSpec((halo, tcn), lambda i, j: (jnp.maximum(i * hb - 1, 0), j)),
                  pl.BlockSpec((tc, tcn), lambda i, j: (i, j)),
                  pl.BlockSpec((halo, tcn), lambda i, j: (jnp.minimum((i + 1) * hb, nhb - 1), j)),
                  pl.BlockSpec((SSD_CONV, tcn), lambda i, j: (0, j)),
                  pl.BlockSpec((1, tcn), lambda i, j: (0, j)),
                  pl.BlockSpec((4 * tc, tc), lambda i, j: (0, 0))],
        out_specs=pl.BlockSpec((tc, tcn), lambda i, j: (i, j)),
        out_shape=jax.ShapeDtypeStruct((m, ch), BF16),
        compiler_params=_params("parallel", "arbitrary"),
        name=name,
    )(xbc, xbc, xbc, conv_w, conv_b.reshape(1, ch), shift)


def _ssd_kernel(xbc_ref, dtraw_ref, dtb_ref, alog_ref, tri_ref, *rest, direction, zero_init, nt):
    y_ref, fin_ref, st_ref = rest[-3:]
    if not zero_init:
        h0_ref = rest[0]
    t = pl.program_id(1)
    q = SSD_CHUNK
    hpg = SSD_HEADS // SSD_GROUPS

    @pl.when(t == 0)
    def _():
        for g in range(SSD_GROUPS):
            if zero_init:
                st_ref[g] = jnp.zeros((SSD_STATE, GROUP_W), F32)
            else:
                hg = h0_ref[0, 0, g * hpg:(g + 1) * hpg].reshape(GROUP_W, SSD_STATE)
                st_ref[g] = hg.T

    x = dtraw_ref[...] + dtb_ref[...]
    dt = jnp.maximum(x, 0.0) + jnp.log1p(jnp.exp(-jnp.abs(x)))
    a = dt * (-jnp.exp(alog_ref[...]) * LOG2_E)
    tri = tri_ref[...]
    a1 = a.astype(BF16)
    r1 = a - a1.astype(F32)
    a2 = r1.astype(BF16)
    a3 = (r1 - a2.astype(F32)).astype(BF16)
    cs = (jnp.dot(tri, a1, preferred_element_type=F32) + jnp.dot(tri, a2, preferred_element_type=F32)
          + jnp.dot(tri, a3, preferred_element_type=F32))
    tot = cs[q - 1:q, :] if direction == 0 else cs[0:1, :]
    cdec = jnp.exp2(tot)
    cs_t = cs.T
    dt_t = dt.T
    wend_t = (dt * jnp.exp2(tot - cs)).T
    li = lax.broadcasted_iota(jnp.int32, (q, q), 0)
    si = lax.broadcasted_iota(jnp.int32, (q, q), 1)
    mask = (li >= si) if direction == 0 else (li <= si)
    low = lax.broadcasted_iota(jnp.int32, (q, LANES), 1) < SSD_HEAD_DIM
    low_row = low[0:1, :]

    for g in range(SSD_GROUPS):
        bg = xbc_ref[:, SSD_D_INNER + g * SSD_STATE:SSD_D_INNER + (g + 1) * SSD_STATE]
        cg = xbc_ref[:, SSD_D_INNER + SSD_BC_DIM + g * SSD_STATE:SSD_D_INNER + SSD_BC_DIM + (g + 1) * SSD_STATE]
        cg32 = cg.astype(F32)
        bg_t = bg.astype(F32).T
        cb = lax.dot_general(cg, bg, (((1,), (1,)), ((), ())), preferred_element_type=F32)
        for p in range(hpg // 2):
            col0 = g * GROUP_W + p * LANES
            xp = xbc_ref[:, col0:col0 + LANES]
            zero = jnp.zeros_like(xp)
            x_lo = jnp.where(low, xp, zero)
            x_hi = jnp.where(low, zero, xp)
            s_in = st_ref[g, :, p * LANES:(p + 1) * LANES]
            s_bf = s_in.astype(BF16)
            s_lo = jnp.where(low, s_bf, zero)
            s_hi = jnp.where(low, zero, s_bf)
            c0 = direction * SSD_HEADS + g * hpg + 2 * p
            diag, offd, bw = [], [], []
            for c in (c0, c0 + 1):
                col = jnp.broadcast_to(cs[:, c:c + 1], (q, q))
                seg = col - cs_t[c:c + 1, :]
                diag.append((cb * jnp.exp2(jnp.where(mask, seg, -jnp.inf)) * dt_t[c:c + 1, :]).astype(BF16))
                offd.append((cg32 * jnp.exp2(col)).astype(BF16))
                bw.append((bg_t * wend_t[c:c + 1, :]).astype(BF16))
            lhs = jnp.concatenate(diag + offd, axis=1)
            rhs = jnp.concatenate([x_lo, x_hi, s_lo, s_hi], axis=0)
            y_ref[:, col0:col0 + LANES] = jnp.dot(lhs, rhs, preferred_element_type=F32).astype(y_ref.dtype)
            new = jnp.dot(jnp.concatenate(bw, axis=1), jnp.concatenate([x_lo, x_hi], axis=0),
                          preferred_element_type=F32)
            keep = jnp.where(low_row, cdec[:, c0:c0 + 1], cdec[:, c0 + 1:c0 + 2])
            st_ref[g, :, p * LANES:(p + 1) * LANES] = s_in * keep + new

    @pl.when(t == nt - 1)
    def _():
        for g in range(SSD_GROUPS):
            fin_ref[0, g * hpg:(g + 1) * hpg] = st_ref[g].T.reshape(hpg, SSD_HEAD_DIM, SSD_STATE)


def _ssd_scan_call(xbc_act, dt_raw, dt_bias, a_log, h0, y_into, fin_into, *, nb, seq, row_off, direction,
                   name):
    q = SSD_CHUNK
    nt = seq // q
    boff = row_off // q
    li = jnp.arange(q)[:, None]
    si = jnp.arange(q)[None, :]
    tri = ((li >= si) if direction == 0 else (li <= si)).astype(BF16)

    def chunk(b, t):
        return boff + b * nt + (t if direction == 0 else nt - 1 - t)

    zero_init = h0 is None
    in_specs = [pl.BlockSpec((q, SSD_CONV_CH), lambda b, t: (chunk(b, t), 0)),
                pl.BlockSpec((q, LANES), lambda b, t: (chunk(b, t), 0)),
                pl.BlockSpec((1, LANES), lambda b, t: (0, 0)),
                pl.BlockSpec((1, LANES), lambda b, t: (0, 0)),
                pl.BlockSpec((q, q), lambda b, t: (0, 0))]
    args = [xbc_act, dt_raw, dt_bias.reshape(1, LANES), a_log.reshape(1, LANES), tri]
    if not zero_init:
        in_specs.append(pl.BlockSpec((1, 1, SSD_HEADS, SSD_HEAD_DIM, SSD_STATE),
                                     lambda b, t: (b, direction, 0, 0, 0)))
        args.append(h0)
    aliases = {}
    if y_into is not None:
        in_specs.append(pl.BlockSpec(memory_space=pl.ANY))
        aliases = {len(args): 0}
        args.append(y_into)
    if fin_into is not None:
        in_specs.append(pl.BlockSpec(memory_space=pl.ANY))
        aliases[len(args)] = 1
        args.append(fin_into)
    return pl.pallas_call(
        functools.partial(_ssd_kernel, direction=direction, zero_init=zero_init, nt=nt),
        grid=(nb, nt),
        in_specs=in_specs,
        out_specs=[pl.BlockSpec((q, SSD_D_INNER), lambda b, t: (chunk(b, t), 0)),
                   pl.BlockSpec((1, None, SSD_HEADS, SSD_HEAD_DIM, SSD_STATE),
                                lambda b, t: (b, direction, 0, 0, 0))],
        out_shape=[jax.ShapeDtypeStruct((xbc_act.shape[0], SSD_D_INNER), BF16),
                   jax.ShapeDtypeStruct((nb, 2, SSD_HEADS, SSD_HEAD_DIM, SSD_STATE), F32)],
        scratch_shapes=[pltpu.VMEM((SSD_GROUPS, SSD_STATE, GROUP_W), F32)],
        input_output_aliases=aliases,
        compiler_params=_params("parallel", "arbitrary"),
        name=name,
    )(*args)


def _ssd_post_kernel(yf_ref, yb_ref, xs_ref, z_ref, d_ref, nw_ref, o_ref):
    z = z_ref[...].astype(F32)
    y = yf_ref[...].astype(F32) + yb_ref[...].astype(F32) + d_ref[...] * xs_ref[...].astype(F32)
    y = y * (z / (1.0 + jnp.exp(-z)))
    outs = []
    for g in range(SSD_GROUPS):
        yg = y[:, g * GROUP_W:(g + 1) * GROUP_W]
        ms = jnp.mean(yg * yg, axis=-1, keepdims=True)
        outs.append(yg * lax.rsqrt(ms + NORM_EPS) * nw_ref[:, g * GROUP_W:(g + 1) * GROUP_W])
    o_ref[...] = jnp.concatenate(outs, axis=1).astype(o_ref.dtype)


def _ssd_post_call(y_f, y_b, xbc_act, z, d_tot, norm_w, *, tm, name):
    m = y_f.shape[0]
    di = SSD_D_INNER
    row = pl.BlockSpec((tm, di), lambda i: (i, 0))
    vec = pl.BlockSpec((1, di), lambda i: (0, 0))
    return pl.pallas_call(
        _ssd_post_kernel,
        grid=(m // tm,),
        in_specs=[row, row, row, row, vec, vec],
        out_specs=row,
        out_shape=jax.ShapeDtypeStruct((m, di), BF16),
        compiler_params=_params("parallel"),
        name=name,
    )(y_f, y_b, xbc_act, z, d_tot, norm_w.reshape(1, di))


def _run_schedule(block_e, n_used):
    nblk = block_e.shape[0]
    idx = jnp.arange(nblk, dtype=jnp.int32)
    prev = jnp.concatenate([block_e[:1] - 1, block_e[:-1]])
    first = ((idx < n_used[0]) & (block_e != prev)).astype(jnp.int32)
    ridx = jnp.cumsum(first) - 1
    is_next = (first[None, :] == 1) & (ridx[None, :] == ridx[:, None] + 1)
    next_e = jnp.where(jnp.any(is_next, axis=1), jnp.sum(jnp.where(is_next, block_e[None, :], 0), axis=1), -1)
    counts = jnp.stack([n_used[0], jnp.sum(first)])
    return (block_e, counts.astype(jnp.int32), first, ridx.astype(jnp.int32), next_e.astype(jnp.int32))


def _weight_copies(w_hbm, stage, sems, layer, e, jj, slot, col_tiles, tn):
    return [pltpu.make_async_copy(w_hbm.at[layer, e, :, pl.ds(pl.multiple_of((off + jj) * tn, tn), tn)],
                                  stage.at[slot, wi], sems.at[slot, wi])
            for wi, off in enumerate(col_tiles)]


def _stage_run_weights(sched, w_hbm, stage, sems, w_bf, *, layer, col_tiles, tn, nj):
    be_ref, cnt_ref, first_ref, ridx_ref, next_ref = sched
    j = pl.program_id(0)
    i = pl.program_id(1)
    copies = functools.partial(_weight_copies, w_hbm, stage, sems, layer, col_tiles=col_tiles, tn=tn)

    @pl.when(first_ref[i] == 1)
    def _():
        slot = (j * cnt_ref[1] + ridx_ref[i]) & 1
        e = be_ref[i]

        @pl.when((i == 0) & (j == 0))
        def _():
            for c in copies(e, j, slot):
                c.start()

        for c in copies(e, j, slot):
            c.wait()
        def cast_rows(r, carry):
            rows = pl.ds(pl.multiple_of(r * CAST_ROWS, CAST_ROWS), CAST_ROWS)
            for wi in range(len(col_tiles)):
                w_bf[wi, rows, :] = stage[slot, wi, rows, :].astype(BF16)
            return carry

        lax.fori_loop(0, w_bf.shape[1] // CAST_ROWS, cast_rows, 0)
        nxt = next_ref[i]

        @pl.when(nxt >= 0)
        def _():
            for c in copies(nxt, j, 1 - slot):
                c.start()

        @pl.when((nxt < 0) & (j + 1 < nj))
        def _():
            for c in copies(be_ref[0], j + 1, 1 - slot):
                c.start()


def _moe_up_kernel(*refs, layer, tn, nj):
    sched, (x_ref, w_hbm, bg_ref, bu_ref, o_ref, stage, w_bf, sems) = refs[:5], refs[5:]
    i = pl.program_id(1)
    _stage_run_weights(sched, w_hbm, stage, sems, w_bf, layer=layer, col_tiles=(0, nj), tn=tn, nj=nj)
    nu_ref = sched[1]

    @pl.when(i < nu_ref[0])
    def _():
        x = x_ref[...]
        g = jnp.dot(x, w_bf[0], preferred_element_type=F32) + bg_ref[...]
        u = jnp.dot(x, w_bf[1], preferred_element_type=F32) + bu_ref[...]
        g = jnp.minimum(g, SWIGLU_LIMIT)
        u = jnp.clip(u, -SWIGLU_LIMIT, SWIGLU_LIMIT)
        act = (u + 1.0) * (g / (1.0 + jnp.exp2((-SWIGLU_ALPHA * LOG2_E) * g)))
        o_ref[...] = act.astype(o_ref.dtype)

    @pl.when(i >= nu_ref[0])
    def _():
        o_ref[...] = jnp.zeros_like(o_ref)


def _moe_down_kernel(*refs, layer, tn, nj):
    sched, (a_ref, w_hbm, b_ref), (o_ref, stage, w_bf, sems) = refs[:5], refs[5:8], refs[-4:]
    i = pl.program_id(1)
    _stage_run_weights(sched, w_hbm, stage, sems, w_bf, layer=layer, col_tiles=(0,), tn=tn, nj=nj)
    nu_ref = sched[1]

    @pl.when(i < nu_ref[0])
    def _():
        y = jnp.dot(a_ref[...], w_bf[0], preferred_element_type=F32) + b_ref[...]
        o_ref[...] = y.astype(o_ref.dtype)

    @pl.when(i >= nu_ref[0])
    def _():
        o_ref[...] = jnp.zeros_like(o_ref)


def _moe_experts(x_sorted, block_e, n_used, y_into, layer, w_gate_up, b_gate_up, w_down, b_down, *, blk_off,
                 total_slots, tn_up, tn_down, name):
    slots, d = x_sorted.shape
    nblk = slots // MOE_BLOCK
    nj = D_FF // tn_up
    nl = w_gate_up.shape[0]
    b_gu = b_gate_up.reshape(nl, N_EXPERTS, 1, 2 * D_FF)
    sched = _run_schedule(block_e, n_used)
    act = pl.pallas_call(
        functools.partial(_moe_up_kernel, layer=layer, tn=tn_up, nj=nj),
        grid_spec=pltpu.PrefetchScalarGridSpec(
            num_scalar_prefetch=len(sched),
            grid=(nj, nblk),
            in_specs=[pl.BlockSpec((MOE_BLOCK, d), lambda j, i, *s: (i, 0)),
                      pl.BlockSpec(memory_space=pl.ANY),
                      pl.BlockSpec((None, None, 1, tn_up), lambda j, i, be, *s: (layer, be[i], 0, j)),
                      pl.BlockSpec((None, None, 1, tn_up), lambda j, i, be, *s: (layer, be[i], 0, nj + j))],
            out_specs=pl.BlockSpec((MOE_BLOCK, tn_up), lambda j, i, *s: (i, j)),
            scratch_shapes=[pltpu.VMEM((2, 2, d, tn_up), F32), pltpu.VMEM((2, d, tn_up), BF16),
                            pltpu.SemaphoreType.DMA((2, 2))]),
        out_shape=jax.ShapeDtypeStruct((slots, D_FF), BF16),
        compiler_params=_params("arbitrary", "arbitrary"),
        name=name + "_up",
    )(*sched, x_sorted, w_gate_up, b_gu, b_gu)
    nj2 = d // tn_down
    in_specs = [pl.BlockSpec((MOE_BLOCK, D_FF), lambda j, i, *s: (i, 0)),
                pl.BlockSpec(memory_space=pl.ANY),
                pl.BlockSpec((None, None, 1, tn_down), lambda j, i, be, *s: (layer, be[i], 0, j))]
    args = [*sched, act, w_down, b_down.reshape(nl, N_EXPERTS, 1, d)]
    aliases = {}
    if y_into is not None:
        in_specs.append(pl.BlockSpec(memory_space=pl.ANY))
        aliases = {len(args): 0}
        args.append(y_into)
    return pl.pallas_call(
        functools.partial(_moe_down_kernel, layer=layer, tn=tn_down, nj=nj2),
        grid_spec=pltpu.PrefetchScalarGridSpec(
            num_scalar_prefetch=len(sched),
            grid=(nj2, nblk),
            in_specs=in_specs,
            out_specs=pl.BlockSpec((MOE_BLOCK, tn_down), lambda j, i, *s: (i + blk_off, j)),
            scratch_shapes=[pltpu.VMEM((2, 1, D_FF, tn_down), F32), pltpu.VMEM((1, D_FF, tn_down), BF16),
                            pltpu.SemaphoreType.DMA((2, 1))]),
        out_shape=jax.ShapeDtypeStruct((total_slots, d), BF16),
        input_output_aliases=aliases,
        compiler_params=_params("arbitrary", "arbitrary"),
        name=name + "_down",
    )(*args)


def _combine_kernel(*refs, next_norm):
    y_refs = refs[:TOP_K]
    gate_ref, x_ref, g2_ref = refs[TOP_K:TOP_K + 3]
    gates = gate_ref[...]
    f = None
    for k in range(TOP_K):
        term = gates[:, k:k + 1] * y_refs[k][...].astype(F32)
        f = term if f is None else f + term
    out = x_ref[...] + g2_ref[0] * f
    if next_norm:
        nw_ref, sh_ref, sc_ref = refs[TOP_K + 3:TOP_K + 6]
        o_ref, h_ref = refs[-2:]
        o_ref[...] = out
        h_ref[...] = _modnorm(out, nw_ref[...], sh_ref[0], sc_ref[0]).astype(h_ref.dtype)
    else:
        refs[-1][...] = out


def _combine_rows_call(y4, gate_pad, x, mod, out_into, next_norm, *, row_off, m, out_rows, out_off, n_p, ls, tm,
                       name):
    d = x.shape[1]
    nt = m // tm
    xoff = row_off // tm
    ooff = out_off // tm

    def mod_spec(which):
        return pl.BlockSpec((1, 1, d), lambda i, j: (_group_of_tile(i + xoff, tm, n_p, ls) * 6 + which, 0, 0))

    y_specs = [pl.BlockSpec((tm, d), functools.partial(lambda i, j, k: (k * nt + i, 0), k=k))
               for k in range(TOP_K)]
    in_specs = y_specs + [pl.BlockSpec((tm, LANES), lambda i, j: (i + xoff, 0)),
                          pl.BlockSpec((tm, d), lambda i, j: (i + xoff, 0)),
                          mod_spec(5)]
    args = [y4] * TOP_K + [gate_pad, x, mod]
    out_spec = pl.BlockSpec((tm, d), lambda i, j: (i + ooff, 0))
    out_specs, out_shape = [out_spec], [jax.ShapeDtypeStruct((out_rows, d), F32)]
    into = [out_into]
    if next_norm is not None:
        nw, mod_next, h_into = next_norm
        in_specs += [pl.BlockSpec((1, d), lambda i, j: (0, 0)), mod_spec(0), mod_spec(1)]
        args += [nw, mod_next, mod_next]
        out_specs.append(out_spec)
        out_shape.append(jax.ShapeDtypeStruct((out_rows, d), BF16))
        into.append(h_into)
    aliases = {}
    for oi, buf in enumerate(into):
        if buf is not None:
            in_specs.append(pl.BlockSpec(memory_space=pl.ANY))
            aliases[len(args)] = oi
            args.append(buf)
    res = pl.pallas_call(
        functools.partial(_combine_kernel, next_norm=next_norm is not None),
        grid=(nt, 1),
        in_specs=in_specs,
        out_specs=out_specs,
        out_shape=out_shape,
        input_output_aliases=aliases,
        compiler_params=_params("parallel", "arbitrary"),
        name=name,
    )(*args)
    return res if next_norm is not None else res[0]


def _moe_layer(x, mod, n_p, ls, layer, norm_w, w_router, b_router, w_gate_up, b_gate_up, w_down, b_down, *,
               split, name, next_norm=None):
    assert not (split and next_norm is not None)
    n, d = x.shape
    h, idx_pad, gate_pad, rank_pad, cnt = _router_call(x, norm_w.reshape(1, d), mod, n_p, ls, w_router,
                                                       b_router, tm=256, name=name + "_router")
    n_assign = n * TOP_K
    flat_e = idx_pad[:, :TOP_K].reshape(-1)
    counts = cnt[0, :N_EXPERTS]
    padded = (counts + MOE_BLOCK - 1) // MOE_BLOCK * MOE_BLOCK
    pad_end = jnp.cumsum(padded)
    pad_start = pad_end - padded
    dest = pad_start[flat_e] + rank_pad[:, :TOP_K].reshape(-1)
    n_blocks = -(-n_assign // MOE_BLOCK) + N_EXPERTS
    slots = n_blocks * MOE_BLOCK
    slot_tok = (jnp.arange(slots, dtype=jnp.int32) % n).at[dest].set(
        jnp.arange(n_assign, dtype=jnp.int32) // TOP_K, unique_indices=True, mode="promise_in_bounds")
    block_start = jnp.arange(n_blocks, dtype=jnp.int32) * MOE_BLOCK
    block_e = jnp.minimum(jnp.sum((pad_end[None, :] <= block_start[:, None]).astype(jnp.int32), axis=1),
                          N_EXPERTS - 1)
    n_used = (pad_end[-1] // MOE_BLOCK).astype(jnp.int32).reshape(1)
    cb = n_blocks // MOE_RANGES
    y_sorted = None
    for c in range(MOE_RANGES):
        rows = slice(c * cb * MOE_BLOCK, (c + 1) * cb * MOE_BLOCK)
        x_sorted = h.at[slot_tok[rows]].get(mode="promise_in_bounds")
        used = jnp.clip(n_used - c * cb, 0, cb)
        y_sorted = _moe_experts(x_sorted, block_e[c * cb:(c + 1) * cb], used, y_sorted, layer, w_gate_up,
                                b_gate_up, w_down, b_down, blk_off=c * cb, total_slots=slots, tn_up=1024,
                                tn_down=2048, name="%s_r%d" % (name, c))
    n_s = n - n_p
    ranges = [(0, n_p), (n_p, n_s // 2), (n_p + n_s // 2, n_s - n_s // 2)]
    dest_tk = dest.reshape(n, TOP_K)
    outs = [None, None]
    h_next = None
    for ri, (t0, m) in enumerate(ranges):
        idx = dest_tk[t0:t0 + m].T.reshape(-1)
        y4 = y_sorted.at[idx].get(mode="promise_in_bounds")
        if split:
            which, out_rows, out_off = (0, n_p, 0) if ri == 0 else (1, n_s, t0 - n_p)
        else:
            which, out_rows, out_off = 0, n, t0
        nn = None if next_norm is None else (next_norm[0], next_norm[1], h_next)
        res = _combine_rows_call(y4, gate_pad, x, mod, outs[which], nn, row_off=t0, m=m, out_rows=out_rows,
                                 out_off=out_off, n_p=n_p, ls=ls, tm=256, name="%s_combine%d" % (name, ri))
        if next_norm is None:
            outs[which] = res
        else:
            outs[which], h_next = res
    if next_norm is not None:
        return outs[0], h_next
    return (outs[0], outs[1]) if split else outs[0]


def _ada_call(cond, w_ada, b_ada, layer, *, name):
    g = cond.shape[0]
    a = jnp.pad(jax.nn.silu(cond), ((0, 16 - g), (0, 0))).astype(BF16)
    m = _matmul(a, w_ada, n_out=6 * D_MODEL, tm=16, tn=1024, out_dtype=F32, name=name, layer=layer,
                bias=b_ada.reshape(b_ada.shape[0], 1, 6 * D_MODEL))
    return m[:g].reshape(g * 6, 1, D_MODEL)


def kernel(x_prompt, x_sample, c, c_ctx, cache_k, cache_v, state_ssm, norm1_w, norm2_w, w_ada, b_ada, w_qkv, q_norm_w, k_norm_w, lambda_q1, lambda_k1, lambda_q2, lambda_k2, subln_w, w_o, w_in_ssd, conv_w, conv_b, dt_bias, a_log, d_skip, ssd_norm_w, w_out_ssd, w_router, b_router, w_gate_up, b_gate_up, w_down, b_down):
    bp, lp, d = x_prompt.shape
    bs, ls, _ = x_sample.shape
    past = cache_k.shape[2]
    n_p, n_s = bp * lp, bs * ls
    xp0, xs0 = x_prompt.reshape(n_p, d), x_sample.reshape(n_s, d)
    cond = jnp.concatenate([c_ctx[None], c], axis=0)
    resid_of = lambda xx, mod, which: (xx, mod, which, n_p, ls)

    n = n_p + n_s
    mod = _ada_call(cond, w_ada, b_ada, 0, name="ada0")
    wq = w_qkv[0].astype(BF16)
    qk_w = jnp.stack([q_norm_w[0], k_norm_w[0]], axis=0)
    lam_init = 0.8 - 0.6 * math.exp(-0.3 * 0)
    lam_params = jnp.stack([lambda_q1[0], lambda_k1[0], lambda_q2[0], lambda_k2[0]], axis=0)
    qp, kp, vp = _qkv_call(xp0, norm1_w[0], mod, wq, qk_w, None, row_off=0, n_p=n_p, kv_dtype=F32, tm=512,
                           tn=512, ls=ls, name="qkv_prompt")
    qs, ks, vs = _qkv_call(xs0, norm1_w[0], mod, wq, qk_w, _rope_tables(ls), row_off=n_p, n_p=n_p,
                           kv_dtype=BF16, tm=512, tn=512, ls=ls, name="qkv_sample")
    o = _attn_call(qp, [(kp, vp, lp)], lam_params, subln_w[0], None, n_rows=n, row_off=0, nb=bp, lq=lp,
                   tq=lp, lam_init=lam_init, name="attn_prompt")
    ck = cache_k[:, 0].reshape(bs * past, d)
    cv = cache_v[:, 0].reshape(bs * past, d)
    o = _attn_call(qs, [(ck, cv, past), (ks, vs, ls)], lam_params, subln_w[0], o, n_rows=n, row_off=n_p,
                   nb=bs, lq=ls, tq=256, lam_init=lam_init, name="attn_sample")
    wo = w_o[0].astype(BF16)
    x = _resid_rows_matmul(o, wo, xp0, mod, 2, None, row_off=0, n_p=n_p, ls=ls, tm=1024, tn=512,
                           name="attn_out_prompt")
    x = _resid_rows_matmul(o, wo, xs0, mod, 2, x, row_off=n_p, n_p=n_p, ls=ls, tm=1024, tn=512,
                           name="attn_out_sample")
    mod1 = _ada_call(cond, w_ada, b_ada, 1, name="ada1")
    x, h = _moe_layer(x, mod, n_p, ls, 0, norm2_w[0], w_router[0], b_router[0], w_gate_up, b_gate_up,
                      w_down, b_down, split=False, name="moe0", next_norm=(norm1_w[1].reshape(1, d), mod1))
    new_k = kp.reshape(bp, 1, lp, ATTN_HEADS, 2, QK_DIM)
    new_v = vp.reshape(bp, 1, lp, ATTN_HEADS, V_DIM)

    mod = mod1
    w_in = w_in_ssd[0].astype(BF16)
    z = _matmul(h, w_in, n_out=SSD_D_INNER, tm=1024, tn=512, out_dtype=BF16, name="ssd_in_z")
    xbc = _matmul(h, w_in, n_out=SSD_CONV_CH, col_off=SSD_D_INNER, tm=1024, tn=512, out_dtype=BF16,
                  name="ssd_in_xbc")
    dt_raw = _matmul(h, w_in, n_out=2 * SSD_HEADS, col_off=SSD_D_INNER + SSD_CONV_CH, tm=1024, tn=LANES,
                     out_dtype=F32, name="ssd_in_dt")
    xbc_act = _conv_call(xbc, conv_w[0], conv_b[0], n_p=n_p, lp=lp, ls=ls, tc=256, tcn=2048, name="ssd_conv")
    ys, fin = [], None
    for direction in (0, 1):
        y, fin = _ssd_scan_call(xbc_act, dt_raw, dt_bias[0], a_log[0], None, None, fin, nb=bp, seq=lp,
                                row_off=0, direction=direction, name="ssd_scan_prompt%d" % direction)
        y, _ = _ssd_scan_call(xbc_act, dt_raw, dt_bias[0], a_log[0], state_ssm[:, 0], y, None, nb=bs, seq=ls,
                              row_off=n_p, direction=direction, name="ssd_scan_sample%d" % direction)
        ys.append(y)
    d_tot = jnp.repeat(d_skip[0, 0] + d_skip[0, 1], SSD_HEAD_DIM).reshape(1, SSD_D_INNER)
    yn = _ssd_post_call(ys[0], ys[1], xbc_act, z, d_tot, ssd_norm_w[0], tm=256, name="ssd_post")
    x = _matmul(yn, w_out_ssd[0].astype(BF16), n_out=d, tm=1024, tn=512, out_dtype=F32, name="ssd_out",
                resid=resid_of(x, mod, 2))
    xp, xs = _moe_layer(x, mod, n_p, ls, 1, norm2_w[1], w_router[1], b_router[1], w_gate_up, b_gate_up,
                        w_down, b_down, split=True, name="moe1")
    new_s = fin[:, None]
    return (xp.reshape(bp, lp, d), xs.reshape(bs, ls, d), new_k, new_v, new_s)
```

```python
import functools
import math

import jax
import jax.numpy as jnp
from jax import lax
from jax.experimental import pallas as pl
from jax.experimental.pallas import tpu as pltpu

F32 = jnp.float32
BF16 = jnp.bfloat16

D_MODEL = 2048
NORM_EPS = 1e-6
GRID_W = 64
ROPE_THETA = 10000.0
ATTN_HEADS = 8
QK_DIM = 128
V_DIM = 256
SSD_D_INNER = 4096
SSD_HEAD_DIM = 64
SSD_HEADS = 64
SSD_GROUPS = 8
SSD_STATE = 128
SSD_CONV = 5
SSD_CHUNK = 128
SSD_BC_DIM = SSD_GROUPS * SSD_STATE
SSD_CONV_CH = SSD_D_INNER + 2 * SSD_BC_DIM
N_EXPERTS = 32
TOP_K = 4
D_FF = 2048
SWIGLU_LIMIT = 7.0
SWIGLU_ALPHA = 1.702
MOE_BLOCK = 512
MOE_RANGES = 4
ATTN_KEY_CHUNK = 512
CAST_ROWS = 256
LOG2_E = math.log2(math.e)
LANES = 128
GROUP_W = SSD_D_INNER // SSD_GROUPS
assert SSD_CHUNK == SSD_STATE == LANES and 2 * SSD_HEAD_DIM == LANES
VMEM_LIMIT = 56 * 1024 * 1024


def _params(*sem):
    return pltpu.CompilerParams(dimension_semantics=sem, vmem_limit_bytes=VMEM_LIMIT)


def _group_of_tile(i, tm, n_p, ls):
    return jnp.where(i * tm < n_p, 0, 1 + (i * tm - n_p) // ls)


def _mod_spec(which, tm, tn, n_p, ls):
    return pl.BlockSpec((1, 1, tn), lambda i, j: (_group_of_tile(i, tm, n_p, ls) * 6 + which, 0, j))


def _mm_kernel(a_ref, w_ref, *rest, epilogue):
    acc = jnp.dot(a_ref[...].astype(BF16), w_ref[...].astype(BF16), preferred_element_type=F32)
    if epilogue == "bias":
        b_ref, o_ref = rest
        o_ref[...] = (acc + b_ref[...]).astype(o_ref.dtype)
    elif epilogue == "resid":
        x_ref, g_ref, o_ref = rest[0], rest[1], rest[-1]
        o_ref[...] = x_ref[...] + g_ref[0] * acc
    else:
        (o_ref,) = rest
        o_ref[...] = acc.astype(o_ref.dtype)


def _matmul(a, w, *, n_out, col_off=0, tm, tn, out_dtype, name, layer=None, bias=None, resid=None):
    m, k = a.shape
    joff = col_off // tn
    if layer is None:
        w_spec = pl.BlockSpec((k, tn), lambda i, j: (0, j + joff))
        b_spec = pl.BlockSpec((1, tn), lambda i, j: (0, j))
    else:
        w_spec = pl.BlockSpec((None, k, tn), lambda i, j: (layer, 0, j + joff))
        b_spec = pl.BlockSpec((None, 1, tn), lambda i, j: (layer, 0, j))
    in_specs = [pl.BlockSpec((tm, k), lambda i, j: (i, 0)), w_spec]
    args = [a, w]
    if bias is not None:
        epilogue = "bias"
        in_specs.append(b_spec)
        args.append(bias)
    elif resid is not None:
        epilogue = "resid"
        x, mod, which, n_p, ls = resid
        in_specs += [pl.BlockSpec((tm, tn), lambda i, j: (i, j)), _mod_spec(which, tm, tn, n_p, ls)]
        args += [x, mod]
    else:
        epilogue = "plain"
    return pl.pallas_call(
        functools.partial(_mm_kernel, epilogue=epilogue),
        grid=(m // tm, n_out // tn),
        in_specs=in_specs,
        out_specs=pl.BlockSpec((tm, tn), lambda i, j: (i, j)),
        out_shape=jax.ShapeDtypeStruct((m, n_out), out_dtype),
        compiler_params=_params("parallel", "arbitrary"),
        name=name,
    )(*args)


def _resid_rows_matmul(a_table, w, x_rows, mod, which, out_into, *, row_off, n_p, ls, tm, tn, name):
    n_rows, k = a_table.shape
    m, n_out = x_rows.shape
    ioff = row_off // tm
    in_specs = [pl.BlockSpec((tm, k), lambda i, j: (i + ioff, 0)),
                pl.BlockSpec((k, tn), lambda i, j: (0, j)),
                pl.BlockSpec((tm, tn), lambda i, j: (i, j)),
                pl.BlockSpec((1, 1, tn), lambda i, j: (_group_of_tile(i + ioff, tm, n_p, ls) * 6 + which, 0, j))]
    args = [a_table, w, x_rows, mod]
    aliases = {}
    if out_into is not None:
        in_specs.append(pl.BlockSpec(memory_space=pl.ANY))
        aliases = {len(args): 0}
        args.append(out_into)
    return pl.pallas_call(
        functools.partial(_mm_kernel, epilogue="resid"),
        grid=(m // tm, n_out // tn),
        in_specs=in_specs,
        out_specs=pl.BlockSpec((tm, tn), lambda i, j: (i + ioff, j)),
        out_shape=jax.ShapeDtypeStruct((n_rows, n_out), F32),
        input_output_aliases=aliases,
        compiler_params=_params("parallel", "arbitrary"),
        name=name,
    )(*args)


def _modnorm(x, w, shift, scale):
    ms = jnp.mean(x * x, axis=-1, keepdims=True)
    return (x * lax.rsqrt(ms + NORM_EPS) * w) * (1.0 + scale) + shift


def _split2(v):
    hi = v.astype(BF16)
    lo = (v - hi.astype(F32)).astype(BF16)
    return hi, lo


def _router_kernel(x_ref, w_ref, sh_ref, sc_ref, wr_ref, br_ref, h_ref, idx_ref, gate_ref, rank_ref, cnt_ref,
                   run_ref):
    @pl.when(pl.program_id(0) == 0)
    def _():
        run_ref[...] = jnp.zeros_like(run_ref)

    h = _modnorm(x_ref[...], w_ref[...], sh_ref[0], sc_ref[0])
    h_ref[...] = h.astype(BF16)
    h_hi, h_lo = _split2(h)
    w_hi, w_lo = _split2(wr_ref[...])
    logits = (jnp.dot(h_hi, w_hi, preferred_element_type=F32)
              + jnp.dot(h_hi, w_lo, preferred_element_type=F32)
              + jnp.dot(h_lo, w_hi, preferred_element_type=F32)) + br_ref[...]
    lane = lax.broadcasted_iota(jnp.int32, logits.shape, 1).astype(F32)
    cur = jnp.where(lane < N_EXPERTS, logits, -jnp.inf)
    vals, idxs = [], []
    for _ in range(TOP_K):
        m = jnp.max(cur, axis=-1, keepdims=True)
        am = jnp.min(jnp.where(cur == m, lane, float(LANES)), axis=-1, keepdims=True)
        vals.append(m)
        idxs.append(am)
        cur = jnp.where(lane == am, -jnp.inf, cur)
    exps = [jnp.exp(v - vals[0]) for v in vals]
    denom = exps[0] + exps[1] + exps[2] + exps[3]
    idx_out = jnp.zeros(logits.shape, F32)
    gate_out = jnp.zeros(logits.shape, F32)
    for k in range(TOP_K):
        idx_out = jnp.where(lane == k, idxs[k], idx_out)
        gate_out = jnp.where(lane == k, exps[k] / denom, gate_out)
    idx_ref[...] = idx_out.astype(jnp.int32)
    gate_ref[...] = gate_out
    tm = logits.shape[0]
    chosen = functools.reduce(jnp.logical_or, [lane == am for am in idxs])
    onehot = jnp.where(chosen, 1.0, 0.0)
    ti = lax.broadcasted_iota(jnp.int32, (tm, tm), 0)
    tj = lax.broadcasted_iota(jnp.int32, (tm, tm), 1)
    earlier = jnp.where(ti > tj, 1.0, 0.0).astype(BF16)
    base = run_ref[...] + jnp.dot(earlier, onehot.astype(BF16), preferred_element_type=F32)
    rank_out = jnp.zeros(logits.shape, F32)
    for k in range(TOP_K):
        rk = jnp.sum(jnp.where(lane == idxs[k], base, 0.0), axis=-1, keepdims=True)
        rank_out = jnp.where(lane == k, rk, rank_out)
    rank_ref[...] = rank_out.astype(jnp.int32)
    run_ref[...] = run_ref[...] + jnp.sum(onehot, axis=0, keepdims=True)
    cnt_ref[...] = run_ref[...].astype(jnp.int32)


def _router_call(x, w, mod, n_p, ls, w_router, b_router, *, tm, name):
    m, d = x.shape
    wr = jnp.pad(w_router, ((0, 0), (0, LANES - N_EXPERTS)))
    br = jnp.pad(b_router, (0, LANES - N_EXPERTS)).reshape(1, LANES)
    return pl.pallas_call(
        _router_kernel,
        grid=(m // tm, 1),
        in_specs=[pl.BlockSpec((tm, d), lambda i, j: (i, 0)),
                  pl.BlockSpec((1, d), lambda i, j: (0, 0)),
                  _mod_spec(3, tm, d, n_p, ls),
                  _mod_spec(4, tm, d, n_p, ls),
                  pl.BlockSpec((d, LANES), lambda i, j: (0, 0)),
                  pl.BlockSpec((1, LANES), lambda i, j: (0, 0))],
        out_specs=[pl.BlockSpec((tm, d), lambda i, j: (i, 0)),
                   pl.BlockSpec((tm, LANES), lambda i, j: (i, 0)),
                   pl.BlockSpec((tm, LANES), lambda i, j: (i, 0)),
                   pl.BlockSpec((tm, LANES), lambda i, j: (i, 0)),
                   pl.BlockSpec((1, LANES), lambda i, j: (0, 0))],
        out_shape=[jax.ShapeDtypeStruct((m, d), BF16),
                   jax.ShapeDtypeStruct((m, LANES), jnp.int32),
                   jax.ShapeDtypeStruct((m, LANES), F32),
                   jax.ShapeDtypeStruct((m, LANES), jnp.int32),
                   jax.ShapeDtypeStruct((1, LANES), jnp.int32)],
        scratch_shapes=[pltpu.VMEM((1, LANES), F32)],
        compiler_params=_params("arbitrary", "arbitrary"),
        name=name,
    )(x, w, mod, mod, wr, br)


def _qkv_kernel(x_ref, n1_ref, sh_ref, sc_ref, w_ref, nw_ref, *rest, rope, tn):
    if rope:
        cos_ref, sin_ref, q_ref, k_ref, v_ref, h_s = rest
    else:
        q_ref, k_ref, v_ref, h_s = rest
    j = pl.program_id(1)
    nq = D_MODEL // tn

    @pl.when(j == 0)
    def _():
        h_s[...] = _modnorm(x_ref[...], n1_ref[...], sh_ref[0], sc_ref[0]).astype(BF16)

    acc = jnp.dot(h_s[...], w_ref[...], preferred_element_type=F32)

    def normed(widx, out_scale):
        w = nw_ref[widx:widx + 1, :]
        outs = []
        for c in range(tn // QK_DIM):
            xc = acc[:, c * QK_DIM:(c + 1) * QK_DIM]
            ms = jnp.mean(xc * xc, axis=-1, keepdims=True)
            y = xc * lax.rsqrt(ms + NORM_EPS) * w
            if rope:
                lane = lax.broadcasted_iota(jnp.int32, y.shape, 1)
                partner = jnp.where(lane % 64 < 32, pltpu.roll(y, QK_DIM - 32, 1), pltpu.roll(y, 32, 1))
                y = y * cos_ref[...] + partner * sin_ref[...]
            outs.append(y * out_scale if out_scale != 1.0 else y)
        return jnp.concatenate(outs, axis=1)

    @pl.when(j < nq)
    def _():
        q_ref[...] = normed(0, QK_DIM ** -0.5 * LOG2_E).astype(q_ref.dtype)

    @pl.when((j >= nq) & (j < 2 * nq))
    def _():
        k_ref[...] = normed(1, 1.0).astype(k_ref.dtype)

    @pl.when(j >= 2 * nq)
    def _():
        v_ref[...] = acc.astype(v_ref.dtype)


def _qkv_call(x_rows, norm1_w, mod, w_qkv, qk_norm_w, rope_tabs, *, row_off, n_p, kv_dtype, tm, tn, ls, name):
    m, d = x_rows.shape
    nq = d // tn
    ioff = row_off // tm
    rope = rope_tabs is not None

    def mod_spec(which):
        return pl.BlockSpec((1, 1, d), lambda i, j: (_group_of_tile(i + ioff, tm, n_p, ls) * 6 + which, 0, 0))

    in_specs = [pl.BlockSpec((tm, d), lambda i, j: (i, 0)),
                pl.BlockSpec((1, d), lambda i, j: (0, 0)),
                mod_spec(0), mod_spec(1),
                pl.BlockSpec((d, tn), lambda i, j: (0, j)),
                pl.BlockSpec((2, QK_DIM), lambda i, j: (0, 0))]
    args = [x_rows, norm1_w.reshape(1, d), mod, mod, w_qkv, qk_norm_w]
    if rope:
        nt = ls // tm
        in_specs += [pl.BlockSpec((tm, QK_DIM), lambda i, j: (i % nt, 0))] * 2
        args += list(rope_tabs)
    return pl.pallas_call(
        functools.partial(_qkv_kernel, rope=rope, tn=tn),
        grid=(m // tm, 3 * nq),
        in_specs=in_specs,
        out_specs=[pl.BlockSpec((tm, tn), lambda i, j: (i, jnp.minimum(j, nq - 1))),
                   pl.BlockSpec((tm, tn), lambda i, j: (i, jnp.clip(j - nq, 0, nq - 1))),
                   pl.BlockSpec((tm, tn), lambda i, j: (i, jnp.clip(j - 2 * nq, 0, nq - 1)))],
        out_shape=[jax.ShapeDtypeStruct((m, d), BF16),
                   jax.ShapeDtypeStruct((m, d), kv_dtype),
                   jax.ShapeDtypeStruct((m, d), kv_dtype)],
        scratch_shapes=[pltpu.VMEM((tm, d), BF16)],
        compiler_params=_params("parallel", "arbitrary"),
        name=name,
    )(*args)


def _rope_tables(ls):
    rows = ls // GRID_W
    row = jnp.repeat(jnp.arange(rows, dtype=F32), GRID_W)
    col = jnp.tile(jnp.arange(GRID_W, dtype=F32), rows)
    axis_dim = QK_DIM // 2
    inv = ROPE_THETA ** (-jnp.arange(0, axis_dim, 2, dtype=F32) / axis_dim)
    ar, ac = row[:, None] * inv, col[:, None] * inv
    cos = jnp.concatenate([jnp.cos(ar), jnp.cos(ar), jnp.cos(ac), jnp.cos(ac)], axis=1)
    sin = jnp.concatenate([-jnp.sin(ar), jnp.sin(ar), -jnp.sin(ac), jnp.sin(ac)], axis=1)
    return cos, sin


def _attn_kernel(lam_ref, subw_ref, q_ref, *rest, nseg, lam_init):
    k_refs, v_refs, o_ref = rest[:nseg], rest[nseg:2 * nseg], rest[-1]
    lp = lam_ref[...]
    lam = (jnp.exp(jnp.sum(lp[0:1] * lp[1:2], axis=-1, keepdims=True))
           - jnp.exp(jnp.sum(lp[2:3] * lp[3:4], axis=-1, keepdims=True)) + lam_init)
    q = q_ref[...]
    chunks = [(k_ref, v_ref, c0) for k_ref, v_ref in zip(k_refs, v_refs)
              for c0 in range(0, k_ref.shape[0], min(ATTN_KEY_CHUNK, k_ref.shape[0]))]
    outs = []
    for mi in range(2):
        qm = q[:, mi * QK_DIM:(mi + 1) * QK_DIM]
        mx = den = pv = None
        for k_ref, v_ref, c0 in chunks:
            kc = min(ATTN_KEY_CHUNK, k_ref.shape[0])
            k = k_ref[c0:c0 + kc, mi * QK_DIM:(mi + 1) * QK_DIM].astype(BF16)
            v = v_ref[c0:c0 + kc, :].astype(BF16)
            s = lax.dot_general(qm, k, (((1,), (1,)), ((), ())), preferred_element_type=F32)
            cmax = jnp.max(s, axis=-1, keepdims=True)
            if mx is None:
                mx = cmax
                p = jnp.exp2(s - mx)
                den = jnp.sum(p, axis=-1, keepdims=True)
                pv = jnp.dot(p.astype(BF16), v, preferred_element_type=F32)
            else:
                new_mx = jnp.maximum(mx, cmax)
                alpha = jnp.exp2(mx - new_mx)
                p = jnp.exp2(s - new_mx)
                den = den * alpha + jnp.sum(p, axis=-1, keepdims=True)
                pv = pv * alpha + jnp.dot(p.astype(BF16), v, preferred_element_type=F32)
                mx = new_mx
        outs.append(pv * (1.0 / den))
    acc = outs[0] - lam * outs[1]
    ms = jnp.mean(acc * acc, axis=-1, keepdims=True)
    o = acc * lax.rsqrt(ms + NORM_EPS) * subw_ref[...] * (1.0 - lam_init)
    o_ref[...] = o.astype(o_ref.dtype)


def _attn_call(q, kv_segs, lam_params, subln_w, o_into, *, n_rows, row_off, nb, lq, tq, lam_init, name):
    nqb = lq // tq
    nseg = len(kv_segs)
    ooff = row_off // tq
    in_specs = [pl.BlockSpec((4, QK_DIM), lambda b, h, i: (0, 0)),
                pl.BlockSpec((1, V_DIM), lambda b, h, i: (0, 0)),
                pl.BlockSpec((tq, V_DIM), lambda b, h, i: (b * nqb + i, h))]
    in_specs += [pl.BlockSpec((lk, V_DIM), lambda b, h, i: (b, h)) for (_, _, lk) in kv_segs] * 2
    args = [lam_params, subln_w.reshape(1, V_DIM), q]
    args += [k for (k, _, _) in kv_segs] + [v for (_, v, _) in kv_segs]
    aliases = {}
    if o_into is not None:
        in_specs.append(pl.BlockSpec(memory_space=pl.ANY))
        aliases = {len(args): 0}
        args.append(o_into)
    return pl.pallas_call(
        functools.partial(_attn_kernel, nseg=nseg, lam_init=lam_init),
        grid=(nb, ATTN_HEADS, nqb),
        in_specs=in_specs,
        out_specs=pl.BlockSpec((tq, V_DIM), lambda b, h, i: (ooff + b * nqb + i, h)),
        out_shape=jax.ShapeDtypeStruct((n_rows, q.shape[1]), BF16),
        input_output_aliases=aliases,
        compiler_params=_params("parallel", "parallel", "arbitrary"),
        name=name,
    )(*args)


def _conv_kernel(xp_ref, x_ref, xn_ref, w_ref, b_ref, shift_ref, o_ref, *, tc, n_p, lp, ls, halo):
    i = pl.program_id(0)
    row0 = i * tc
    in_prompt = row0 < n_p
    first = jnp.where(in_prompt, row0 % lp == 0, (row0 - n_p) % ls == 0)
    last = jnp.where(in_prompt, (row0 + tc) % lp == 0, (row0 + tc - n_p) % ls == 0)
    x = x_ref[...]
    shifted = jnp.dot(shift_ref[...], x, preferred_element_type=F32)
    acc = b_ref[...] + w_ref[2:3, :] * x.astype(F32)
    for si, k in enumerate((0, 1, 3, 4)):
        acc = acc + w_ref[k:k + 1, :] * shifted[si * tc:(si + 1) * tc, :]
    prev = jnp.where(first, 0.0, xp_ref[...].astype(F32)[halo - 8:halo, :])
    nxt = jnp.where(last, 0.0, xn_ref[...].astype(F32)[0:8, :])
    r8 = lax.broadcasted_iota(jnp.int32, prev.shape, 0)
    top = (w_ref[0:1, :] * jnp.where(r8 < 2, pltpu.roll(prev, 2, 0), 0.0)
           + w_ref[1:2, :] * jnp.where(r8 < 1, pltpu.roll(prev, 1, 0), 0.0))
    bot = (w_ref[3:4, :] * jnp.where(r8 >= 7, pltpu.roll(nxt, 7, 0), 0.0)
           + w_ref[4:5, :] * jnp.where(r8 >= 6, pltpu.roll(nxt, 6, 0), 0.0))
    acc = acc + jnp.concatenate([top, jnp.zeros((tc - 16, acc.shape[1]), F32), bot], axis=0)
    o_ref[...] = (acc / (1.0 + jnp.exp2(-LOG2_E * acc))).astype(o_ref.dtype)


def _conv_call(xbc, conv_w, conv_b, *, n_p, lp, ls, tc, tcn, name):
    m, ch = xbc.shape
    halo = 16
    hb = tc // halo
    nhb = m // halo
    out_row = jnp.arange(tc)[None, :, None]
    in_row = jnp.arange(tc)[None, None, :]
    tap = jnp.array([0, 1, 3, 4])[:, None, None]
    shift = (in_row == out_row + tap - SSD_CONV // 2).astype(BF16).reshape(4 * tc, tc)
    return pl.pallas_call(
        functools.partial(_conv_kernel, tc=tc, n_p=n_p, lp=lp, ls=ls, halo=halo),
        grid=(m // tc, ch // tcn),
        in_specs=[pl.BlockSpec((halo, tcn), lambda i, j: (jnp.maximum(i * hb - 1, 0), j)),
                  pl.BlockSpec((tc, tcn), lambda i, j: (i, j)),
                  pl.BlockSpec((halo, tcn), lambda i, j: (jnp.minimum((i + 1) * hb, nhb - 1), j)),
                  pl.BlockSpec((SSD_CONV, tcn), lambda i, j: (0, j)),
                  pl.BlockSpec((1, tcn), lambda i, j: (0, j)),
                  pl.BlockSpec((4 * tc, tc), lambda i, j: (0, 0))],
        out_specs=pl.BlockSpec((tc, tcn), lambda i, j: (i, j)),
        out_shape=jax.ShapeDtypeStruct((m, ch), BF16),
        compiler_params=_params("parallel", "arbitrary"),
        name=name,
    )(xbc, xbc, xbc, conv_w, conv_b.reshape(1, ch), shift)


def _ssd_kernel(xbc_ref, dtraw_ref, dtb_ref, alog_ref, tri_ref, *rest, direction, zero_init, nt):
    y_ref, fin_ref, st_ref = rest[-3:]
    if not zero_init:
        h0_ref = rest[0]
    t = pl.program_id(1)
    q = SSD_CHUNK
    hpg = SSD_HEADS // SSD_GROUPS

    @pl.when(t == 0)
    def _():
        for g in range(SSD_GROUPS):
            if zero_init:
                st_ref[g] = jnp.zeros((SSD_STATE, GROUP_W), F32)
            else:
                hg = h0_ref[0, 0, g * hpg:(g + 1) * hpg].reshape(GROUP_W, SSD_STATE)
                st_ref[g] = hg.T

    x = dtraw_ref[...] + dtb_ref[...]
    dt = jnp.maximum(x, 0.0) + jnp.log1p(jnp.exp(-jnp.abs(x)))
    a = dt * (-jnp.exp(alog_ref[...]) * LOG2_E)
    tri = tri_ref[...]
    a1 = a.astype(BF16)
    r1 = a - a1.astype(F32)
    a2 = r1.astype(BF16)
    a3 = (r1 - a2.astype(F32)).astype(BF16)
    cs = (jnp.dot(tri, a1, preferred_element_type=F32) + jnp.dot(tri, a2, preferred_element_type=F32)
          + jnp.dot(tri, a3, preferred_element_type=F32))
    tot = cs[q - 1:q, :] if direction == 0 else cs[0:1, :]
    cdec = jnp.exp2(tot)
    cs_t = cs.T
    dt_t = dt.T
    wend_t = (dt * jnp.exp2(tot - cs)).T
    li = lax.broadcasted_iota(jnp.int32, (q, q), 0)
    si = lax.broadcasted_iota(jnp.int32, (q, q), 1)
    mask = (li >= si) if direction == 0 else (li <= si)
    low = lax.broadcasted_iota(jnp.int32, (q, LANES), 1) < SSD_HEAD_DIM
    low_row = low[0:1, :]

    for g in range(SSD_GROUPS):
        bg = xbc_ref[:, SSD_D_INNER + g * SSD_STATE:SSD_D_INNER + (g + 1) * SSD_STATE]
        cg = xbc_ref[:, SSD_D_INNER + SSD_BC_DIM + g * SSD_STATE:SSD_D_INNER + SSD_BC_DIM + (g + 1) * SSD_STATE]
        cg32 = cg.astype(F32)
        bg_t = bg.astype(F32).T
        cb = lax.dot_general(cg, bg, (((1,), (1,)), ((), ())), preferred_element_type=F32)
        for p in range(hpg // 2):
            col0 = g * GROUP_W + p * LANES
            xp = xbc_ref[:, col0:col0 + LANES]
            zero = jnp.zeros_like(xp)
            x_lo = jnp.where(low, xp, zero)
            x_hi = jnp.where(low, zero, xp)
            s_in = st_ref[g, :, p * LANES:(p + 1) * LANES]
            s_bf = s_in.astype(BF16)
            s_lo = jnp.where(low, s_bf, zero)
            s_hi = jnp.where(low, zero, s_bf)
            c0 = direction * SSD_HEADS + g * hpg + 2 * p
            diag, offd, bw = [], [], []
            for c in (c0, c0 + 1):
                col = jnp.broadcast_to(cs[:, c:c + 1], (q, q))
                seg = col - cs_t[c:c + 1, :]
                diag.append((cb * jnp.exp2(jnp.where(mask, seg, -jnp.inf)) * dt_t[c:c + 1, :]).astype(BF16))
                offd.append((cg32 * jnp.exp2(col)).astype(BF16))
                bw.append((bg_t * wend_t[c:c + 1, :]).astype(BF16))
            lhs = jnp.concatenate(diag + offd, axis=1)
            rhs = jnp.concatenate([x_lo, x_hi, s_lo, s_hi], axis=0)
            y_ref[:, col0:col0 + LANES] = jnp.dot(lhs, rhs, preferred_element_type=F32).astype(y_ref.dtype)
            new = jnp.dot(jnp.concatenate(bw, axis=1), jnp.concatenate([x_lo, x_hi], axis=0),
                          preferred_element_type=F32)
            keep = jnp.where(low_row, cdec[:, c0:c0 + 1], cdec[:, c0 + 1:c0 + 2])
            st_ref[g, :, p * LANES:(p + 1) * LANES] = s_in * keep + new

    @pl.when(t == nt - 1)
    def _():
        for g in range(SSD_GROUPS):
            fin_ref[0, g * hpg:(g + 1) * hpg] = st_ref[g].T.reshape(hpg, SSD_HEAD_DIM, SSD_STATE)


def _ssd_scan_call(xbc_act, dt_raw, dt_bias, a_log, h0, y_into, fin_into, *, nb, seq, row_off, direction,
                   name):
    q = SSD_CHUNK
    nt = seq // q
    boff = row_off // q
    li = jnp.arange(q)[:, None]
    si = jnp.arange(q)[None, :]
    tri = ((li >= si) if direction == 0 else (li <= si)).astype(BF16)

    def chunk(b, t):
        return boff + b * nt + (t if direction == 0 else nt - 1 - t)

    zero_init = h0 is None
    in_specs = [pl.BlockSpec((q, SSD_CONV_CH), lambda b, t: (chunk(b, t), 0)),
                pl.BlockSpec((q, LANES), lambda b, t: (chunk(b, t), 0)),
                pl.BlockSpec((1, LANES), lambda b, t: (0, 0)),
                pl.BlockSpec((1, LANES), lambda b, t: (0, 0)),
                pl.BlockSpec((q, q), lambda b, t: (0, 0))]
    args = [xbc_act, dt_raw, dt_bias.reshape(1, LANES), a_log.reshape(1, LANES), tri]
    if not zero_init:
        in_specs.append(pl.BlockSpec((1, 1, SSD_HEADS, SSD_HEAD_DIM, SSD_STATE),
                                     lambda b, t: (b, direction, 0, 0, 0)))
        args.append(h0)
    aliases = {}
    if y_into is not None:
        in_specs.append(pl.BlockSpec(memory_space=pl.ANY))
        aliases = {len(args): 0}
        args.append(y_into)
    if fin_into is not None:
        in_specs.append(pl.BlockSpec(memory_space=pl.ANY))
        aliases[len(args)] = 1
        args.append(fin_into)
    return pl.pallas_call(
        functools.partial(_ssd_kernel, direction=direction, zero_init=zero_init, nt=nt),
        grid=(nb, nt),
        in_specs=in_specs,
        out_specs=[pl.BlockSpec((q, SSD_D_INNER), lambda b, t: (chunk(b, t), 0)),
                   pl.BlockSpec((1, None, SSD_HEADS, SSD_HEAD_DIM, SSD_STATE),
                                lambda b, t: (b, direction, 0, 0, 0))],
        out_shape=[jax.ShapeDtypeStruct((xbc_act.shape[0], SSD_D_INNER), BF16),
                   jax.ShapeDtypeStruct((nb, 2, SSD_HEADS, SSD_HEAD_DIM, SSD_STATE), F32)],
        scratch_shapes=[pltpu.VMEM((SSD_GROUPS, SSD_STATE, GROUP_W), F32)],
        input_output_aliases=aliases,
        compiler_params=_params("parallel", "arbitrary"),
        name=name,
    )(*args)


def _ssd_post_kernel(yf_ref, yb_ref, xs_ref, z_ref, d_ref, nw_ref, o_ref):
    z = z_ref[...].astype(F32)
    y = yf_ref[...].astype(F32) + yb_ref[...].astype(F32) + d_ref[...] * xs_ref[...].astype(F32)
    y = y * (z / (1.0 + jnp.exp(-z)))
    outs = []
    for g in range(SSD_GROUPS):
        yg = y[:, g * GROUP_W:(g + 1) * GROUP_W]
        ms = jnp.mean(yg * yg, axis=-1, keepdims=True)
        outs.append(yg * lax.rsqrt(ms + NORM_EPS) * nw_ref[:, g * GROUP_W:(g + 1) * GROUP_W])
    o_ref[...] = jnp.concatenate(outs, axis=1).astype(o_ref.dtype)


def _ssd_post_call(y_f, y_b, xbc_act, z, d_tot, norm_w, *, tm, name):
    m = y_f.shape[0]
    di = SSD_D_INNER
    row = pl.BlockSpec((tm, di), lambda i: (i, 0))
    vec = pl.BlockSpec((1, di), lambda i: (0, 0))
    return pl.pallas_call(
        _ssd_post_kernel,
        grid=(m // tm,),
        in_specs=[row, row, row, row, vec, vec],
        out_specs=row,
        out_shape=jax.ShapeDtypeStruct((m, di), BF16),
        compiler_params=_params("parallel"),
        name=name,
    )(y_f, y_b, xbc_act, z, d_tot, norm_w.reshape(1, di))


def _run_schedule(block_e, n_used):
    nblk = block_e.shape[0]
    idx = jnp.arange(nblk, dtype=jnp.int32)
    prev = jnp.concatenate([block_e[:1] - 1, block_e[:-1]])
    first = ((idx < n_used[0]) & (block_e != prev)).astype(jnp.int32)
    ridx = jnp.cumsum(first) - 1
    is_next = (first[None, :] == 1) & (ridx[None, :] == ridx[:, None] + 1)
    next_e = jnp.where(jnp.any(is_next, axis=1), jnp.sum(jnp.where(is_next, block_e[None, :], 0), axis=1), -1)
    counts = jnp.stack([n_used[0], jnp.sum(first)])
    return (block_e, counts.astype(jnp.int32), first, ridx.astype(jnp.int32), next_e.astype(jnp.int32))


def _weight_copies(w_hbm, stage, sems, layer, e, jj, slot, col_tiles, tn):
    return [pltpu.make_async_copy(w_hbm.at[layer, e, :, pl.ds(pl.multiple_of((off + jj) * tn, tn), tn)],
                                  stage.at[slot, wi], sems.at[slot, wi])
            for wi, off in enumerate(col_tiles)]


def _stage_run_weights(sched, w_hbm, stage, sems, w_bf, *, layer, col_tiles, tn, nj):
    be_ref, cnt_ref, first_ref, ridx_ref, next_ref = sched
    j = pl.program_id(0)
    i = pl.program_id(1)
    copies = functools.partial(_weight_copies, w_hbm, stage, sems, layer, col_tiles=col_tiles, tn=tn)

    @pl.when(first_ref[i] == 1)
    def _():
        slot = (j * cnt_ref[1] + ridx_ref[i]) & 1
        e = be_ref[i]

        @pl.when((i == 0) & (j == 0))
        def _():
            for c in copies(e, j, slot):
                c.start()

        for c in copies(e, j, slot):
            c.wait()
        def cast_rows(r, carry):
            rows = pl.ds(pl.multiple_of(r * CAST_ROWS, CAST_ROWS), CAST_ROWS)
            for wi in range(len(col_tiles)):
                w_bf[wi, rows, :] = stage[slot, wi, rows, :].astype(BF16)
            return carry

        lax.fori_loop(0, w_bf.shape[1] // CAST_ROWS, cast_rows, 0)
        nxt = next_ref[i]

        @pl.when(nxt >= 0)
        def _():
            for c in copies(nxt, j, 1 - slot):
                c.start()

        @pl.when((nxt < 0) & (j + 1 < nj))
        def _():
            for c in copies(be_ref[0], j + 1, 1 - slot):
                c.start()


def _moe_up_kernel(*refs, layer, tn, nj):
    sched, (x_ref, w_hbm, bg_ref, bu_ref, o_ref, stage, w_bf, sems) = refs[:5], refs[5:]
    i = pl.program_id(1)
    _stage_run_weights(sched, w_hbm, stage, sems, w_bf, layer=layer, col_tiles=(0, nj), tn=tn, nj=nj)
    nu_ref = sched[1]

    @pl.when(i < nu_ref[0])
    def _():
        x = x_ref[...]
        g = jnp.dot(x, w_bf[0], preferred_element_type=F32) + bg_ref[...]
        u = jnp.dot(x, w_bf[1], preferred_element_type=F32) + bu_ref[...]
        g = jnp.minimum(g, SWIGLU_LIMIT)
        u = jnp.clip(u, -SWIGLU_LIMIT, SWIGLU_LIMIT)
        act = (u + 1.0) * (g / (1.0 + jnp.exp2((-SWIGLU_ALPHA * LOG2_E) * g)))
        o_ref[...] = act.astype(o_ref.dtype)

    @pl.when(i >= nu_ref[0])
    def _():
        o_ref[...] = jnp.zeros_like(o_ref)


def _moe_down_kernel(*refs, layer, tn, nj):
    sched, (a_ref, w_hbm, b_ref), (o_ref, stage, w_bf, sems) = refs[:5], refs[5:8], refs[-4:]
    i = pl.program_id(1)
    _stage_run_weights(sched, w_hbm, stage, sems, w_bf, layer=layer, col_tiles=(0,), tn=tn, nj=nj)
    nu_ref = sched[1]

    @pl.when(i < nu_ref[0])
    def _():
        y = jnp.dot(a_ref[...], w_bf[0], preferred_element_type=F32) + b_ref[...]
        o_ref[...] = y.astype(o_ref.dtype)

    @pl.when(i >= nu_ref[0])
    def _():
        o_ref[...] = jnp.zeros_like(o_ref)


def _moe_experts(x_sorted, block_e, n_used, y_into, layer, w_gate_up, b_gate_up, w_down, b_down, *, blk_off,
                 total_slots, tn_up, tn_down, name):
    slots, d = x_sorted.shape
    nblk = slots // MOE_BLOCK
    nj = D_FF // tn_up
    nl = w_gate_up.shape[0]
    b_gu = b_gate_up.reshape(nl, N_EXPERTS, 1, 2 * D_FF)
    sched = _run_schedule(block_e, n_used)
    act = pl.pallas_call(
        functools.partial(_moe_up_kernel, layer=layer, tn=tn_up, nj=nj),
        grid_spec=pltpu.PrefetchScalarGridSpec(
            num_scalar_prefetch=len(sched),
            grid=(nj, nblk),
            in_specs=[pl.BlockSpec((MOE_BLOCK, d), lambda j, i, *s: (i, 0)),
                      pl.BlockSpec(memory_space=pl.ANY),
                      pl.BlockSpec((None, None, 1, tn_up), lambda j, i, be, *s: (layer, be[i], 0, j)),
                      pl.BlockSpec((None, None, 1, tn_up), lambda j, i, be, *s: (layer, be[i], 0, nj + j))],
            out_specs=pl.BlockSpec((MOE_BLOCK, tn_up), lambda j, i, *s: (i, j)),
            scratch_shapes=[pltpu.VMEM((2, 2, d, tn_up), F32), pltpu.VMEM((2, d, tn_up), BF16),
                            pltpu.SemaphoreType.DMA((2, 2))]),
        out_shape=jax.ShapeDtypeStruct((slots, D_FF), BF16),
        compiler_params=_params("arbitrary", "arbitrary"),
        name=name + "_up",
    )(*sched, x_sorted, w_gate_up, b_gu, b_gu)
    nj2 = d // tn_down
    in_specs = [pl.BlockSpec((MOE_BLOCK, D_FF), lambda j, i, *s: (i, 0)),
                pl.BlockSpec(memory_space=pl.ANY),
                pl.BlockSpec((None, None, 1, tn_down), lambda j, i, be, *s: (layer, be[i], 0, j))]
    args = [*sched, act, w_down, b_down.reshape(nl, N_EXPERTS, 1, d)]
    aliases = {}
    if y_into is not None:
        in_specs.append(pl.BlockSpec(memory_space=pl.ANY))
        aliases = {len(args): 0}
        args.append(y_into)
    return pl.pallas_call(
        functools.partial(_moe_down_kernel, layer=layer, tn=tn_down, nj=nj2),
        grid_spec=pltpu.PrefetchScalarGridSpec(
            num_scalar_prefetch=len(sched),
            grid=(nj2, nblk),
            in_specs=in_specs,
            out_specs=pl.BlockSpec((MOE_BLOCK, tn_down), lambda j, i, *s: (i + blk_off, j)),
            scratch_shapes=[pltpu.VMEM((2, 1, D_FF, tn_down), F32), pltpu.VMEM((1, D_FF, tn_down), BF16),
                            pltpu.SemaphoreType.DMA((2, 1))]),
        out_shape=jax.ShapeDtypeStruct((total_slots, d), BF16),
        input_output_aliases=aliases,
        compiler_params=_params("arbitrary", "arbitrary"),
        name=name + "_down",
    )(*args)


def _combine_kernel(*refs, next_norm):
    y_refs = refs[:TOP_K]
    gate_ref, x_ref, g2_ref = refs[TOP_K:TOP_K + 3]
    gates = gate_ref[...]
    f = None
    for k in range(TOP_K):
        term = gates[:, k:k + 1] * y_refs[k][...].astype(F32)
        f = term if f is None else f + term
    out = x_ref[...] + g2_ref[0] * f
    if next_norm:
        nw_ref, sh_ref, sc_ref = refs[TOP_K + 3:TOP_K + 6]
        o_ref, h_ref = refs[-2:]
        o_ref[...] = out
        h_ref[...] = _modnorm(out, nw_ref[...], sh_ref[0], sc_ref[0]).astype(h_ref.dtype)
    else:
        refs[-1][...] = out


def _combine_rows_call(y4, gate_pad, x, mod, out_into, next_norm, *, row_off, m, out_rows, out_off, n_p, ls, tm,
                       name):
    d = x.shape[1]
    nt = m // tm
    xoff = row_off // tm
    ooff = out_off // tm

    def mod_spec(which):
        return pl.BlockSpec((1, 1, d), lambda i, j: (_group_of_tile(i + xoff, tm, n_p, ls) * 6 + which, 0, 0))

    y_specs = [pl.BlockSpec((tm, d), functools.partial(lambda i, j, k: (k * nt + i, 0), k=k))
               for k in range(TOP_K)]
    in_specs = y_specs + [pl.BlockSpec((tm, LANES), lambda i, j: (i + xoff, 0)),
                          pl.BlockSpec((tm, d), lambda i, j: (i + xoff, 0)),
                          mod_spec(5)]
    args = [y4] * TOP_K + [gate_pad, x, mod]
    out_spec = pl.BlockSpec((tm, d), lambda i, j: (i + ooff, 0))
    out_specs, out_shape = [out_spec], [jax.ShapeDtypeStruct((out_rows, d), F32)]
    into = [out_into]
    if next_norm is not None:
        nw, mod_next, h_into = next_norm
        in_specs += [pl.BlockSpec((1, d), lambda i, j: (0, 0)), mod_spec(0), mod_spec(1)]
        args += [nw, mod_next, mod_next]
        out_specs.append(out_spec)
        out_shape.append(jax.ShapeDtypeStruct((out_rows, d), BF16))
        into.append(h_into)
    aliases = {}
    for oi, buf in enumerate(into):
        if buf is not None:
            in_specs.append(pl.BlockSpec(memory_space=pl.ANY))
            aliases[len(args)] = oi
            args.append(buf)
    res = pl.pallas_call(
        functools.partial(_combine_kernel, next_norm=next_norm is not None),
        grid=(nt, 1),
        in_specs=in_specs,
        out_specs=out_specs,
        out_shape=out_shape,
        input_output_aliases=aliases,
        compiler_params=_params("parallel", "arbitrary"),
        name=name,
    )(*args)
    return res if next_norm is not None else res[0]


def _moe_layer(x, mod, n_p, ls, layer, norm_w, w_router, b_router, w_gate_up, b_gate_up, w_down, b_down, *,
               split, name, next_norm=None):
    assert not (split and next_norm is not None)
    n, d = x.shape
    h, idx_pad, gate_pad, rank_pad, cnt = _router_call(x, norm_w.reshape(1, d), mod, n_p, ls, w_router,
                                                       b_router, tm=256, name=name + "_router")
    n_assign = n * TOP_K
    flat_e = idx_pad[:, :TOP_K].reshape(-1)
    counts = cnt[0, :N_EXPERTS]
    padded = (counts + MOE_BLOCK - 1) // MOE_BLOCK * MOE_BLOCK
    pad_end = jnp.cumsum(padded)
    pad_start = pad_end - padded
    dest = pad_start[flat_e] + rank_pad[:, :TOP_K].reshape(-1)
    n_blocks = -(-n_assign // MOE_BLOCK) + N_EXPERTS
    slots = n_blocks * MOE_BLOCK
    slot_tok = (jnp.arange(slots, dtype=jnp.int32) % n).at[dest].set(
        jnp.arange(n_assign, dtype=jnp.int32) // TOP_K, unique_indices=True, mode="promise_in_bounds")
    block_start = jnp.arange(n_blocks, dtype=jnp.int32) * MOE_BLOCK
    block_e = jnp.minimum(jnp.sum((pad_end[None, :] <= block_start[:, None]).astype(jnp.int32), axis=1),
                          N_EXPERTS - 1)
    n_used = (pad_end[-1] // MOE_BLOCK).astype(jnp.int32).reshape(1)
    cb = n_blocks // MOE_RANGES
    y_sorted = None
    for c in range(MOE_RANGES):
        rows = slice(c * cb * MOE_BLOCK, (c + 1) * cb * MOE_BLOCK)
        x_sorted = h.at[slot_tok[rows]].get(mode="promise_in_bounds")
        used = jnp.clip(n_used - c * cb, 0, cb)
        y_sorted = _moe_experts(x_sorted, block_e[c * cb:(c + 1) * cb], used, y_sorted, layer, w_gate_up,
                                b_gate_up, w_down, b_down, blk_off=c * cb, total_slots=slots, tn_up=1024,
                                tn_down=2048, name="%s_r%d" % (name, c))
    n_s = n - n_p
    ranges = [(0, n_p), (n_p, n_s // 2), (n_p + n_s // 2, n_s - n_s // 2)]
    dest_tk = dest.reshape(n, TOP_K)
    outs = [None, None]
    h_next = None
    for ri, (t0, m) in enumerate(ranges):
        idx = dest_tk[t0:t0 + m].T.reshape(-1)
        y4 = y_sorted.at[idx].get(mode="promise_in_bounds")
        if split:
            which, out_rows, out_off = (0, n_p, 0) if ri == 0 else (1, n_s, t0 - n_p)
        else:
            which, out_rows, out_off = 0, n, t0
        nn = None if next_norm is None else (next_norm[0], next_norm[1], h_next)
        res = _combine_rows_call(y4, gate_pad, x, mod, outs[which], nn, row_off=t0, m=m, out_rows=out_rows,
                                 out_off=out_off, n_p=n_p, ls=ls, tm=512, name="%s_combine%d" % (name, ri))
        if next_norm is None:
            outs[which] = res
        else:
            outs[which], h_next = res
    if next_norm is not None:
        return outs[0], h_next
    return (outs[0], outs[1]) if split else outs[0]


def _ada_call(cond, w_ada, b_ada, layer, *, name):
    g = cond.shape[0]
    a = jnp.pad(jax.nn.silu(cond), ((0, 16 - g), (0, 0))).astype(BF16)
    m = _matmul(a, w_ada, n_out=6 * D_MODEL, tm=16, tn=1024, out_dtype=F32, name=name, layer=layer,
                bias=b_ada.reshape(b_ada.shape[0], 1, 6 * D_MODEL))
    return m[:g].reshape(g * 6, 1, D_MODEL)


def kernel(x_prompt, x_sample, c, c_ctx, cache_k, cache_v, state_ssm, norm1_w, norm2_w, w_ada, b_ada, w_qkv, q_norm_w, k_norm_w, lambda_q1, lambda_k1, lambda_q2, lambda_k2, subln_w, w_o, w_in_ssd, conv_w, conv_b, dt_bias, a_log, d_skip, ssd_norm_w, w_out_ssd, w_router, b_router, w_gate_up, b_gate_up, w_down, b_down):
    bp, lp, d = x_prompt.shape
    bs, ls, _ = x_sample.shape
    past = cache_k.shape[2]
    n_p, n_s = bp * lp, bs * ls
    xp0, xs0 = x_prompt.reshape(n_p, d), x_sample.reshape(n_s, d)
    cond = jnp.concatenate([c_ctx[None], c], axis=0)
    resid_of = lambda xx, mod, which: (xx, mod, which, n_p, ls)

    n = n_p + n_s
    mod = _ada_call(cond, w_ada, b_ada, 0, name="ada0")
    wq = w_qkv[0].astype(BF16)
    qk_w = jnp.stack([q_norm_w[0], k_norm_w[0]], axis=0)
    lam_init = 0.8 - 0.6 * math.exp(-0.3 * 0)
    lam_params = jnp.stack([lambda_q1[0], lambda_k1[0], lambda_q2[0], lambda_k2[0]], axis=0)
    qp, kp, vp = _qkv_call(xp0, norm1_w[0], mod, wq, qk_w, None, row_off=0, n_p=n_p, kv_dtype=F32, tm=512,
                           tn=512, ls=ls, name="qkv_prompt")
    qs, ks, vs = _qkv_call(xs0, norm1_w[0], mod, wq, qk_w, _rope_tables(ls), row_off=n_p, n_p=n_p,
                           kv_dtype=BF16, tm=512, tn=512, ls=ls, name="qkv_sample")
    o = _attn_call(qp, [(kp, vp, lp)], lam_params, subln_w[0], None, n_rows=n, row_off=0, nb=bp, lq=lp,
                   tq=lp, lam_init=lam_init, name="attn_prompt")
    ck = cache_k[:, 0].reshape(bs * past, d)
    cv = cache_v[:, 0].reshape(bs * past, d)
    o = _attn_call(qs, [(ck, cv, past), (ks, vs, ls)], lam_params, subln_w[0], o, n_rows=n, row_off=n_p,
                   nb=bs, lq=ls, tq=512, lam_init=lam_init, name="attn_sample")
    wo = w_o[0].astype(BF16)
    x = _resid_rows_matmul(o, wo, xp0, mod, 2, None, row_off=0, n_p=n_p, ls=ls, tm=1024, tn=512,
                           name="attn_out_prompt")
    x = _resid_rows_matmul(o, wo, xs0, mod, 2, x, row_off=n_p, n_p=n_p, ls=ls, tm=1024, tn=512,
                           name="attn_out_sample")
    mod1 = _ada_call(cond, w_ada, b_ada, 1, name="ada1")
    x, h = _moe_layer(x, mod, n_p, ls, 0, norm2_w[0], w_router[0], b_router[0], w_gate_up, b_gate_up,
                      w_down, b_down, split=False, name="moe0", next_norm=(norm1_w[1].reshape(1, d), mod1))
    new_k = kp.reshape(bp, 1, lp, ATTN_HEADS, 2, QK_DIM)
    new_v = vp.reshape(bp, 1, lp, ATTN_HEADS, V_DIM)

    mod = mod1
    w_in = w_in_ssd[0].astype(BF16)
    z = _matmul(h, w_in, n_out=SSD_D_INNER, tm=1024, tn=512, out_dtype=BF16, name="ssd_in_z")
    xbc = _matmul(h, w_in, n_out=SSD_CONV_CH, col_off=SSD_D_INNER, tm=1024, tn=512, out_dtype=BF16,
                  name="ssd_in_xbc")
    dt_raw = _matmul(h, w_in, n_out=2 * SSD_HEADS, col_off=SSD_D_INNER + SSD_CONV_CH, tm=1024, tn=LANES,
                     out_dtype=F32, name="ssd_in_dt")
    xbc_act = _conv_call(xbc, conv_w[0], conv_b[0], n_p=n_p, lp=lp, ls=ls, tc=256, tcn=2048, name="ssd_conv")
    ys, fin = [], None
    for direction in (0, 1):
        y, fin = _ssd_scan_call(xbc_act, dt_raw, dt_bias[0], a_log[0], None, None, fin, nb=bp, seq=lp,
                                row_off=0, direction=direction, name="ssd_scan_prompt%d" % direction)
        y, _ = _ssd_scan_call(xbc_act, dt_raw, dt_bias[0], a_log[0], state_ssm[:, 0], y, None, nb=bs, seq=ls,
                              row_off=n_p, direction=direction, name="ssd_scan_sample%d" % direction)
        ys.append(y)
    d_tot = jnp.repeat(d_skip[0, 0] + d_skip[0, 1], SSD_HEAD_DIM).reshape(1, SSD_D_INNER)
    yn = _ssd_post_call(ys[0], ys[1], xbc_act, z, d_tot, ssd_norm_w[0], tm=256, name="ssd_post")
    x = _matmul(yn, w_out_ssd[0].astype(BF16), n_out=d, tm=1024, tn=512, out_dtype=F32, name="ssd_out",
                resid=resid_of(x, mod, 2))
    xp, xs = _moe_layer(x, mod, n_p, ls, 1, norm2_w[1], w_router[1], b_router[1], w_gate_up, b_gate_up,
                        w_down, b_down, split=True, name="moe1")
    new_s = fin[:, None]
    return (xp.reshape(bp, lp, d), xs.reshape(bs, ls, d), new_k, new_v, new_s)
```
